```python
import jax, jax.numpy as jnp
from jax import lax
import numpy as np

D_MODEL = 1024
BATCH = 8
SEQ = 2048
DEPTH = 1
DEC_BATCH = 128
DEC_SEQ = 4
PAST_LEN = 2048
PAGE_SIZE = 128

HEAD_DIM = 64
H_A = 8
HKV_A = 1
G_A = H_A // HKV_A
H_IDX = 8
D_IDX = 64
TOPK_A_MAX = 256
H_B = 8
HKV_B = 2
G_B = H_B // HKV_B
MOBA_BLOCK = 256
TOPK_B = 3
D_FF = 2752
ROPE_THETA = 10000.0
EPS = 1e-6
Q_BLOCK = 128

W_QA = H_A * HEAD_DIM
W_KA = HKV_A * HEAD_DIM
W_VA = HKV_A * HEAD_DIM
W_QI = H_IDX * D_IDX
W_KI = D_IDX
W_WI = H_IDX
W_QB = H_B * HEAD_DIM
W_KB = HKV_B * HEAD_DIM
W_VB = HKV_B * HEAD_DIM
N_IN = W_QA + W_KA + W_VA + W_QI + W_KI + W_WI + W_QB + W_KB + W_VB + 2 * D_MODEL

kernel_name = "hybrid_dsa_moba_macaron_step"


def rms_norm(x, g):
    xf = x.astype(jnp.float32)
    y = xf * lax.rsqrt(jnp.mean(xf * xf, axis=-1, keepdims=True) + EPS)
    return (y * g.astype(jnp.float32)).astype(x.dtype)


def rope(x, pos):
    half = x.shape[-1] // 2
    inv = ROPE_THETA ** (-jnp.arange(half, dtype=jnp.float32) / half)
    ang = pos.astype(jnp.float32)[:, None] * inv[None, :]
    cos = jnp.cos(ang)[:, None, :]
    sin = jnp.sin(ang)[:, None, :]
    xf = x.astype(jnp.float32)
    x1, x2 = xf[..., :half], xf[..., half:]
    return jnp.concatenate([x1 * cos - x2 * sin, x2 * cos + x1 * sin], axis=-1).astype(x.dtype)


def swiglu(x, w_in, w_out):
    hu = x @ w_in
    g, u = hu[..., :D_FF], hu[..., D_FF:]
    return (jax.nn.silu(g) * u) @ w_out


def split_cols(z):
    sizes = (W_QA, W_KA, W_VA, W_QI, W_KI, W_WI, W_QB, W_KB, W_VB, D_MODEL, D_MODEL)
    outs, start = [], 0
    for s in sizes:
        outs.append(z[..., start:start + s])
        start += s
    return outs


def project(h, pos, w_in, q_norm_a, k_norm_a, q_norm_b, k_norm_b, gate_bias):
    b, t, _ = h.shape
    qa, ka, va, qi, ki, wi, qb, kb, vb, ga, gb = split_cols(h @ w_in)
    return {
        "qa": rope(rms_norm(qa.reshape(b, t, H_A, HEAD_DIM), q_norm_a), pos),
        "ka": rope(rms_norm(ka.reshape(b, t, HKV_A, HEAD_DIM), k_norm_a), pos),
        "va": va.reshape(b, t, HKV_A, HEAD_DIM),
        "qi": rope(qi.reshape(b, t, H_IDX, D_IDX), pos),
        "ki": rope(ki.reshape(b, t, 1, D_IDX), pos).reshape(b, t, D_IDX),
        "wi": wi * (H_IDX ** -0.5),
        "qb": rope(rms_norm(qb.reshape(b, t, H_B, HEAD_DIM), q_norm_b), pos),
        "kb": rope(rms_norm(kb.reshape(b, t, HKV_B, HEAD_DIM), k_norm_b), pos),
        "vb": vb.reshape(b, t, HKV_B, HEAD_DIM),
        "ga": ga + gate_bias[:D_MODEL],
        "gb": gb + gate_bias[D_MODEL:],
    }


def dsa_attend(qa, qi, wi, pos, ka, va, ki, k_sel):
    n_q = qa.shape[0]
    n_keys = ka.shape[0]
    s = jnp.einsum('qhd,sd->qhs', qi.astype(jnp.float32), ki.astype(jnp.float32)) * (D_IDX ** -0.5)
    score = jnp.einsum('qh,qhs->qs', wi.astype(jnp.float32), jax.nn.relu(s))
    key_pos = jnp.arange(n_keys)
    score = jnp.where(key_pos[None, :] <= pos[:, None], score, -jnp.inf)
    _, sel = lax.top_k(score, k_sel)
    valid = sel <= pos[:, None]
    ks = ka[sel]
    vs = va[sel]
    qg = qa.reshape(n_q, HKV_A, G_A, HEAD_DIM)
    logits = jnp.einsum('qngd,qknd->qngk', qg, ks).astype(jnp.float32) * (HEAD_DIM ** -0.5)
    logits = jnp.where(valid[:, None, None, :], logits, -jnp.inf)
    p = jax.nn.softmax(logits, axis=-1).astype(vs.dtype)
    return jnp.einsum('qngk,qknd->qngd', p, vs).reshape(n_q, H_A * HEAD_DIM)


def moba_attend(qb, pos, kb_blocks, vb_blocks, kb_mean):
    n_q = qb.shape[0]
    n_blk = kb_blocks.shape[0]
    qg = qb.reshape(n_q, HKV_B, G_B, HEAD_DIM)
    own = pos // MOBA_BLOCK
    gate = jnp.einsum('qngd,jnd->qngj', qg.astype(jnp.float32), kb_mean)
    past = jnp.arange(n_blk)[None, :] < own[:, None]
    gate = jnp.where(past[:, None, None, :], gate, -jnp.inf)
    n_top = min(TOPK_B, n_blk)
    _, sel = lax.top_k(gate, n_top)
    sel_ok = sel < own[:, None, None, None]
    own_b = jnp.broadcast_to(own[:, None, None, None], sel.shape[:-1] + (1,))
    blocks = jnp.concatenate([sel, own_b.astype(sel.dtype)], axis=-1)
    blk_ok = jnp.concatenate([sel_ok, jnp.ones(own_b.shape, dtype=bool)], axis=-1)
    kvh = jnp.arange(HKV_B)[None, :, None, None]
    ks = kb_blocks.transpose(2, 0, 1, 3)[kvh, blocks]
    vs = vb_blocks.transpose(2, 0, 1, 3)[kvh, blocks]
    key_pos = blocks[..., None] * MOBA_BLOCK + jnp.arange(MOBA_BLOCK)
    mask = blk_ok[..., None] & (key_pos <= pos[:, None, None, None, None])
    logits = jnp.einsum('qngd,qngjkd->qngjk', qg, ks).astype(jnp.float32) * (HEAD_DIM ** -0.5)
    logits = jnp.where(mask, logits, -jnp.inf)
    shp = logits.shape
    p = jax.nn.softmax(logits.reshape(shp[:3] + (-1,)), axis=-1).reshape(shp).astype(vs.dtype)
    return jnp.einsum('qngjk,qngjkd->qngd', p, vs).reshape(n_q, H_B * HEAD_DIM)


def seq_attend(qa, qi, wi, qb, pos, ka, va, ki, kb, vb, k_sel):
    n_keys = ka.shape[0]
    n_blk = -(-n_keys // MOBA_BLOCK)
    pad = n_blk * MOBA_BLOCK - n_keys
    kb_blocks = jnp.pad(kb, ((0, pad), (0, 0), (0, 0))).reshape(n_blk, MOBA_BLOCK, HKV_B, HEAD_DIM)
    vb_blocks = jnp.pad(vb, ((0, pad), (0, 0), (0, 0))).reshape(n_blk, MOBA_BLOCK, HKV_B, HEAD_DIM)
    kb_mean = jnp.mean(kb_blocks.astype(jnp.float32), axis=1)
    n_q = qa.shape[0]
    qblk = Q_BLOCK if n_q % Q_BLOCK == 0 else n_q
    nqb = n_q // qblk

    def blk(a):
        return a.reshape((nqb, qblk) + a.shape[1:])

    def step(args):
        qa_, qi_, wi_, qb_, pos_ = args
        return (dsa_attend(qa_, qi_, wi_, pos_, ka, va, ki, k_sel),
                moba_attend(qb_, pos_, kb_blocks, vb_blocks, kb_mean))

    oa, ob = lax.map(step, (blk(qa), blk(qi), blk(wi), blk(qb), blk(pos)))
    return oa.reshape(n_q, H_A * HEAD_DIM), ob.reshape(n_q, H_B * HEAD_DIM)


def mix_group(t, ka, va, ki, kb, vb, pos, k_sel):
    def per_seq(args):
        qa, qi, wi, qb, ka_, va_, ki_, kb_, vb_ = args
        return seq_attend(qa, qi, wi, qb, pos, ka_, va_, ki_, kb_, vb_, k_sel)
    return lax.map(per_seq, (t["qa"], t["qi"], t["wi"], t["qb"], ka, va, ki, kb, vb))


def merge(t, oa, ob, w_up_a, w_up_b, w_out):
    m = jax.nn.sigmoid(t["ga"]) * (oa @ w_up_a) + jax.nn.sigmoid(t["gb"]) * (ob @ w_up_b)
    return m @ w_out


def gather_pages(pool, page_table):
    g = pool[page_table]
    return g.reshape((page_table.shape[0], -1) + pool.shape[2:])


def setup_inputs(seed: int = 0) -> dict:
    key = jax.random.key(seed)
    ks = jax.random.split(key, 32)
    n_pages = PAST_LEN // PAGE_SIZE
    n_used = DEC_BATCH * n_pages
    n_pool = n_used + n_used // 4

    def nrm(k, shape, scale=1.0):
        return jax.random.normal(k, shape, jnp.float32) * scale

    def gain(k, shape):
        return 1.0 + 0.01 * jax.random.normal(k, shape, jnp.float32)

    page_table = jax.random.permutation(ks[7], n_pool)[:n_used].reshape(DEC_BATCH, n_pages).astype(jnp.int32)
    return {
        "x_prompt": nrm(ks[0], (BATCH, SEQ, D_MODEL)),
        "x_sample": nrm(ks[1], (DEC_BATCH, DEC_SEQ, D_MODEL)),
        "cache_k_a": nrm(ks[2], (DEPTH, n_pool, PAGE_SIZE, HKV_A, HEAD_DIM)),
        "cache_v_a": nrm(ks[3], (DEPTH, n_pool, PAGE_SIZE, HKV_A, HEAD_DIM)),
        "cache_kidx_a": nrm(ks[4], (DEPTH, n_pool, PAGE_SIZE, D_IDX)),
        "cache_k_b": nrm(ks[5], (DEPTH, n_pool, PAGE_SIZE, HKV_B, HEAD_DIM)),
        "cache_v_b": nrm(ks[6], (DEPTH, n_pool, PAGE_SIZE, HKV_B, HEAD_DIM)),
        "page_table": page_table,
        "ffn1_norm": gain(ks[8], (DEPTH, D_MODEL)),
        "ffn1_w_in": nrm(ks[9], (DEPTH, D_MODEL, 2 * D_FF), D_MODEL ** -0.5),
        "ffn1_w_out": nrm(ks[10], (DEPTH, D_FF, D_MODEL), D_FF ** -0.5),
        "mix_norm": gain(ks[11], (DEPTH, D_MODEL)),
        "w_in": nrm(ks[12], (DEPTH, D_MODEL, N_IN), D_MODEL ** -0.5),
        "q_norm_a": gain(ks[13], (DEPTH, HEAD_DIM)),
        "k_norm_a": gain(ks[14], (DEPTH, HEAD_DIM)),
        "q_norm_b": gain(ks[15], (DEPTH, HEAD_DIM)),
        "k_norm_b": gain(ks[16], (DEPTH, HEAD_DIM)),
        "gate_bias": nrm(ks[17], (DEPTH, 2 * D_MODEL), 0.01),
        "w_up_a": nrm(ks[18], (DEPTH, H_A * HEAD_DIM, D_MODEL), (H_A * HEAD_DIM) ** -0.5),
        "w_up_b": nrm(ks[19], (DEPTH, H_B * HEAD_DIM, D_MODEL), (H_B * HEAD_DIM) ** -0.5),
        "w_out": nrm(ks[20], (DEPTH, D_MODEL, D_MODEL), D_MODEL ** -0.5),
        "ffn2_norm": gain(ks[21], (DEPTH, D_MODEL)),
        "ffn2_w_in": nrm(ks[22], (DEPTH, D_MODEL, 2 * D_FF), D_MODEL ** -0.5),
        "ffn2_w_out": nrm(ks[23], (DEPTH, D_FF, D_MODEL), D_FF ** -0.5),
    }


def reference(x_prompt, x_sample, cache_k_a, cache_v_a, cache_kidx_a, cache_k_b, cache_v_b, page_table,
              ffn1_norm, ffn1_w_in, ffn1_w_out, mix_norm, w_in, q_norm_a, k_norm_a, q_norm_b, k_norm_b,
              gate_bias, w_up_a, w_up_b, w_out, ffn2_norm, ffn2_w_in, ffn2_w_out):
    seq = x_prompt.shape[1]
    n_new = x_sample.shape[1]
    past_len = page_table.shape[1] * cache_k_a.shape[2]
    pos_p = jnp.arange(seq, dtype=jnp.int32)
    pos_s = past_len + jnp.arange(n_new, dtype=jnp.int32)
    k_sel_p = min(TOPK_A_MAX, seq // 4)
    k_sel_s = min(TOPK_A_MAX, (past_len + n_new) // 4)

    names = ("ka", "va", "ki", "kb", "vb")
    rows_p = {n: [] for n in names}
    rows_s = {n: [] for n in names}
    xp, xs = x_prompt, x_sample
    for l in range(DEPTH):
        xp = xp + 0.5 * swiglu(rms_norm(xp, ffn1_norm[l]), ffn1_w_in[l], ffn1_w_out[l])
        xs = xs + 0.5 * swiglu(rms_norm(xs, ffn1_norm[l]), ffn1_w_in[l], ffn1_w_out[l])

        tp = project(rms_norm(xp, mix_norm[l]), pos_p, w_in[l], q_norm_a[l], k_norm_a[l],
                     q_norm_b[l], k_norm_b[l], gate_bias[l])
        ts = project(rms_norm(xs, mix_norm[l]), pos_s, w_in[l], q_norm_a[l], k_norm_a[l],
                     q_norm_b[l], k_norm_b[l], gate_bias[l])

        oa, ob = mix_group(tp, tp["ka"], tp["va"], tp["ki"], tp["kb"], tp["vb"], pos_p, k_sel_p)
        xp = xp + merge(tp, oa, ob, w_up_a[l], w_up_b[l], w_out[l])

        def with_past(pool, new):
            return jnp.concatenate([gather_pages(pool[l], page_table).astype(new.dtype), new], axis=1)

        oa, ob = mix_group(ts, with_past(cache_k_a, ts["ka"]), with_past(cache_v_a, ts["va"]),
                           with_past(cache_kidx_a, ts["ki"]), with_past(cache_k_b, ts["kb"]),
                           with_past(cache_v_b, ts["vb"]), pos_s, k_sel_s)
        xs = xs + merge(ts, oa, ob, w_up_a[l], w_up_b[l], w_out[l])

        xp = xp + 0.5 * swiglu(rms_norm(xp, ffn2_norm[l]), ffn2_w_in[l], ffn2_w_out[l])
        xs = xs + 0.5 * swiglu(rms_norm(xs, ffn2_norm[l]), ffn2_w_in[l], ffn2_w_out[l])

        for n in names:
            rows_p[n].append(tp[n])
            rows_s[n].append(ts[n])

    return (xp, xs,
            jnp.stack(rows_p["ka"]), jnp.stack(rows_p["va"]), jnp.stack(rows_p["ki"]),
            jnp.stack(rows_p["kb"]), jnp.stack(rows_p["vb"]),
            jnp.stack(rows_s["ka"]), jnp.stack(rows_s["va"]), jnp.stack(rows_s["ki"]),
            jnp.stack(rows_s["kb"]), jnp.stack(rows_s["vb"]))
```

```python
import functools

import numpy as np
import jax
import jax.numpy as jnp
from jax import lax
from jax.experimental import pallas as pl
from jax.experimental.pallas import tpu as pltpu

F32 = jnp.float32
BF16 = jnp.bfloat16
I32 = jnp.int32

D_MODEL = 1024
HEAD_DIM = 64
N_HEADS = 8
HKV_B = 2
G_B = N_HEADS // HKV_B
D_FF = 2752
TOPK_A = 256
MOBA_BLOCK = 256
TOPK_B = 3
PAGE = 128
ROPE_THETA = 10000.0
EPS = 1e-6

LANES = 128
D_FF_PAD = 2816
FF_CHUNK = 512
W_Q = N_HEADS * HEAD_DIM

A_W = 1280
B_W = 640
C_W = 256
G_W = 2 * D_MODEL
P_W = A_W + B_W + C_W + G_W

INT_MIN = np.int32(-2 ** 31)
NEG_INF = float("-inf")
VMEM_LIMIT = 56 * 1024 * 1024


def _dot(a, b):
    return jnp.dot(a, b, preferred_element_type=F32)


def _dot_nt(a, b):
    return lax.dot_general(a, b, (((1,), (1,)), ((), ())), preferred_element_type=F32)


def _split_bf16(x):
    hi = x.astype(BF16)
    lo = (x - hi.astype(F32)).astype(BF16)
    return hi, lo


def _rms(x, g):
    ms = jnp.mean(x * x, axis=-1, keepdims=True)
    return x * lax.rsqrt(ms + EPS) * g


def _const_spec(shape):
    nd = len(shape)
    return pl.BlockSpec(shape, lambda *_: (0,) * nd)


def _ffn_kernel(x_ref, g_ref, wg_ref, wu_ref, wo_ref, o_ref):
    x = x_ref[...]
    h = _rms(x, g_ref[...]).astype(BF16)
    acc = jnp.zeros_like(x)
    for s in range(0, D_FF_PAD, FF_CHUNK):
        e = min(s + FF_CHUNK, D_FF_PAD)
        g = _dot(h, wg_ref[:, s:e])
        u = _dot(h, wu_ref[:, s:e])
        a = (g * jax.nn.sigmoid(g) * u).astype(BF16)
        acc = acc + _dot(a, wo_ref[s:e, :])
    o_ref[...] = x + 0.5 * acc


def _ffn(x, g, wg, wu, wo, tm):
    t = x.shape[0]
    return pl.pallas_call(
        _ffn_kernel,
        grid=(t // tm,),
        in_specs=[
            pl.BlockSpec((tm, D_MODEL), lambda i: (i, 0)),
            _const_spec((1, D_MODEL)),
            _const_spec((D_MODEL, D_FF_PAD)),
            _const_spec((D_MODEL, D_FF_PAD)),
            _const_spec((D_FF_PAD, D_MODEL)),
        ],
        out_specs=pl.BlockSpec((tm, D_MODEL), lambda i: (i, 0)),
        out_shape=jax.ShapeDtypeStruct((t, D_MODEL), F32),
        compiler_params=pltpu.CompilerParams(
            dimension_semantics=("arbitrary",), vmem_limit_bytes=VMEM_LIMIT),
        name="ffn",
    )(x, g, wg, wu, wo)


def _rope(y, cos, sin, first_half):
    r_lo = pltpu.roll(y, 32, 1)
    r_hi = pltpu.roll(y, 96, 1)
    return y * cos + jnp.where(first_half, r_hi, r_lo) * sin


def _proj_kernel(x_ref, gm_ref, wp_ref, gain_ref, bias_ref, cos_ref, sin_ref, bd_ref,
                 qa_ref, qb_ref, qi_ref, kb_ref, ka_ref, ki_ref, vb_ref, va_ref, misc_ref,
                 sga_ref, sgb_ref, *, transposed):
    x = x_ref[...]
    tm = x.shape[0]
    h = _rms(x, gm_ref[...]).astype(BF16)
    cos = cos_ref[...]
    sin = sin_ref[...]
    bd = bd_ref[...]
    lane = lax.broadcasted_iota(I32, (tm, LANES), 1)
    first_half = (lane % HEAD_DIM) < (HEAD_DIM // 2)

    z_a = _dot(h, wp_ref[:, 0:A_W])
    tiles_a = []
    for j in range(A_W // LANES):
        z = z_a[:, j * LANES:(j + 1) * LANES]
        hi, lo = _split_bf16(z * z)
        ms = (_dot(hi, bd) + _dot(lo, bd)) * (1.0 / HEAD_DIM)
        y = z * lax.rsqrt(ms + EPS) * gain_ref[:, j * LANES:(j + 1) * LANES]
        tiles_a.append(_rope(y, cos, sin, first_half))

    z_b = _dot(h, wp_ref[:, A_W:A_W + B_W])
    tiles_b = [_rope(z_b[:, j * LANES:(j + 1) * LANES], cos, sin, first_half)
               for j in range(B_W // LANES)]

    z_c = _dot(h, wp_ref[:, A_W + B_W:A_W + B_W + C_W])
    z_g = _dot(h, wp_ref[:, A_W + B_W + C_W:P_W]) + bias_ref[...]

    scale = HEAD_DIM ** -0.5
    qa = jnp.concatenate(tiles_a[0:4], axis=1) * scale
    qb = jnp.concatenate(tiles_a[4:8], axis=1) * scale
    qi = jnp.concatenate(tiles_b[0:4], axis=1) * scale
    if transposed:
        qa_ref[...] = qa.T.astype(BF16)
        qb_ref[...] = qb.T.astype(BF16)
        qi_ref[...] = qi.T.astype(BF16)
        misc_ref[...] = z_c.T
    else:
        qa_ref[...] = qa.astype(BF16)
        qb_ref[...] = qb.astype(BF16)
        qi_ref[...] = qi.astype(BF16)
        misc_ref[...] = z_c
    kb_ref[...] = tiles_a[8]
    ka_ref[...] = tiles_a[9][:, 0:HEAD_DIM]
    ki_ref[...] = tiles_b[4][:, 0:HEAD_DIM]
    vb_ref[...] = z_c[:, 0:LANES]
    va_ref[...] = z_c[:, LANES:LANES + HEAD_DIM]
    sg = jax.nn.sigmoid(z_g)
    sga_ref[...] = sg[:, 0:D_MODEL]
    sgb_ref[...] = sg[:, D_MODEL:G_W]


def _proj(x, gm, wp, gain, bias, cos_t, sin_t, bd, tm, transposed):
    t = x.shape[0]
    n_tab = cos_t.shape[0] // tm
    row = lambda w: pl.BlockSpec((tm, w), lambda i: (i, 0))
    col = lambda w: pl.BlockSpec((w, tm), lambda i: (0, i))
    qspec, mspec = (col(W_Q), col(C_W)) if transposed else (row(W_Q), row(C_W))
    qshape = (W_Q, t) if transposed else (t, W_Q)
    mshape = (C_W, t) if transposed else (t, C_W)
    out_shape = (
        jax.ShapeDtypeStruct(qshape, BF16), jax.ShapeDtypeStruct(qshape, BF16),
        jax.ShapeDtypeStruct(qshape, BF16),
        jax.ShapeDtypeStruct((t, 2 * HEAD_DIM), F32), jax.ShapeDtypeStruct((t, HEAD_DIM), F32),
        jax.ShapeDtypeStruct((t, HEAD_DIM), F32),
        jax.ShapeDtypeStruct((t, 2 * HEAD_DIM), F32), jax.ShapeDtypeStruct((t, HEAD_DIM), F32),
        jax.ShapeDtypeStruct(mshape, F32),
        jax.ShapeDtypeStruct((t, D_MODEL), F32), jax.ShapeDtypeStruct((t, D_MODEL), F32),
    )
    out_specs = (qspec, qspec, qspec, row(2 * HEAD_DIM), row(HEAD_DIM), row(HEAD_DIM),
                 row(2 * HEAD_DIM), row(HEAD_DIM), mspec, row(D_MODEL), row(D_MODEL))
    return pl.pallas_call(
        functools.partial(_proj_kernel, transposed=transposed),
        grid=(t // tm,),
        in_specs=[
            row(D_MODEL),
            _const_spec((1, D_MODEL)),
            _const_spec((D_MODEL, P_W)),
            _const_spec((1, A_W)),
            _const_spec((1, G_W)),
            pl.BlockSpec((tm, LANES), lambda i: (i % n_tab, 0)),
            pl.BlockSpec((tm, LANES), lambda i: (i % n_tab, 0)),
            _const_spec((LANES, LANES)),
        ],
        out_specs=out_specs,
        out_shape=out_shape,
        compiler_params=pltpu.CompilerParams(
            dimension_semantics=("arbitrary",), vmem_limit_bytes=VMEM_LIMIT),
        name="proj",
    )(x, gm, wp, gain, bias, cos_t, sin_t, bd)


def _sortable(score):
    b = lax.bitcast_convert_type(score, I32)
    k = b ^ ((b >> 31) & np.int32(0x7FFFFFFF))
    return jnp.where(score == 0.0, np.int32(0), k)


def _count(mask, axis):
    return jnp.sum(jnp.where(mask, 1.0, 0.0), axis=axis, keepdims=True)


def _topk_bias(load_keys, key_index, axis, vec_shape, k_sel, n_index_bits):
    one = np.int32(1)

    def value_bit(it, ans):
        cand = ans | lax.shift_left(one, 31 - it)
        thr = cand ^ INT_MIN
        return jnp.where(_count(load_keys() >= thr, axis) >= k_sel, cand, ans)

    ans = lax.fori_loop(0, 32, value_bit, jnp.zeros(vec_shape, I32))
    thr = ans ^ INT_MIN
    need = k_sel - _count(load_keys() > thr, axis)

    def index_bit(it, cut):
        cand = cut | lax.shift_left(one, n_index_bits - 1 - it)
        hit = jnp.where(load_keys() == thr, jnp.where(key_index() < cand, 1.0, 0.0), 0.0)
        return jnp.where(jnp.sum(hit, axis=axis, keepdims=True) < need, cand, cut)

    cut = lax.fori_loop(0, n_index_bits, index_bit, jnp.zeros(vec_shape, I32))
    keys = load_keys()
    keep_eq = jnp.where(key_index() <= cut, 0.0, NEG_INF)
    return jnp.where(keys > thr, 0.0, jnp.where(keys == thr, keep_eq, NEG_INF))


def _softmax_pv_t(lg, v_t):
    m = jnp.max(lg, axis=0, keepdims=True)
    p = jnp.exp(lg - m)
    l = jnp.sum(p, axis=0, keepdims=True)
    return _dot(v_t, p.astype(BF16)) / l


def _attn_prompt_kernel(qi_ref, qa_ref, qb_ref, ki_ref, ka_ref, kb_ref, misc_ref,
                        oa_ref, ob_ref,
                        kib, kab, kbb, vat, vbt, wit, kmh, kml, sc, kint, bias, ot):
    seq = ki_ref.shape[0]
    tq = MOBA_BLOCK
    n_blk = seq // tq

    kib[...] = ki_ref[...].astype(BF16)
    kab[...] = ka_ref[...].astype(BF16)
    kb = kb_ref[...]
    vat[...] = misc_ref[2 * HEAD_DIM:3 * HEAD_DIM, :].astype(BF16)
    wit[...] = misc_ref[3 * HEAD_DIM:3 * HEAD_DIM + N_HEADS, :] * (N_HEADS ** -0.5)
    means = jnp.concatenate(
        [jnp.mean(kb[j * tq:(j + 1) * tq, :], axis=0, keepdims=True) for j in range(n_blk)]
        + [jnp.zeros((kmh.shape[1] - n_blk, 2 * HEAD_DIM), F32)], axis=0)
    for n in range(HKV_B):
        kbb[n] = kb[:, n * HEAD_DIM:(n + 1) * HEAD_DIM].astype(BF16)
        vbt[n] = misc_ref[n * HEAD_DIM:(n + 1) * HEAD_DIM, :].astype(BF16)
        hi, lo = _split_bf16(means[:, n * HEAD_DIM:(n + 1) * HEAD_DIM])
        kmh[n] = hi
        kml[n] = lo

    r_loc = lax.broadcasted_iota(I32, (tq, tq), 0)
    c_loc = lax.broadcasted_iota(I32, (tq, tq), 1)
    causal = r_loc <= c_loc

    for i in range(n_blk):
        c0 = i * tq
        lk = c0 + tq
        cols = slice(c0, c0 + tq)

        sc[0:lk, :] = jnp.zeros((lk, tq), F32)

        def idx_head(h, carry):
            off = pl.multiple_of(h * HEAD_DIM, HEAD_DIM)
            s = _dot(kib[0:lk, :], qi_ref[pl.ds(off, HEAD_DIM), cols])
            sc[0:lk, :] += jnp.maximum(s, 0.0) * wit[pl.ds(h, 1), cols]
            return carry

        lax.fori_loop(0, N_HEADS, idx_head, 0)

        if i == 0:
            bias[0:tq, :] = jnp.where(causal, 0.0, NEG_INF)
        else:
            keys = _sortable(sc[0:lk, :])
            kint[0:c0, :] = keys[0:c0, :]
            kint[c0:lk, :] = jnp.where(causal, keys[c0:lk, :], INT_MIN)
            bias[0:lk, :] = _topk_bias(lambda: kint[0:lk, :],
                                       lambda: lax.broadcasted_iota(I32, (lk, tq), 0),
                                       0, (1, tq), TOPK_A, (lk - 1).bit_length())

        def dsa_head(h, carry):
            off = pl.multiple_of(h * HEAD_DIM, HEAD_DIM)
            lg = _dot(kab[0:lk, :], qa_ref[pl.ds(off, HEAD_DIM), cols]) + bias[0:lk, :]
            ot[pl.ds(off, HEAD_DIM), :] = _softmax_pv_t(lg, vat[:, 0:lk])
            return carry

        lax.fori_loop(0, N_HEADS, dsa_head, 0)
        oa_ref[cols, :] = ot[...].T.astype(BF16)

        def moba_head(h, carry):
            off = pl.multiple_of(h * HEAD_DIM, HEAD_DIM)
            n = h // G_B
            q_t = qb_ref[pl.ds(off, HEAD_DIM), cols]
            lg = _dot(kbb[n, 0:lk, :], q_t)
            pieces = []
            if i > TOPK_B:
                gate = _dot(kmh[n], q_t) + _dot(kml[n], q_t)
                rows = lax.broadcasted_iota(I32, gate.shape, 0)
                for j in range(i):
                    gj = gate[j:j + 1, :]
                    beats = jnp.where(rows < j, jnp.where(gate >= gj, 1, 0), jnp.where(gate > gj, 1, 0))
                    rank = jnp.sum(jnp.where(rows < i, beats, 0), axis=0, keepdims=True)
                    pieces.append(lg[j * tq:(j + 1) * tq, :] + jnp.where(rank < TOPK_B, 0.0, NEG_INF))
            else:
                pieces = [lg[j * tq:(j + 1) * tq, :] for j in range(i)]
            pieces.append(jnp.where(causal, lg[c0:lk, :], NEG_INF))
            lgm = jnp.concatenate(pieces, axis=0) if len(pieces) > 1 else pieces[0]
            ot[pl.ds(off, HEAD_DIM), :] = _softmax_pv_t(lgm, vbt[n, :, 0:lk])
            return carry

        lax.fori_loop(0, N_HEADS, moba_head, 0)
        ob_ref[cols, :] = ot[...].T.astype(BF16)


def _attn_prompt(qi_t, qa_t, qb_t, ki, ka, kb, misc_t, n_seq, seq):
    tq = MOBA_BLOCK
    q_spec = pl.BlockSpec((W_Q, seq), lambda b: (0, b))
    row = lambda w: pl.BlockSpec((seq, w), lambda b: (b, 0))
    return pl.pallas_call(
        _attn_prompt_kernel,
        grid=(n_seq,),
        in_specs=[q_spec, q_spec, q_spec, row(HEAD_DIM), row(HEAD_DIM), row(2 * HEAD_DIM),
                  pl.BlockSpec((C_W, seq), lambda b: (0, b))],
        out_specs=(row(W_Q), row(W_Q)),
        out_shape=(jax.ShapeDtypeStruct((n_seq * seq, W_Q), BF16),
                   jax.ShapeDtypeStruct((n_seq * seq, W_Q), BF16)),
        scratch_shapes=[
            pltpu.VMEM((seq, HEAD_DIM), BF16),
            pltpu.VMEM((seq, HEAD_DIM), BF16),
            pltpu.VMEM((HKV_B, seq, HEAD_DIM), BF16),
            pltpu.VMEM((HEAD_DIM, seq), BF16),
            pltpu.VMEM((HKV_B, HEAD_DIM, seq), BF16),
            pltpu.VMEM((N_HEADS, seq), F32),
            pltpu.VMEM((HKV_B, 16, HEAD_DIM), BF16),
            pltpu.VMEM((HKV_B, 16, HEAD_DIM), BF16),
            pltpu.VMEM((seq, tq), F32),
            pltpu.VMEM((seq, tq), I32),
            pltpu.VMEM((seq, tq), F32),
            pltpu.VMEM((W_Q, tq), F32),
        ],
        compiler_params=pltpu.CompilerParams(
            dimension_semantics=("arbitrary",), vmem_limit_bytes=VMEM_LIMIT),
        name="attn_prompt",
    )(qi_t, qa_t, qb_t, ki, ka, kb, misc_t)


ROWS = N_HEADS * 8


def _attn_sample_kernel(pt_ref, cki_ref, cka_ref, cva_ref, ckb_ref, cvb_ref,
                        nki_ref, nka_ref, nva_ref, nkb_ref, nvb_ref,
                        qi_ref, qa_ref, qb_ref, wc_ref,
                        oa_ref, ob_ref,
                        ski, ska, sva, skb, svb, kint, bias,
                        *, group, n_new, past):
    del pt_ref
    s_id = pl.program_id(1)
    p_id = pl.program_id(2)
    n_pages = pl.num_programs(2)
    lpad = ski.shape[1]
    r0 = pl.multiple_of(p_id * PAGE, PAGE)
    ski[s_id, pl.ds(r0, PAGE), :] = cki_ref[0]
    ska[s_id, pl.ds(r0, PAGE), :] = cka_ref[0]
    sva[s_id, pl.ds(r0, PAGE), :] = cva_ref[0]
    skb[s_id, pl.ds(r0, PAGE), :] = ckb_ref[0]
    svb[s_id, pl.ds(r0, PAGE), :] = cvb_ref[0]

    @pl.when(jnp.logical_and(s_id == group - 1, p_id == n_pages - 1))
    def _():
        n_blk = past // MOBA_BLOCK
        col = lax.broadcasted_iota(I32, (8, lpad), 1)
        qrow = lax.broadcasted_iota(I32, (8, lpad), 0)
        visible8 = col <= past + jnp.minimum(qrow, n_new - 1)
        for s in range(group):
            for slab, new in ((ski, nki_ref), (ska, nka_ref), (sva, nva_ref), (skb, nkb_ref), (svb, nvb_ref)):
                slab[s, past:lpad, :] = jnp.zeros((lpad - past, slab.shape[2]), F32)
                slab[s, past:past + n_new, :] = new[s]
            st = _dot_nt(qi_ref[s], ski[s].astype(BF16))
            st = jnp.maximum(st, 0.0) * wc_ref[s]
            score = st[0:8, :]
            for h in range(1, N_HEADS):
                score = score + st[h * 8:(h + 1) * 8, :]
            kint[s * 8:(s + 1) * 8, :] = jnp.where(visible8, _sortable(score), INT_MIN)

        rows = group * 8
        bias[...] = _topk_bias(lambda: kint[...], lambda: lax.broadcasted_iota(I32, (rows, lpad), 1),
                               1, (rows, 1), TOPK_A, (lpad - 1).bit_length())

        colr = lax.broadcasted_iota(I32, (ROWS, LANES), 1)
        qr = lax.broadcasted_iota(I32, (ROWS, LANES), 0) % 8
        visible_tail = colr <= jnp.minimum(qr, n_new - 1)
        for s in range(group):
            lg = _dot_nt(qa_ref[s], ska[s].astype(BF16)) + jnp.tile(bias[s * 8:(s + 1) * 8, :], (N_HEADS, 1))
            m = jnp.max(lg, axis=1, keepdims=True)
            p = jnp.exp(lg - m)
            l = jnp.sum(p, axis=1, keepdims=True)
            oa_ref[s] = _dot(p.astype(BF16), sva[s].astype(BF16)) / l

            kbf = skb[s]
            q2 = qb_ref[s]
            mean_rows = []
            for j in range(n_blk):
                mj = jnp.mean(kbf[j * MOBA_BLOCK:(j + 1) * MOBA_BLOCK, :], axis=0, keepdims=True)
                mean_rows.append(jnp.broadcast_to(mj, (LANES, 2 * HEAD_DIM)))
            hi, lo = _split_bf16(jnp.concatenate(mean_rows, axis=0))
            gate = _dot_nt(q2, hi) + _dot_nt(q2, lo)
            lgb = _dot_nt(q2, kbf.astype(BF16))
            pieces = []
            for j in range(n_blk):
                gj = gate[:, j * LANES:(j + 1) * LANES]
                rank = jnp.zeros((ROWS, LANES), I32)
                for j2 in range(n_blk):
                    if j2 == j:
                        continue
                    g2 = gate[:, j2 * LANES:(j2 + 1) * LANES]
                    rank = rank + jnp.where((g2 >= gj) if j2 < j else (g2 > gj), 1, 0)
                bj = jnp.where(rank < TOPK_B, 0.0, NEG_INF)
                blk = lgb[:, j * MOBA_BLOCK:(j + 1) * MOBA_BLOCK]
                pieces.append(blk + jnp.concatenate([bj] * (MOBA_BLOCK // LANES), axis=1))
            pieces.append(jnp.where(visible_tail, lgb[:, past:lpad], NEG_INF))
            lgm = jnp.concatenate(pieces, axis=1)
            m = jnp.max(lgm, axis=1, keepdims=True)
            p = jnp.exp(lgm - m)
            l = jnp.sum(p, axis=1, keepdims=True)
            ob_ref[s] = _dot(p.astype(BF16), svb[s].astype(BF16)) / l


def _attn_sample(page_table, cki, cka, cva, ckb, cvb, nki, nka, nva, nkb, nvb, qi, qa, qb2, wcol, group):
    n_seq, n_pages = page_table.shape
    n_new = nki.shape[1]
    past = n_pages * PAGE
    lpad = past + LANES
    page = lambda w: pl.BlockSpec((1, PAGE, w), lambda g, s, p, pt: (pt[g * group + s, p], 0, 0))
    grp = lambda a, b: pl.BlockSpec((group, a, b), lambda g, s, p, pt: (g, 0, 0))
    grid_spec = pltpu.PrefetchScalarGridSpec(
        num_scalar_prefetch=1,
        grid=(n_seq // group, group, n_pages),
        in_specs=[page(HEAD_DIM), page(HEAD_DIM), page(HEAD_DIM), page(2 * HEAD_DIM), page(2 * HEAD_DIM),
                  grp(n_new, HEAD_DIM), grp(n_new, HEAD_DIM), grp(n_new, HEAD_DIM),
                  grp(n_new, 2 * HEAD_DIM), grp(n_new, 2 * HEAD_DIM),
                  grp(ROWS, HEAD_DIM), grp(ROWS, HEAD_DIM), grp(ROWS, 2 * HEAD_DIM), grp(ROWS, 1)],
        out_specs=(grp(ROWS, HEAD_DIM), grp(ROWS, 2 * HEAD_DIM)),
        scratch_shapes=[
            pltpu.VMEM((group, lpad, HEAD_DIM), F32),
            pltpu.VMEM((group, lpad, HEAD_DIM), F32),
            pltpu.VMEM((group, lpad, HEAD_DIM), F32),
            pltpu.VMEM((group, lpad, 2 * HEAD_DIM), F32),
            pltpu.VMEM((group, lpad, 2 * HEAD_DIM), F32),
            pltpu.VMEM((group * 8, lpad), I32),
            pltpu.VMEM((group * 8, lpad), F32),
        ],
    )
    return pl.pallas_call(
        functools.partial(_attn_sample_kernel, group=group, n_new=n_new, past=past),
        grid_spec=grid_spec,
        out_shape=(jax.ShapeDtypeStruct((n_seq, ROWS, HEAD_DIM), F32),
                   jax.ShapeDtypeStruct((n_seq, ROWS, 2 * HEAD_DIM), F32)),
        compiler_params=pltpu.CompilerParams(
            dimension_semantics=("arbitrary", "arbitrary", "arbitrary"), vmem_limit_bytes=VMEM_LIMIT),
        name="attn_sample",
    )(page_table, cki, cka, cva, ckb, cvb, nki, nka, nva, nkb, nvb, qi, qa, qb2, wcol)


def _merge_kernel(x_ref, oa_ref, ob_ref, sga_ref, sgb_ref, wa_ref, wb_ref, wo_ref, o_ref):
    m = sga_ref[...] * _dot(oa_ref[...], wa_ref[...]) + sgb_ref[...] * _dot(ob_ref[...], wb_ref[...])
    o_ref[...] = x_ref[...] + _dot(m.astype(BF16), wo_ref[...])


def _merge(x, oa, ob, sga, sgb, wa, wb, wo, tm):
    t = x.shape[0]
    row = lambda w: pl.BlockSpec((tm, w), lambda i: (i, 0))
    return pl.pallas_call(
        _merge_kernel,
        grid=(t // tm,),
        in_specs=[row(D_MODEL), row(W_Q), row(W_Q), row(D_MODEL), row(D_MODEL),
                  _const_spec((W_Q, D_MODEL)), _const_spec((W_Q, D_MODEL)), _const_spec((D_MODEL, D_MODEL))],
        out_specs=row(D_MODEL),
        out_shape=jax.ShapeDtypeStruct((t, D_MODEL), F32),
        compiler_params=pltpu.CompilerParams(
            dimension_semantics=("arbitrary",), vmem_limit_bytes=VMEM_LIMIT),
        name="merge",
    )(x, oa, ob, sga, sgb, wa, wb, wo)


def _ffn_weights(w_in, w_out):
    pad = D_FF_PAD - D_FF
    wg = jnp.pad(w_in[:, :D_FF], ((0, 0), (0, pad))).astype(BF16)
    wu = jnp.pad(w_in[:, D_FF:], ((0, 0), (0, pad))).astype(BF16)
    wo = jnp.pad(w_out, ((0, pad), (0, 0))).astype(BF16)
    return wg, wu, wo


def _rope_tables(pos):
    half = HEAD_DIM // 2
    inv = ROPE_THETA ** (-jnp.arange(half, dtype=F32) / half)
    ang = pos.astype(F32)[:, None] * inv[None, :]
    cos, sin = jnp.cos(ang), jnp.sin(ang)
    return (jnp.concatenate([cos, cos, cos, cos], axis=1),
            jnp.concatenate([-sin, sin, -sin, sin], axis=1))


def _head_rows(a, n_seq, n_new):
    w = a.shape[1] // N_HEADS
    a = a.reshape(n_seq, n_new, N_HEADS, w).transpose(0, 2, 1, 3)
    a = jnp.pad(a, ((0, 0), (0, 0), (0, 8 - n_new), (0, 0)))
    return a.reshape(n_seq, ROWS, w)


def _from_head_rows(a, n_seq, n_new):
    w = a.shape[2]
    a = a.reshape(n_seq, N_HEADS, 8, w)[:, :, :n_new].transpose(0, 2, 1, 3)
    return a.reshape(n_seq * n_new, N_HEADS * w)


def kernel(x_prompt, x_sample, cache_k_a, cache_v_a, cache_kidx_a, cache_k_b, cache_v_b, page_table,
           ffn1_norm, ffn1_w_in, ffn1_w_out, mix_norm, w_in, q_norm_a, k_norm_a, q_norm_b, k_norm_b,
           gate_bias, w_up_a, w_up_b, w_out, ffn2_norm, ffn2_w_in, ffn2_w_out):
    n_seq, seq, _ = x_prompt.shape
    n_dec, n_new, _ = x_sample.shape
    depth = ffn1_norm.shape[0]
    n_pool = cache_k_a.shape[1]
    past = page_table.shape[1] * PAGE
    assert depth == 1 and seq % MOBA_BLOCK == 0 and past % MOBA_BLOCK == 0 and n_new <= 8
    assert seq >= 4 * TOPK_A and past + n_new >= 4 * TOPK_A
    l = 0
    tp, ts = n_seq * seq, n_dec * n_new

    xp = x_prompt.reshape(tp, D_MODEL)
    xs = x_sample.reshape(ts, D_MODEL)

    f1 = _ffn_weights(ffn1_w_in[l], ffn1_w_out[l])
    f2 = _ffn_weights(ffn2_w_in[l], ffn2_w_out[l])
    w = w_in[l]
    o = np.cumsum([0, W_Q, HEAD_DIM, HEAD_DIM, W_Q, HEAD_DIM, N_HEADS, W_Q, 2 * HEAD_DIM, 2 * HEAD_DIM,
                   D_MODEL, D_MODEL])
    c = lambda k: w[:, o[k]:o[k + 1]]
    z = lambda n: jnp.zeros((D_MODEL, n), F32)
    wp = jnp.concatenate([c(0), c(6), c(7), c(1), z(64), c(3), c(4), z(64), c(8), c(2), c(5), z(56),
                          c(9), c(10)], axis=1).astype(BF16)
    gain = jnp.concatenate([jnp.tile(q_norm_a[l], N_HEADS), jnp.tile(q_norm_b[l], N_HEADS),
                            jnp.tile(k_norm_b[l], HKV_B), k_norm_a[l], jnp.zeros((64,), F32)])[None, :]
    bias = gate_bias[l][None, :]
    lane = np.arange(LANES)
    bd = jnp.asarray((lane[:, None] // HEAD_DIM) == (lane[None, :] // HEAD_DIM), BF16)
    cos_p, sin_p = _rope_tables(jnp.arange(seq, dtype=I32))
    cos_s, sin_s = _rope_tables(past + (jnp.arange(ts, dtype=I32) % n_new))
    wa, wb, wo = w_up_a[l].astype(BF16), w_up_b[l].astype(BF16), w_out[l].astype(BF16)
    g1, gm, g2 = ffn1_norm[l][None, :], mix_norm[l][None, :], ffn2_norm[l][None, :]

    tm = 512
    xp1 = _ffn(xp, g1, *f1, tm)
    (qa_t, qb_t, qi_t, kb_p, ka_p, ki_p, vb_p, va_p, misc_t, sga_p, sgb_p) = _proj(
        xp1, gm, wp, gain, bias, cos_p, sin_p, bd, 256, True)
    oa_p, ob_p = _attn_prompt(qi_t, qa_t, qb_t, ki_p, ka_p, kb_p, misc_t, n_seq, seq)
    xp2 = _merge(xp1, oa_p, ob_p, sga_p, sgb_p, wa, wb, wo, tm)
    yp = _ffn(xp2, g2, *f2, tm)

    xs1 = _ffn(xs, g1, *f1, ts)
    (qa_s, qb_s, qi_s, kb_s, ka_s, ki_s, vb_s, va_s, misc_s, sga_s, sgb_s) = _proj(
        xs1, gm, wp, gain, bias, cos_s, sin_s, bd, ts, False)
    qi_r = _head_rows(qi_s, n_dec, n_new)
    qa_r = _head_rows(qa_s, n_dec, n_new)
    qb_r = _head_rows(qb_s, n_dec, n_new)
    zq = jnp.zeros_like(qb_r)
    first = (jnp.arange(ROWS) < ROWS // HKV_B)[None, :, None]
    qb2 = jnp.concatenate([jnp.where(first, qb_r, zq), jnp.where(first, zq, qb_r)], axis=2)
    wi_s = misc_s[:, 3 * HEAD_DIM:3 * HEAD_DIM + N_HEADS] * (N_HEADS ** -0.5)
    wcol = _head_rows(wi_s, n_dec, n_new)
    r3 = lambda a: a.reshape(n_dec, n_new, a.shape[1])
    oa_r, ob_r = _attn_sample(
        page_table,
        cache_kidx_a[l], cache_k_a[l].reshape(n_pool, PAGE, HEAD_DIM), cache_v_a[l].reshape(n_pool, PAGE, HEAD_DIM),
        cache_k_b[l].reshape(n_pool, PAGE, 2 * HEAD_DIM), cache_v_b[l].reshape(n_pool, PAGE, 2 * HEAD_DIM),
        r3(ki_s), r3(ka_s), r3(va_s), r3(kb_s), r3(vb_s), qi_r, qa_r, qb2, wcol, 4)
    oa_s = _from_head_rows(oa_r, n_dec, n_new).astype(BF16)
    ob_sel = jnp.where(first, ob_r[:, :, :HEAD_DIM], ob_r[:, :, HEAD_DIM:])
    ob_s = _from_head_rows(ob_sel, n_dec, n_new).astype(BF16)
    xs2 = _merge(xs1, oa_s, ob_s, sga_s, sgb_s, wa, wb, wo, ts)
    ys = _ffn(xs2, g2, *f2, ts)

    d = depth
    return (yp.reshape(n_seq, seq, D_MODEL), ys.reshape(n_dec, n_new, D_MODEL),
            ka_p.reshape(d, n_seq, seq, 1, HEAD_DIM), va_p.reshape(d, n_seq, seq, 1, HEAD_DIM),
            ki_p.reshape(d, n_seq, seq, HEAD_DIM),
            kb_p.reshape(d, n_seq, seq, HKV_B, HEAD_DIM), vb_p.reshape(d, n_seq, seq, HKV_B, HEAD_DIM),
            ka_s.reshape(d, n_dec, n_new, 1, HEAD_DIM), va_s.reshape(d, n_dec, n_new, 1, HEAD_DIM),
            ki_s.reshape(d, n_dec, n_new, HEAD_DIM),
            kb_s.reshape(d, n_dec, n_new, HKV_B, HEAD_DIM), vb_s.reshape(d, n_dec, n_new, HKV_B, HEAD_DIM))
```

```python
import functools

import numpy as np
import jax
import jax.numpy as jnp
from jax import lax
from jax.experimental import pallas as pl
from jax.experimental.pallas import tpu as pltpu

F32 = jnp.float32
BF16 = jnp.bfloat16
I32 = jnp.int32

D_MODEL = 1024
HEAD_DIM = 64
N_HEADS = 8
HKV_B = 2
G_B = N_HEADS // HKV_B
D_FF = 2752
TOPK_A = 256
MOBA_BLOCK = 256
TOPK_B = 3
PAGE = 128
ROPE_THETA = 10000.0
EPS = 1e-6

LANES = 128
D_FF_PAD = 2816
FF_CHUNK = 512
W_Q = N_HEADS * HEAD_DIM

A_W = 1280
B_W = 640
C_W = 256
G_W = 2 * D_MODEL
P_W = A_W + B_W + C_W + G_W

INT_MIN = np.int32(-2 ** 31)
NEG_INF = float("-inf")
VMEM_LIMIT = 56 * 1024 * 1024


def _dot(a, b):
    return jnp.dot(a, b, preferred_element_type=F32)


def _dot_nt(a, b):
    return lax.dot_general(a, b, (((1,), (1,)), ((), ())), preferred_element_type=F32)


def _split_bf16(x):
    hi = x.astype(BF16)
    lo = (x - hi.astype(F32)).astype(BF16)
    return hi, lo


def _rms(x, g):
    ms = jnp.mean(x * x, axis=-1, keepdims=True)
    return x * lax.rsqrt(ms + EPS) * g


def _const_spec(shape):
    nd = len(shape)
    return pl.BlockSpec(shape, lambda *_: (0,) * nd)


def _ffn_kernel(x_ref, g_ref, wg_ref, wu_ref, wo_ref, o_ref):
    x = x_ref[...]
    h = _rms(x, g_ref[...]).astype(BF16)
    acc = jnp.zeros_like(x)
    for s in range(0, D_FF_PAD, FF_CHUNK):
        e = min(s + FF_CHUNK, D_FF_PAD)
        g = _dot(h, wg_ref[:, s:e])
        u = _dot(h, wu_ref[:, s:e])
        a = (g * jax.nn.sigmoid(g) * u).astype(BF16)
        acc = acc + _dot(a, wo_ref[s:e, :])
    o_ref[...] = x + 0.5 * acc


def _ffn(x, g, wg, wu, wo, tm):
    t = x.shape[0]
    return pl.pallas_call(
        _ffn_kernel,
        grid=(t // tm,),
        in_specs=[
            pl.BlockSpec((tm, D_MODEL), lambda i: (i, 0)),
            _const_spec((1, D_MODEL)),
            _const_spec((D_MODEL, D_FF_PAD)),
            _const_spec((D_MODEL, D_FF_PAD)),
            _const_spec((D_FF_PAD, D_MODEL)),
        ],
        out_specs=pl.BlockSpec((tm, D_MODEL), lambda i: (i, 0)),
        out_shape=jax.ShapeDtypeStruct((t, D_MODEL), F32),
        compiler_params=pltpu.CompilerParams(
            dimension_semantics=("arbitrary",), vmem_limit_bytes=VMEM_LIMIT),
        name="ffn",
    )(x, g, wg, wu, wo)


def _rope(y, cos, sin, first_half):
    r_lo = pltpu.roll(y, 32, 1)
    r_hi = pltpu.roll(y, 96, 1)
    return y * cos + jnp.where(first_half, r_hi, r_lo) * sin


def _proj_kernel(x_ref, gm_ref, wp_ref, gain_ref, bias_ref, cos_ref, sin_ref, bd_ref,
                 qa_ref, qb_ref, qi_ref, kb_ref, ka_ref, ki_ref, vb_ref, va_ref, wi_ref,
                 sga_ref, sgb_ref, *, transposed):
    x = x_ref[...]
    tm = x.shape[0]
    h = _rms(x, gm_ref[...]).astype(BF16)
    cos = cos_ref[...]
    sin = sin_ref[...]
    bd = bd_ref[...]
    lane = lax.broadcasted_iota(I32, (tm, LANES), 1)
    first_half = (lane % HEAD_DIM) < (HEAD_DIM // 2)

    z_a = _dot(h, wp_ref[:, 0:A_W])
    tiles_a = []
    for j in range(A_W // LANES):
        z = z_a[:, j * LANES:(j + 1) * LANES]
        hi, lo = _split_bf16(z * z)
        ms = (_dot(hi, bd) + _dot(lo, bd)) * (1.0 / HEAD_DIM)
        y = z * lax.rsqrt(ms + EPS) * gain_ref[:, j * LANES:(j + 1) * LANES]
        tiles_a.append(_rope(y, cos, sin, first_half))

    z_b = _dot(h, wp_ref[:, A_W:A_W + B_W])
    tiles_b = [_rope(z_b[:, j * LANES:(j + 1) * LANES], cos, sin, first_half)
               for j in range(B_W // LANES)]

    z_c = _dot(h, wp_ref[:, A_W + B_W:A_W + B_W + C_W])
    z_g = _dot(h, wp_ref[:, A_W + B_W + C_W:P_W]) + bias_ref[...]

    scale = HEAD_DIM ** -0.5
    qa = jnp.concatenate(tiles_a[0:4], axis=1) * scale
    qb = jnp.concatenate(tiles_a[4:8], axis=1) * scale
    qi = jnp.concatenate(tiles_b[0:4], axis=1) * scale
    wi_scale = N_HEADS ** -0.5
    if transposed:
        qa_ref[...] = qa.T.astype(BF16)
        qb_ref[...] = qb.T.astype(BF16)
        qi_ref[...] = qi.T.astype(BF16)
        kb_ref[...] = tiles_a[8].T
        ka_ref[...] = tiles_a[9].T[0:HEAD_DIM, :]
        ki_ref[...] = tiles_b[4].T[0:HEAD_DIM, :]
        zc_t = z_c.T
        vb_ref[...] = zc_t[0:LANES, :]
        va_ref[...] = zc_t[LANES:LANES + HEAD_DIM, :]
        wi_ref[...] = zc_t[LANES + HEAD_DIM:LANES + HEAD_DIM + N_HEADS, :] * wi_scale
    else:
        qa_ref[...] = qa.astype(BF16)
        qb_ref[...] = qb.astype(BF16)
        qi_ref[...] = qi.astype(BF16)
        kb_ref[...] = tiles_a[8]
        ka_ref[...] = tiles_a[9][:, 0:HEAD_DIM]
        ki_ref[...] = tiles_b[4][:, 0:HEAD_DIM]
        vb_ref[...] = z_c[:, 0:LANES]
        va_ref[...] = z_c[:, LANES:LANES + HEAD_DIM]
        wi_ref[...] = z_c[:, LANES + HEAD_DIM:LANES + HEAD_DIM + N_HEADS] * wi_scale
    sg = jax.nn.sigmoid(z_g)
    sga_ref[...] = sg[:, 0:D_MODEL]
    sgb_ref[...] = sg[:, D_MODEL:G_W]


def _proj(x, gm, wp, gain, bias, cos_t, sin_t, bd, tm, seq):
    t = x.shape[0]
    n_tab = cos_t.shape[0] // tm
    row = lambda w: pl.BlockSpec((tm, w), lambda i: (i, 0))
    widths = (W_Q, W_Q, W_Q, 2 * HEAD_DIM, HEAD_DIM, HEAD_DIM, 2 * HEAD_DIM, HEAD_DIM, N_HEADS)
    dtypes = (BF16, BF16, BF16, F32, F32, F32, F32, F32, F32)
    if seq is None:
        specs = [row(w) for w in widths]
        shapes = [jax.ShapeDtypeStruct((t, w), d) for w, d in zip(widths, dtypes)]
    else:
        nb = seq // tm
        specs = [pl.BlockSpec((None, w, tm), lambda i: (i // nb, 0, i % nb)) for w in widths]
        shapes = [jax.ShapeDtypeStruct((t // seq, w, seq), d) for w, d in zip(widths, dtypes)]
    out_specs = tuple(specs) + (row(D_MODEL), row(D_MODEL))
    out_shape = tuple(shapes) + (jax.ShapeDtypeStruct((t, D_MODEL), F32), jax.ShapeDtypeStruct((t, D_MODEL), F32))
    transposed = seq is not None
    return pl.pallas_call(
        functools.partial(_proj_kernel, transposed=transposed),
        grid=(t // tm,),
        in_specs=[
            row(D_MODEL),
            _const_spec((1, D_MODEL)),
            _const_spec((D_MODEL, P_W)),
            _const_spec((1, A_W)),
            _const_spec((1, G_W)),
            pl.BlockSpec((tm, LANES), lambda i: (i % n_tab, 0)),
            pl.BlockSpec((tm, LANES), lambda i: (i % n_tab, 0)),
            _const_spec((LANES, LANES)),
        ],
        out_specs=out_specs,
        out_shape=out_shape,
        compiler_params=pltpu.CompilerParams(
            dimension_semantics=("arbitrary",), vmem_limit_bytes=VMEM_LIMIT),
        name="proj",
    )(x, gm, wp, gain, bias, cos_t, sin_t, bd)


def _sortable(score):
    b = lax.bitcast_convert_type(score, I32)
    k = b ^ ((b >> 31) & np.int32(0x7FFFFFFF))
    return jnp.where(score == 0.0, np.int32(0), k)


def _count(mask, axis):
    return jnp.sum(jnp.where(mask, 1.0, 0.0), axis=axis, keepdims=True)


def _topk_bias(store, load_keys, key_index, axis, vec_shape, k_sel, n_index_bits):
    one = np.int32(1)

    def value_bit(it, ans):
        cand = ans | lax.shift_left(one, 31 - it)
        thr = cand ^ INT_MIN
        return jnp.where(_count(load_keys() >= thr, axis) >= k_sel, cand, ans)

    ans = lax.fori_loop(0, 32, value_bit, jnp.zeros(vec_shape, I32))
    thr = ans ^ INT_MIN
    has_tie = jnp.max(_count(load_keys() >= thr, axis)) > k_sel

    @pl.when(jnp.logical_not(has_tie))
    def _():
        store(jnp.where(load_keys() >= thr, 0.0, NEG_INF))

    @pl.when(has_tie)
    def _():
        need = k_sel - _count(load_keys() > thr, axis)

        def index_bit(it, cut):
            cand = cut | lax.shift_left(one, n_index_bits - 1 - it)
            hit = jnp.where(load_keys() == thr, jnp.where(key_index() < cand, 1.0, 0.0), 0.0)
            return jnp.where(jnp.sum(hit, axis=axis, keepdims=True) < need, cand, cut)

        cut = lax.fori_loop(0, n_index_bits, index_bit, jnp.zeros(vec_shape, I32))
        keys = load_keys()
        keep_eq = jnp.where(key_index() <= cut, 0.0, NEG_INF)
        store(jnp.where(keys > thr, 0.0, jnp.where(keys == thr, keep_eq, NEG_INF)))


V_ROWS = HEAD_DIM + 16


def _softmax_pv_t(lg, v_t):
    m = jnp.max(lg, axis=0, keepdims=True)
    p = jnp.exp(lg - m).astype(BF16)
    o = _dot(v_t, p)
    return o[0:HEAD_DIM, :] / o[HEAD_DIM:HEAD_DIM + 1, :]


def _attn_prompt_kernel(qi_ref, qa_ref, qb_ref, ki_ref, ka_ref, kb_ref, va_ref, vb_ref, wit,
                        oa_ref, ob_ref,
                        kib, kab, kbb, vat, vbt, kmh, kml, sc, kint, bias, ot):
    seq = ki_ref.shape[1]
    tq = MOBA_BLOCK
    n_blk = seq // tq

    kk = jnp.concatenate([ka_ref[...], ki_ref[...]], axis=0).T
    kab[...] = kk[:, 0:HEAD_DIM].astype(BF16)
    kib[...] = kk[:, HEAD_DIM:2 * HEAD_DIM].astype(BF16)
    kb = kb_ref[...].T
    ones = jnp.ones((V_ROWS - HEAD_DIM, seq), BF16)
    vat[...] = jnp.concatenate([va_ref[...].astype(BF16), ones], axis=0)
    means = jnp.concatenate(
        [jnp.mean(kb[j * tq:(j + 1) * tq, :], axis=0, keepdims=True) for j in range(n_blk)]
        + [jnp.zeros((kmh.shape[1] - n_blk, 2 * HEAD_DIM), F32)], axis=0)
    for n in range(HKV_B):
        kbb[n] = kb[:, n * HEAD_DIM:(n + 1) * HEAD_DIM].astype(BF16)
        vbt[n] = jnp.concatenate([vb_ref[n * HEAD_DIM:(n + 1) * HEAD_DIM, :].astype(BF16), ones], axis=0)
        hi, lo = _split_bf16(means[:, n * HEAD_DIM:(n + 1) * HEAD_DIM])
        kmh[n] = hi
        kml[n] = lo

    r_loc = lax.broadcasted_iota(I32, (tq, tq), 0)
    c_loc = lax.broadcasted_iota(I32, (tq, tq), 1)
    causal = r_loc <= c_loc

    for i in range(n_blk):
        c0 = i * tq
        lk = c0 + tq
        cols = slice(c0, c0 + tq)

        sc[0:lk, :] = jnp.zeros((lk, tq), F32)

        def idx_head(h, carry):
            off = pl.multiple_of(h * HEAD_DIM, HEAD_DIM)
            s = _dot(kib[0:lk, :], qi_ref[pl.ds(off, HEAD_DIM), cols])
            sc[0:lk, :] += jnp.maximum(s, 0.0) * wit[pl.ds(h, 1), cols]
            return carry

        lax.fori_loop(0, N_HEADS, idx_head, 0)

        if i == 0:
            bias[0:tq, :] = jnp.where(causal, 0.0, NEG_INF)
        else:
            keys = _sortable(sc[0:lk, :])
            kint[0:c0, :] = keys[0:c0, :]
            kint[c0:lk, :] = jnp.where(causal, keys[c0:lk, :], INT_MIN)
            def store_bias(v):
                bias[0:lk, :] = v

            _topk_bias(store_bias, lambda: kint[0:lk, :], lambda: lax.broadcasted_iota(I32, (lk, tq), 0),
                       0, (1, tq), TOPK_A, (lk - 1).bit_length())

        def dsa_head(h, carry):
            off = pl.multiple_of(h * HEAD_DIM, HEAD_DIM)
            lg = _dot(kab[0:lk, :], qa_ref[pl.ds(off, HEAD_DIM), cols]) + bias[0:lk, :]
            ot[pl.ds(off, HEAD_DIM), :] = _softmax_pv_t(lg, vat[:, 0:lk])
            return carry

        lax.fori_loop(0, N_HEADS, dsa_head, 0)
        oa_ref[cols, :] = ot[...].T.astype(BF16)

        def moba_head(h, carry):
            off = pl.multiple_of(h * HEAD_DIM, HEAD_DIM)
            n = h // G_B
            q_t = qb_ref[pl.ds(off, HEAD_DIM), cols]
            lg = _dot(kbb[n, 0:lk, :], q_t)
            pieces = []
            if i > TOPK_B:
                gate = _dot(kmh[n], q_t) + _dot(kml[n], q_t)
                rows = lax.broadcasted_iota(I32, gate.shape, 0)
                for j in range(i):
                    gj = gate[j:j + 1, :]
                    beats = jnp.where(rows < j, jnp.where(gate >= gj, 1, 0), jnp.where(gate > gj, 1, 0))
                    rank = jnp.sum(jnp.where(rows < i, beats, 0), axis=0, keepdims=True)
                    pieces.append(lg[j * tq:(j + 1) * tq, :] + jnp.where(rank < TOPK_B, 0.0, NEG_INF))
            else:
                pieces = [lg[j * tq:(j + 1) * tq, :] for j in range(i)]
            pieces.append(jnp.where(causal, lg[c0:lk, :], NEG_INF))
            lgm = jnp.concatenate(pieces, axis=0) if len(pieces) > 1 else pieces[0]
            ot[pl.ds(off, HEAD_DIM), :] = _softmax_pv_t(lgm, vbt[n, :, 0:lk])
            return carry

        lax.fori_loop(0, N_HEADS, moba_head, 0)
        ob_ref[cols, :] = ot[...].T.astype(BF16)


def _attn_prompt(qi_t, qa_t, qb_t, ki_t, ka_t, kb_t, va_t, vb_t, wi_t):
    n_seq, _, seq = qi_t.shape
    tq = MOBA_BLOCK
    fm = lambda w: pl.BlockSpec((None, w, seq), lambda b: (b, 0, 0))
    row = lambda w: pl.BlockSpec((seq, w), lambda b: (b, 0))
    return pl.pallas_call(
        _attn_prompt_kernel,
        grid=(n_seq,),
        in_specs=[fm(W_Q), fm(W_Q), fm(W_Q), fm(HEAD_DIM), fm(HEAD_DIM), fm(2 * HEAD_DIM),
                  fm(HEAD_DIM), fm(2 * HEAD_DIM), fm(N_HEADS)],
        out_specs=(row(W_Q), row(W_Q)),
        out_shape=(jax.ShapeDtypeStruct((n_seq * seq, W_Q), BF16),
                   jax.ShapeDtypeStruct((n_seq * seq, W_Q), BF16)),
        scratch_shapes=[
            pltpu.VMEM((seq, HEAD_DIM), BF16),
            pltpu.VMEM((seq, HEAD_DIM), BF16),
            pltpu.VMEM((HKV_B, seq, HEAD_DIM), BF16),
            pltpu.VMEM((V_ROWS, seq), BF16),
            pltpu.VMEM((HKV_B, V_ROWS, seq), BF16),
            pltpu.VMEM((HKV_B, 16, HEAD_DIM), BF16),
            pltpu.VMEM((HKV_B, 16, HEAD_DIM), BF16),
            pltpu.VMEM((seq, tq), F32),
            pltpu.VMEM((seq, tq), I32),
            pltpu.VMEM((seq, tq), F32),
            pltpu.VMEM((W_Q, tq), F32),
        ],
        compiler_params=pltpu.CompilerParams(
            dimension_semantics=("arbitrary",), vmem_limit_bytes=VMEM_LIMIT),
        name="attn_prompt",
    )(qi_t, qa_t, qb_t, ki_t, ka_t, kb_t, va_t, vb_t, wi_t)


ROWS = N_HEADS * 8
SAMPLE_GROUP = 4


def _attn_sample_kernel(pt_ref, cki_hbm, cka_hbm, cva_hbm, ckb_hbm, cvb_hbm,
                        nki_ref, nka_ref, nva_ref, nkb_ref, nvb_ref,
                        qi_ref, qa_ref, qb_ref, wc_ref,
                        oa_ref, ob_ref,
                        ski, ska, sva, skb, svb, kint, bias, sem,
                        *, group, n_new, past):
    g = pl.program_id(0)
    n_groups = pl.num_programs(0)
    n_pages = past // PAGE
    lpad = ski.shape[3]
    slot = g % 2
    pools = ((cki_hbm, ski), (cka_hbm, ska), (cva_hbm, sva), (ckb_hbm, skb), (cvb_hbm, svb))

    def page_copies(grp, to_slot, t):
        s = t // n_pages
        p = t % n_pages
        page = pt_ref[grp * group + s, p]
        lanes = pl.ds(pl.multiple_of(p * PAGE, PAGE), PAGE)
        return [pltpu.make_async_copy(hbm.at[page], slab.at[to_slot, s, :, lanes], sem.at[to_slot])
                for hbm, slab in pools]

    def start_group(grp, to_slot):
        def body(t, carry):
            for cp in page_copies(grp, to_slot, t):
                cp.start()
            return carry
        lax.fori_loop(0, group * n_pages, body, 0)

    def wait_group(grp, to_slot):
        def body(t, carry):
            for cp in page_copies(grp, to_slot, t):
                cp.wait()
            return carry
        lax.fori_loop(0, group * n_pages, body, 0)

    @pl.when(g == 0)
    def _():
        start_group(0, 0)

    @pl.when(g + 1 < n_groups)
    def _():
        start_group(g + 1, 1 - slot)

    wait_group(g, slot)

    n_blk = past // MOBA_BLOCK
    col = lax.broadcasted_iota(I32, (8, lpad), 1)
    qrow = lax.broadcasted_iota(I32, (8, lpad), 0)
    visible8 = col <= past + jnp.minimum(qrow, n_new - 1)
    for s in range(group):
        for slab, new in ((ski, nki_ref), (ska, nka_ref), (sva, nva_ref), (skb, nkb_ref), (svb, nvb_ref)):
            slab[slot, s, :, past:lpad] = jnp.zeros((slab.shape[2], lpad - past), F32)
            slab[slot, s, :, past:past + n_new] = new[s]
        st = _dot(qi_ref[s], ski[slot, s].astype(BF16))
        st = jnp.maximum(st, 0.0) * wc_ref[s]
        score = st[0:8, :]
        for h in range(1, N_HEADS):
            score = score + st[h * 8:(h + 1) * 8, :]
        kint[s * 8:(s + 1) * 8, :] = jnp.where(visible8, _sortable(score), INT_MIN)

    rows = group * 8

    def store_bias(v):
        bias[...] = v

    _topk_bias(store_bias, lambda: kint[...], lambda: lax.broadcasted_iota(I32, (rows, lpad), 1),
               1, (rows, 1), TOPK_A, (lpad - 1).bit_length())

    colr = lax.broadcasted_iota(I32, (ROWS, LANES), 1)
    qr = lax.broadcasted_iota(I32, (ROWS, LANES), 0) % 8
    visible_tail = colr <= jnp.minimum(qr, n_new - 1)
    for s in range(group):
        lg = _dot(qa_ref[s], ska[slot, s].astype(BF16)) + jnp.tile(bias[s * 8:(s + 1) * 8, :], (N_HEADS, 1))
        m = jnp.max(lg, axis=1, keepdims=True)
        p = jnp.exp(lg - m)
        l = jnp.sum(p, axis=1, keepdims=True)
        oa_ref[s] = _dot_nt(p.astype(BF16), sva[slot, s].astype(BF16)) / l

        lgb = _dot(qb_ref[s], skb[slot, s].astype(BF16))
        gate = [jnp.sum(lgb[:, j * MOBA_BLOCK:(j + 1) * MOBA_BLOCK], axis=1, keepdims=True)
                for j in range(n_blk)]
        pieces = []
        for j in range(n_blk):
            rank = jnp.zeros((ROWS, 1), I32)
            for j2 in range(n_blk):
                if j2 != j:
                    rank = rank + jnp.where((gate[j2] >= gate[j]) if j2 < j else (gate[j2] > gate[j]), 1, 0)
            pieces.append(lgb[:, j * MOBA_BLOCK:(j + 1) * MOBA_BLOCK] + jnp.where(rank < TOPK_B, 0.0, NEG_INF))
        pieces.append(jnp.where(visible_tail, lgb[:, past:lpad], NEG_INF))
        lgm = jnp.concatenate(pieces, axis=1)
        m = jnp.max(lgm, axis=1, keepdims=True)
        p = jnp.exp(lgm - m)
        l = jnp.sum(p, axis=1, keepdims=True)
        ob_ref[s] = _dot_nt(p.astype(BF16), svb[slot, s].astype(BF16)) / l


def _attn_sample(page_table, cki, cka, cva, ckb, cvb, nki, nka, nva, nkb, nvb, qi, qa, qb2, wcol, group):
    n_seq, n_pages = page_table.shape
    n_new = nki.shape[2]
    past = n_pages * PAGE
    lpad = past + LANES
    hbm = pl.BlockSpec(memory_space=pl.ANY)
    grp = lambda a, b: pl.BlockSpec((group, a, b), lambda g, pt: (g, 0, 0))
    slab = lambda w: pltpu.VMEM((2, group, w, lpad), F32)
    grid_spec = pltpu.PrefetchScalarGridSpec(
        num_scalar_prefetch=1,
        grid=(n_seq // group,),
        in_specs=[hbm, hbm, hbm, hbm, hbm,
                  grp(HEAD_DIM, n_new), grp(HEAD_DIM, n_new), grp(HEAD_DIM, n_new),
                  grp(2 * HEAD_DIM, n_new), grp(2 * HEAD_DIM, n_new),
                  grp(ROWS, HEAD_DIM), grp(ROWS, HEAD_DIM), grp(ROWS, 2 * HEAD_DIM), grp(ROWS, 1)],
        out_specs=(grp(ROWS, HEAD_DIM), grp(ROWS, 2 * HEAD_DIM)),
        scratch_shapes=[
            slab(HEAD_DIM),
            slab(HEAD_DIM),
            slab(HEAD_DIM),
            slab(2 * HEAD_DIM),
            slab(2 * HEAD_DIM),
            pltpu.VMEM((group * 8, lpad), I32),
            pltpu.VMEM((group * 8, lpad), F32),
            pltpu.SemaphoreType.DMA((2,)),
        ],
    )
    return pl.pallas_call(
        functools.partial(_attn_sample_kernel, group=group, n_new=n_new, past=past),
        grid_spec=grid_spec,
        out_shape=(jax.ShapeDtypeStruct((n_seq, ROWS, HEAD_DIM), F32),
                   jax.ShapeDtypeStruct((n_seq, ROWS, 2 * HEAD_DIM), F32)),
        compiler_params=pltpu.CompilerParams(
            dimension_semantics=("arbitrary",), vmem_limit_bytes=VMEM_LIMIT),
        name="attn_sample",
    )(page_table, cki, cka, cva, ckb, cvb, nki, nka, nva, nkb, nvb, qi, qa, qb2, wcol)


def _merge_kernel(x_ref, oa_ref, ob_ref, sga_ref, sgb_ref, wa_ref, wb_ref, wo_ref, o_ref):
    m = sga_ref[...] * _dot(oa_ref[...], wa_ref[...]) + sgb_ref[...] * _dot(ob_ref[...], wb_ref[...])
    o_ref[...] = x_ref[...] + _dot(m.astype(BF16), wo_ref[...])


def _merge(x, oa, ob, sga, sgb, wa, wb, wo, tm):
    t = x.shape[0]
    row = lambda w: pl.BlockSpec((tm, w), lambda i: (i, 0))
    return pl.pallas_call(
        _merge_kernel,
        grid=(t // tm,),
        in_specs=[row(D_MODEL), row(W_Q), row(W_Q), row(D_MODEL), row(D_MODEL),
                  _const_spec((W_Q, D_MODEL)), _const_spec((W_Q, D_MODEL)), _const_spec((D_MODEL, D_MODEL))],
        out_specs=row(D_MODEL),
        out_shape=jax.ShapeDtypeStruct((t, D_MODEL), F32),
        compiler_params=pltpu.CompilerParams(
            dimension_semantics=("arbitrary",), vmem_limit_bytes=VMEM_LIMIT),
        name="merge",
    )(x, oa, ob, sga, sgb, wa, wb, wo)


def _ffn_weights(w_in, w_out):
    pad = D_FF_PAD - D_FF
    wg = jnp.pad(w_in[:, :D_FF], ((0, 0), (0, pad))).astype(BF16)
    wu = jnp.pad(w_in[:, D_FF:], ((0, 0), (0, pad))).astype(BF16)
    wo = jnp.pad(w_out, ((0, pad), (0, 0))).astype(BF16)
    return wg, wu, wo


def _rope_tables(pos):
    half = HEAD_DIM // 2
    inv = ROPE_THETA ** (-jnp.arange(half, dtype=F32) / half)
    ang = pos.astype(F32)[:, None] * inv[None, :]
    cos, sin = jnp.cos(ang), jnp.sin(ang)
    return (jnp.concatenate([cos, cos, cos, cos], axis=1),
            jnp.concatenate([-sin, sin, -sin, sin], axis=1))


def _head_rows(a, n_seq, n_new):
    w = a.shape[1] // N_HEADS
    a = a.reshape(n_seq, n_new, N_HEADS, w).transpose(0, 2, 1, 3)
    a = jnp.pad(a, ((0, 0), (0, 0), (0, 8 - n_new), (0, 0)))
    return a.reshape(n_seq, ROWS, w)


def _from_head_rows(a, n_seq, n_new):
    w = a.shape[2]
    a = a.reshape(n_seq, N_HEADS, 8, w)[:, :, :n_new].transpose(0, 2, 1, 3)
    return a.reshape(n_seq * n_new, N_HEADS * w)


def kernel(x_prompt, x_sample, cache_k_a, cache_v_a, cache_kidx_a, cache_k_b, cache_v_b, page_table,
           ffn1_norm, ffn1_w_in, ffn1_w_out, mix_norm, w_in, q_norm_a, k_norm_a, q_norm_b, k_norm_b,
           gate_bias, w_up_a, w_up_b, w_out, ffn2_norm, ffn2_w_in, ffn2_w_out):
    n_seq, seq, _ = x_prompt.shape
    n_dec, n_new, _ = x_sample.shape
    depth = ffn1_norm.shape[0]
    n_pool = cache_k_a.shape[1]
    past = page_table.shape[1] * PAGE
    assert depth == 1 and seq % MOBA_BLOCK == 0 and past % MOBA_BLOCK == 0 and n_new <= 8
    assert seq >= 4 * TOPK_A and past + n_new >= 4 * TOPK_A
    l = 0
    tp, ts = n_seq * seq, n_dec * n_new

    xp = x_prompt.reshape(tp, D_MODEL)
    xs = x_sample.reshape(ts, D_MODEL)

    f1 = _ffn_weights(ffn1_w_in[l], ffn1_w_out[l])
    f2 = _ffn_weights(ffn2_w_in[l], ffn2_w_out[l])
    w = w_in[l]
    o = np.cumsum([0, W_Q, HEAD_DIM, HEAD_DIM, W_Q, HEAD_DIM, N_HEADS, W_Q, 2 * HEAD_DIM, 2 * HEAD_DIM,
                   D_MODEL, D_MODEL])
    c = lambda k: w[:, o[k]:o[k + 1]]
    z = lambda n: jnp.zeros((D_MODEL, n), F32)
    wp = jnp.concatenate([c(0), c(6), c(7), c(1), z(64), c(3), c(4), z(64), c(8), c(2), c(5), z(56),
                          c(9), c(10)], axis=1).astype(BF16)
    gain = jnp.concatenate([jnp.tile(q_norm_a[l], N_HEADS), jnp.tile(q_norm_b[l], N_HEADS),
                            jnp.tile(k_norm_b[l], HKV_B), k_norm_a[l], jnp.zeros((64,), F32)])[None, :]
    bias = gate_bias[l][None, :]
    lane = np.arange(LANES)
    bd = jnp.asarray((lane[:, None] // HEAD_DIM) == (lane[None, :] // HEAD_DIM), BF16)
    cos_p, sin_p = _rope_tables(jnp.arange(seq, dtype=I32))
    cos_s, sin_s = _rope_tables(past + (jnp.arange(ts, dtype=I32) % n_new))
    wa, wb, wo = w_up_a[l].astype(BF16), w_up_b[l].astype(BF16), w_out[l].astype(BF16)
    g1, gm, g2 = ffn1_norm[l][None, :], mix_norm[l][None, :], ffn2_norm[l][None, :]

    tm = 512
    xp1 = _ffn(xp, g1, *f1, tm)
    (qa_t, qb_t, qi_t, kb_t, ka_t, ki_t, vb_t, va_t, wi_t, sga_p, sgb_p) = _proj(
        xp1, gm, wp, gain, bias, cos_p, sin_p, bd, 256, seq)
    oa_p, ob_p = _attn_prompt(qi_t, qa_t, qb_t, ki_t, ka_t, kb_t, va_t, vb_t, wi_t)
    xp2 = _merge(xp1, oa_p, ob_p, sga_p, sgb_p, wa, wb, wo, tm)
    yp = _ffn(xp2, g2, *f2, tm)

    xs1 = _ffn(xs, g1, *f1, ts)
    (qa_s, qb_s, qi_s, kb_s, ka_s, ki_s, vb_s, va_s, wi_s, sga_s, sgb_s) = _proj(
        xs1, gm, wp, gain, bias, cos_s, sin_s, bd, ts, None)
    qi_r = _head_rows(qi_s, n_dec, n_new)
    qa_r = _head_rows(qa_s, n_dec, n_new)
    qb_r = _head_rows(qb_s, n_dec, n_new)
    zq = jnp.zeros_like(qb_r)
    first = (jnp.arange(ROWS) < ROWS // HKV_B)[None, :, None]
    qb2 = jnp.concatenate([jnp.where(first, qb_r, zq), jnp.where(first, zq, qb_r)], axis=2)
    wcol = _head_rows(wi_s, n_dec, n_new)
    pages_t = lambda a: jnp.swapaxes(a[l].reshape(n_pool, PAGE, -1), 1, 2)
    new_t = lambda a: jnp.swapaxes(a.reshape(n_dec, n_new, a.shape[1]), 1, 2)
    oa_r, ob_r = _attn_sample(
        page_table,
        pages_t(cache_kidx_a), pages_t(cache_k_a), pages_t(cache_v_a), pages_t(cache_k_b), pages_t(cache_v_b),
        new_t(ki_s), new_t(ka_s), new_t(va_s), new_t(kb_s), new_t(vb_s), qi_r, qa_r, qb2, wcol, SAMPLE_GROUP)
    oa_s = _from_head_rows(oa_r, n_dec, n_new).astype(BF16)
    ob_sel = jnp.where(first, ob_r[:, :, :HEAD_DIM], ob_r[:, :, HEAD_DIM:])
    ob_s = _from_head_rows(ob_sel, n_dec, n_new).astype(BF16)
    xs2 = _merge(xs1, oa_s, ob_s, sga_s, sgb_s, wa, wb, wo, ts)
    ys = _ffn(xs2, g2, *f2, ts)

    d = depth
    tok = lambda a: jnp.swapaxes(a, 1, 2)
    return (yp.reshape(n_seq, seq, D_MODEL), ys.reshape(n_dec, n_new, D_MODEL),
            tok(ka_t).reshape(d, n_seq, seq, 1, HEAD_DIM), tok(va_t).reshape(d, n_seq, seq, 1, HEAD_DIM),
            tok(ki_t).reshape(d, n_seq, seq, HEAD_DIM),
            tok(kb_t).reshape(d, n_seq, seq, HKV_B, HEAD_DIM), tok(vb_t).reshape(d, n_seq, seq, HKV_B, HEAD_DIM),
            ka_s.reshape(d, n_dec, n_new, 1, HEAD_DIM), va_s.reshape(d, n_dec, n_new, 1, HEAD_DIM),
            ki_s.reshape(d, n_dec, n_new, HEAD_DIM),
            kb_s.reshape(d, n_dec, n_new, HKV_B, HEAD_DIM), vb_s.reshape(d, n_dec, n_new, HKV_B, HEAD_DIM))
```

```python
import functools

import numpy as np
import jax
import jax.numpy as jnp
from jax import lax
from jax.experimental import pallas as pl
from jax.experimental.pallas import tpu as pltpu

F32 = jnp.float32
BF16 = jnp.bfloat16
I32 = jnp.int32

D_MODEL = 1024
HEAD_DIM = 64
N_HEADS = 8
HKV_B = 2
G_B = N_HEADS // HKV_B
D_FF = 2752
TOPK_A = 256
MOBA_BLOCK = 256
TOPK_B = 3
PAGE = 128
ROPE_THETA = 10000.0
EPS = 1e-6

LANES = 128
D_FF_PAD = 2816
FF_CHUNK = 512
W_Q = N_HEADS * HEAD_DIM

A_W = 1280
B_W = 640
C_W = 256
G_W = 2 * D_MODEL
P_W = A_W + B_W + C_W + G_W

INT_MIN = np.int32(-2 ** 31)
NEG_INF = float("-inf")
VMEM_LIMIT = 56 * 1024 * 1024


def _dot(a, b):
    return jnp.dot(a, b, preferred_element_type=F32)


def _dot_nt(a, b):
    return lax.dot_general(a, b, (((1,), (1,)), ((), ())), preferred_element_type=F32)


def _split_bf16(x):
    hi = x.astype(BF16)
    lo = (x - hi.astype(F32)).astype(BF16)
    return hi, lo


def _rms(x, g):
    ms = jnp.mean(x * x, axis=-1, keepdims=True)
    return x * lax.rsqrt(ms + EPS) * g


def _const_spec(shape):
    nd = len(shape)
    return pl.BlockSpec(shape, lambda *_: (0,) * nd)


def _ffn_kernel(x_ref, g_ref, wg_ref, wu_ref, wo_ref, o_ref):
    x = x_ref[...]
    h = _rms(x, g_ref[...]).astype(BF16)
    acc = jnp.zeros_like(x)
    for s in range(0, D_FF_PAD, FF_CHUNK):
        e = min(s + FF_CHUNK, D_FF_PAD)
        g = _dot(h, wg_ref[:, s:e])
        u = _dot(h, wu_ref[:, s:e])
        a = (g * jax.nn.sigmoid(g) * u).astype(BF16)
        acc = acc + _dot(a, wo_ref[s:e, :])
    o_ref[...] = x + 0.5 * acc


def _ffn(x, g, wg, wu, wo, tm):
    t = x.shape[0]
    return pl.pallas_call(
        _ffn_kernel,
        grid=(t // tm,),
        in_specs=[
            pl.BlockSpec((tm, D_MODEL), lambda i: (i, 0)),
            _const_spec((1, D_MODEL)),
            _const_spec((D_MODEL, D_FF_PAD)),
            _const_spec((D_MODEL, D_FF_PAD)),
            _const_spec((D_FF_PAD, D_MODEL)),
        ],
        out_specs=pl.BlockSpec((tm, D_MODEL), lambda i: (i, 0)),
        out_shape=jax.ShapeDtypeStruct((t, D_MODEL), F32),
        compiler_params=pltpu.CompilerParams(
            dimension_semantics=("arbitrary",), vmem_limit_bytes=VMEM_LIMIT),
        name="ffn",
    )(x, g, wg, wu, wo)


def _rope(y, cos, sin, first_half):
    r_lo = pltpu.roll(y, 32, 1)
    r_hi = pltpu.roll(y, 96, 1)
    return y * cos + jnp.where(first_half, r_hi, r_lo) * sin


def _proj_kernel(x_ref, gm_ref, wp_ref, gain_ref, bias_ref, cos_ref, sin_ref, bd_ref,
                 qa_ref, qb_ref, qi_ref, kb_ref, ka_ref, ki_ref, vb_ref, va_ref, wi_ref,
                 sga_ref, sgb_ref, *, transposed):
    x = x_ref[...]
    tm = x.shape[0]
    h = _rms(x, gm_ref[...]).astype(BF16)
    cos = cos_ref[...]
    sin = sin_ref[...]
    bd = bd_ref[...]
    lane = lax.broadcasted_iota(I32, (tm, LANES), 1)
    first_half = (lane % HEAD_DIM) < (HEAD_DIM // 2)

    z_a = _dot(h, wp_ref[:, 0:A_W])
    tiles_a = []
    for j in range(A_W // LANES):
        z = z_a[:, j * LANES:(j + 1) * LANES]
        hi, lo = _split_bf16(z * z)
        ms = (_dot(hi, bd) + _dot(lo, bd)) * (1.0 / HEAD_DIM)
        y = z * lax.rsqrt(ms + EPS) * gain_ref[:, j * LANES:(j + 1) * LANES]
        tiles_a.append(_rope(y, cos, sin, first_half))

    z_b = _dot(h, wp_ref[:, A_W:A_W + B_W])
    tiles_b = [_rope(z_b[:, j * LANES:(j + 1) * LANES], cos, sin, first_half)
               for j in range(B_W // LANES)]

    z_c = _dot(h, wp_ref[:, A_W + B_W:A_W + B_W + C_W])
    z_g = _dot(h, wp_ref[:, A_W + B_W + C_W:P_W]) + bias_ref[...]

    scale = HEAD_DIM ** -0.5
    qa = jnp.concatenate(tiles_a[0:4], axis=1) * scale
    qb = jnp.concatenate(tiles_a[4:8], axis=1) * scale
    qi = jnp.concatenate(tiles_b[0:4], axis=1) * scale
    wi_scale = N_HEADS ** -0.5
    if transposed:
        qa_ref[...] = qa.T.astype(BF16)
        qb_ref[...] = qb.T.astype(BF16)
        qi_ref[...] = qi.T.astype(BF16)
        kb_ref[...] = tiles_a[8].T
        ka_ref[...] = tiles_a[9].T[0:HEAD_DIM, :]
        ki_ref[...] = tiles_b[4].T[0:HEAD_DIM, :]
        zc_t = z_c.T
        vb_ref[...] = zc_t[0:LANES, :]
        va_ref[...] = zc_t[LANES:LANES + HEAD_DIM, :]
        wi_ref[...] = zc_t[LANES + HEAD_DIM:LANES + HEAD_DIM + N_HEADS, :] * wi_scale
    else:
        qa_ref[...] = qa.astype(BF16)
        qb_ref[...] = qb.astype(BF16)
        qi_ref[...] = qi.astype(BF16)
        kb_ref[...] = tiles_a[8]
        ka_ref[...] = tiles_a[9][:, 0:HEAD_DIM]
        ki_ref[...] = tiles_b[4][:, 0:HEAD_DIM]
        vb_ref[...] = z_c[:, 0:LANES]
        va_ref[...] = z_c[:, LANES:LANES + HEAD_DIM]
        wi_ref[...] = z_c[:, LANES + HEAD_DIM:LANES + HEAD_DIM + N_HEADS] * wi_scale
    sg = jax.nn.sigmoid(z_g)
    sga_ref[...] = sg[:, 0:D_MODEL]
    sgb_ref[...] = sg[:, D_MODEL:G_W]


def _proj(x, gm, wp, gain, bias, cos_t, sin_t, bd, tm, seq):
    t = x.shape[0]
    n_tab = cos_t.shape[0] // tm
    row = lambda w: pl.BlockSpec((tm, w), lambda i: (i, 0))
    widths = (W_Q, W_Q, W_Q, 2 * HEAD_DIM, HEAD_DIM, HEAD_DIM, 2 * HEAD_DIM, HEAD_DIM, N_HEADS)
    dtypes = (BF16, BF16, BF16, F32, F32, F32, F32, F32, F32)
    if seq is None:
        specs = [row(w) for w in widths]
        shapes = [jax.ShapeDtypeStruct((t, w), d) for w, d in zip(widths, dtypes)]
    else:
        nb = seq // tm
        specs = [pl.BlockSpec((None, w, tm), lambda i: (i // nb, 0, i % nb)) for w in widths]
        shapes = [jax.ShapeDtypeStruct((t // seq, w, seq), d) for w, d in zip(widths, dtypes)]
    out_specs = tuple(specs) + (row(D_MODEL), row(D_MODEL))
    out_shape = tuple(shapes) + (jax.ShapeDtypeStruct((t, D_MODEL), F32), jax.ShapeDtypeStruct((t, D_MODEL), F32))
    transposed = seq is not None
    return pl.pallas_call(
        functools.partial(_proj_kernel, transposed=transposed),
        grid=(t // tm,),
        in_specs=[
            row(D_MODEL),
            _const_spec((1, D_MODEL)),
            _const_spec((D_MODEL, P_W)),
            _const_spec((1, A_W)),
            _const_spec((1, G_W)),
            pl.BlockSpec((tm, LANES), lambda i: (i % n_tab, 0)),
            pl.BlockSpec((tm, LANES), lambda i: (i % n_tab, 0)),
            _const_spec((LANES, LANES)),
        ],
        out_specs=out_specs,
        out_shape=out_shape,
        compiler_params=pltpu.CompilerParams(
            dimension_semantics=("arbitrary",), vmem_limit_bytes=VMEM_LIMIT),
        name="proj",
    )(x, gm, wp, gain, bias, cos_t, sin_t, bd)


def _sortable(score):
    b = lax.bitcast_convert_type(score, I32)
    k = b ^ ((b >> 31) & np.int32(0x7FFFFFFF))
    return jnp.where(score == 0.0, np.int32(0), k)


def _count(mask, axis):
    return jnp.sum(jnp.where(mask, 1.0, 0.0), axis=axis, keepdims=True)


def _topk_bias(store, load_keys, key_index, axis, vec_shape, k_sel, n_index_bits):
    one = np.int32(1)

    def value_bit(it, ans):
        cand = ans | lax.shift_left(one, 31 - it)
        thr = cand ^ INT_MIN
        return jnp.where(_count(load_keys() >= thr, axis) >= k_sel, cand, ans)

    ans = lax.fori_loop(0, 32, value_bit, jnp.zeros(vec_shape, I32))
    thr = ans ^ INT_MIN
    has_tie = jnp.max(_count(load_keys() >= thr, axis)) > k_sel

    @pl.when(jnp.logical_not(has_tie))
    def _():
        store(jnp.where(load_keys() >= thr, 0.0, NEG_INF))

    @pl.when(has_tie)
    def _():
        need = k_sel - _count(load_keys() > thr, axis)

        def index_bit(it, cut):
            cand = cut | lax.shift_left(one, n_index_bits - 1 - it)
            hit = jnp.where(load_keys() == thr, jnp.where(key_index() < cand, 1.0, 0.0), 0.0)
            return jnp.where(jnp.sum(hit, axis=axis, keepdims=True) < need, cand, cut)

        cut = lax.fori_loop(0, n_index_bits, index_bit, jnp.zeros(vec_shape, I32))
        keys = load_keys()
        keep_eq = jnp.where(key_index() <= cut, 0.0, NEG_INF)
        store(jnp.where(keys > thr, 0.0, jnp.where(keys == thr, keep_eq, NEG_INF)))


V_ROWS = HEAD_DIM + 16


def _pipelined_heads(n_chunks, chunk, head_logits, values_t, lgb, store_out):
    rows = lambda c: slice(c * chunk, (c + 1) * chunk)

    def logits_stage(h, slot):
        logits = head_logits(h)
        m = None
        for c in range(n_chunks):
            lg = logits(c)
            lgb[slot, rows(c), :] = lg
            cm = jnp.max(lg, axis=0, keepdims=True)
            m = cm if m is None else jnp.maximum(m, cm)
        return m

    def step(h, m_cur, cur, nxt):
        logits = head_logits(jnp.minimum(h + 1, N_HEADS - 1))
        acc = None
        m_next = None
        for c in range(n_chunks):
            lg = logits(c)
            nxt[rows(c), :] = lg
            cm = jnp.max(lg, axis=0, keepdims=True)
            m_next = cm if m_next is None else jnp.maximum(m_next, cm)
            p = jnp.exp(cur[rows(c), :] - m_cur).astype(BF16)
            pv = _dot(values_t(h, c), p)
            acc = pv if acc is None else acc + pv
        store_out(h, acc[0:HEAD_DIM, :] / acc[HEAD_DIM:HEAD_DIM + 1, :])
        return m_next

    def body(t, m_cur):
        m_mid = step(2 * t, m_cur, lgb.at[0], lgb.at[1])
        return step(2 * t + 1, m_mid, lgb.at[1], lgb.at[0])

    lax.fori_loop(0, N_HEADS // 2, body, logits_stage(0, 0))


def _attn_prompt_kernel(qi_ref, qa_ref, qb_ref, ki_ref, ka_ref, kb_ref, va_ref, vb_ref, wit,
                        oa_ref, ob_ref,
                        kib, kab, kbb, vat, vbt, kmh, kml, sc, kint, bias, lgb, ot):
    seq = ki_ref.shape[1]
    tq = MOBA_BLOCK
    n_blk = seq // tq

    kk = jnp.concatenate([ka_ref[...], ki_ref[...]], axis=0).T
    kab[...] = kk[:, 0:HEAD_DIM].astype(BF16)
    kib[...] = kk[:, HEAD_DIM:2 * HEAD_DIM].astype(BF16)
    kb = kb_ref[...].T
    ones = jnp.ones((V_ROWS - HEAD_DIM, seq), BF16)
    vat[...] = jnp.concatenate([va_ref[...].astype(BF16), ones], axis=0)
    means = jnp.concatenate(
        [jnp.mean(kb[j * tq:(j + 1) * tq, :], axis=0, keepdims=True) for j in range(n_blk)]
        + [jnp.zeros((kmh.shape[1] - n_blk, 2 * HEAD_DIM), F32)], axis=0)
    for n in range(HKV_B):
        kbb[n] = kb[:, n * HEAD_DIM:(n + 1) * HEAD_DIM].astype(BF16)
        vbt[n] = jnp.concatenate([vb_ref[n * HEAD_DIM:(n + 1) * HEAD_DIM, :].astype(BF16), ones], axis=0)
        hi, lo = _split_bf16(means[:, n * HEAD_DIM:(n + 1) * HEAD_DIM])
        kmh[n] = hi
        kml[n] = lo

    r_loc = lax.broadcasted_iota(I32, (tq, tq), 0)
    c_loc = lax.broadcasted_iota(I32, (tq, tq), 1)
    causal = r_loc <= c_loc

    for i in range(n_blk):
        c0 = i * tq
        lk = c0 + tq
        cols = slice(c0, c0 + tq)

        sc[0:lk, :] = jnp.zeros((lk, tq), F32)

        def idx_head(h, carry):
            off = pl.multiple_of(h * HEAD_DIM, HEAD_DIM)
            s = _dot(kib[0:lk, :], qi_ref[pl.ds(off, HEAD_DIM), cols])
            sc[0:lk, :] += jnp.maximum(s, 0.0) * wit[pl.ds(h, 1), cols]
            return carry

        lax.fori_loop(0, N_HEADS, idx_head, 0)

        if i == 0:
            bias[0:tq, :] = jnp.where(causal, 0.0, NEG_INF)
        else:
            keys = _sortable(sc[0:lk, :])
            kint[0:c0, :] = keys[0:c0, :]
            kint[c0:lk, :] = jnp.where(causal, keys[c0:lk, :], INT_MIN)
            def store_bias(v):
                bias[0:lk, :] = v

            _topk_bias(store_bias, lambda: kint[0:lk, :], lambda: lax.broadcasted_iota(I32, (lk, tq), 0),
                       0, (1, tq), TOPK_A, (lk - 1).bit_length())

        blk = lambda c: slice(c * tq, (c + 1) * tq)

        def store_head(h, o):
            ot[pl.ds(pl.multiple_of(h * HEAD_DIM, HEAD_DIM), HEAD_DIM), :] = o

        def dsa_logits(h):
            q_t = qa_ref[pl.ds(pl.multiple_of(h * HEAD_DIM, HEAD_DIM), HEAD_DIM), cols]
            return lambda c: _dot(kab[blk(c), :], q_t) + bias[blk(c), :]

        _pipelined_heads(i + 1, tq, dsa_logits, lambda h, c: vat[:, blk(c)], lgb, store_head)
        oa_ref[cols, :] = ot[...].T.astype(BF16)

        def moba_logits(h):
            n = h // G_B
            q_t = qb_ref[pl.ds(pl.multiple_of(h * HEAD_DIM, HEAD_DIM), HEAD_DIM), cols]
            block_bias = [None] * i
            if i > TOPK_B:
                gate = _dot(kmh[n], q_t) + _dot(kml[n], q_t)
                rows = lax.broadcasted_iota(I32, gate.shape, 0)
                for j in range(i):
                    gj = gate[j:j + 1, :]
                    beats = jnp.where(rows < j, jnp.where(gate >= gj, 1, 0), jnp.where(gate > gj, 1, 0))
                    rank = jnp.sum(jnp.where(rows < i, beats, 0), axis=0, keepdims=True)
                    block_bias[j] = jnp.where(rank < TOPK_B, 0.0, NEG_INF)

            def logits(c):
                lg = _dot(kbb[n, blk(c), :], q_t)
                if c == i:
                    return jnp.where(causal, lg, NEG_INF)
                return lg if block_bias[c] is None else lg + block_bias[c]

            return logits

        _pipelined_heads(i + 1, tq, moba_logits, lambda h, c: vbt[h // G_B, :, blk(c)], lgb, store_head)
        ob_ref[cols, :] = ot[...].T.astype(BF16)


def _attn_prompt(qi_t, qa_t, qb_t, ki_t, ka_t, kb_t, va_t, vb_t, wi_t):
    n_seq, _, seq = qi_t.shape
    tq = MOBA_BLOCK
    fm = lambda w: pl.BlockSpec((None, w, seq), lambda b: (b, 0, 0))
    row = lambda w: pl.BlockSpec((seq, w), lambda b: (b, 0))
    return pl.pallas_call(
        _attn_prompt_kernel,
        grid=(n_seq,),
        in_specs=[fm(W_Q), fm(W_Q), fm(W_Q), fm(HEAD_DIM), fm(HEAD_DIM), fm(2 * HEAD_DIM),
                  fm(HEAD_DIM), fm(2 * HEAD_DIM), fm(N_HEADS)],
        out_specs=(row(W_Q), row(W_Q)),
        out_shape=(jax.ShapeDtypeStruct((n_seq * seq, W_Q), BF16),
                   jax.ShapeDtypeStruct((n_seq * seq, W_Q), BF16)),
        scratch_shapes=[
            pltpu.VMEM((seq, HEAD_DIM), BF16),
            pltpu.VMEM((seq, HEAD_DIM), BF16),
            pltpu.VMEM((HKV_B, seq, HEAD_DIM), BF16),
            pltpu.VMEM((V_ROWS, seq), BF16),
            pltpu.VMEM((HKV_B, V_ROWS, seq), BF16),
            pltpu.VMEM((HKV_B, 16, HEAD_DIM), BF16),
            pltpu.VMEM((HKV_B, 16, HEAD_DIM), BF16),
            pltpu.VMEM((seq, tq), F32),
            pltpu.VMEM((seq, tq), I32),
            pltpu.VMEM((seq, tq), F32),
            pltpu.VMEM((2, seq, tq), F32),
            pltpu.VMEM((W_Q, tq), F32),
        ],
        compiler_params=pltpu.CompilerParams(
            dimension_semantics=("arbitrary",), vmem_limit_bytes=VMEM_LIMIT),
        name="attn_prompt",
    )(qi_t, qa_t, qb_t, ki_t, ka_t, kb_t, va_t, vb_t, wi_t)


ROWS = N_HEADS * 8
SAMPLE_GROUP = 4


def _attn_sample_kernel(pt_ref, cki_hbm, cka_hbm, cva_hbm, ckb_hbm, cvb_hbm,
                        nki_ref, nka_ref, nva_ref, nkb_ref, nvb_ref,
                        qi_ref, qa_ref, qb_ref, wc_ref,
                        oa_ref, ob_ref,
                        ski, ska, sva, skb, svb, kint, bias, sem,
                        *, group, n_new, past):
    g = pl.program_id(0)
    n_groups = pl.num_programs(0)
    n_pages = past // PAGE
    lpad = ski.shape[3]
    slot = g % 2
    pools = ((cki_hbm, ski), (cka_hbm, ska), (cva_hbm, sva), (ckb_hbm, skb), (cvb_hbm, svb))

    def page_copies(grp, to_slot, t):
        s = t // n_pages
        p = t % n_pages
        page = pt_ref[grp * group + s, p]
        lanes = pl.ds(pl.multiple_of(p * PAGE, PAGE), PAGE)
        return [pltpu.make_async_copy(hbm.at[page], slab.at[to_slot, s, :, lanes], sem.at[to_slot])
                for hbm, slab in pools]

    def start_group(grp, to_slot):
        def body(t, carry):
            for cp in page_copies(grp, to_slot, t):
                cp.start()
            return carry
        lax.fori_loop(0, group * n_pages, body, 0)

    def wait_group(grp, to_slot):
        def body(t, carry):
            for cp in page_copies(grp, to_slot, t):
                cp.wait()
            return carry
        lax.fori_loop(0, group * n_pages, body, 0)

    @pl.when(g == 0)
    def _():
        start_group(0, 0)

    @pl.when(g + 1 < n_groups)
    def _():
        start_group(g + 1, 1 - slot)

    wait_group(g, slot)

    n_blk = past // MOBA_BLOCK
    col = lax.broadcasted_iota(I32, (8, lpad), 1)
    qrow = lax.broadcasted_iota(I32, (8, lpad), 0)
    visible8 = col <= past + jnp.minimum(qrow, n_new - 1)
    for s in range(group):
        for slab, new in ((ski, nki_ref), (ska, nka_ref), (sva, nva_ref), (skb, nkb_ref), (svb, nvb_ref)):
            slab[slot, s, :, past:lpad] = jnp.zeros((slab.shape[2], lpad - past), F32)
            slab[slot, s, :, past:past + n_new] = new[s]
        st = _dot(qi_ref[s], ski[slot, s].astype(BF16))
        st = jnp.maximum(st, 0.0) * wc_ref[s]
        score = st[0:8, :]
        for h in range(1, N_HEADS):
            score = score + st[h * 8:(h + 1) * 8, :]
        kint[s * 8:(s + 1) * 8, :] = jnp.where(visible8, _sortable(score), INT_MIN)

    rows = group * 8

    def store_bias(v):
        bias[...] = v

    _topk_bias(store_bias, lambda: kint[...], lambda: lax.broadcasted_iota(I32, (rows, lpad), 1),
               1, (rows, 1), TOPK_A, (lpad - 1).bit_length())

    colr = lax.broadcasted_iota(I32, (ROWS, LANES), 1)
    qr = lax.broadcasted_iota(I32, (ROWS, LANES), 0) % 8
    visible_tail = colr <= jnp.minimum(qr, n_new - 1)
    for s in range(group):
        lg = _dot(qa_ref[s], ska[slot, s].astype(BF16)) + jnp.tile(bias[s * 8:(s + 1) * 8, :], (N_HEADS, 1))
        m = jnp.max(lg, axis=1, keepdims=True)
        p = jnp.exp(lg - m)
        l = jnp.sum(p, axis=1, keepdims=True)
        oa_ref[s] = _dot_nt(p.astype(BF16), sva[slot, s].astype(BF16)) / l

        lgb = _dot(qb_ref[s], skb[slot, s].astype(BF16))
        gate = [jnp.sum(lgb[:, j * MOBA_BLOCK:(j + 1) * MOBA_BLOCK], axis=1, keepdims=True)
                for j in range(n_blk)]
        pieces = []
        for j in range(n_blk):
            rank = jnp.zeros((ROWS, 1), I32)
            for j2 in range(n_blk):
                if j2 != j:
                    rank = rank + jnp.where((gate[j2] >= gate[j]) if j2 < j else (gate[j2] > gate[j]), 1, 0)
            pieces.append(lgb[:, j * MOBA_BLOCK:(j + 1) * MOBA_BLOCK] + jnp.where(rank < TOPK_B, 0.0, NEG_INF))
        pieces.append(jnp.where(visible_tail, lgb[:, past:lpad], NEG_INF))
        lgm = jnp.concatenate(pieces, axis=1)
        m = jnp.max(lgm, axis=1, keepdims=True)
        p = jnp.exp(lgm - m)
        l = jnp.sum(p, axis=1, keepdims=True)
        ob_ref[s] = _dot_nt(p.astype(BF16), svb[slot, s].astype(BF16)) / l


def _attn_sample(page_table, cki, cka, cva, ckb, cvb, nki, nka, nva, nkb, nvb, qi, qa, qb2, wcol, group):
    n_seq, n_pages = page_table.shape
    n_new = nki.shape[2]
    past = n_pages * PAGE
    lpad = past + LANES
    hbm = pl.BlockSpec(memory_space=pl.ANY)
    grp = lambda a, b: pl.BlockSpec((group, a, b), lambda g, pt: (g, 0, 0))
    slab = lambda w: pltpu.VMEM((2, group, w, lpad), F32)
    grid_spec = pltpu.PrefetchScalarGridSpec(
        num_scalar_prefetch=1,
        grid=(n_seq // group,),
        in_specs=[hbm, hbm, hbm, hbm, hbm,
                  grp(HEAD_DIM, n_new), grp(HEAD_DIM, n_new), grp(HEAD_DIM, n_new),
                  grp(2 * HEAD_DIM, n_new), grp(2 * HEAD_DIM, n_new),
                  grp(ROWS, HEAD_DIM), grp(ROWS, HEAD_DIM), grp(ROWS, 2 * HEAD_DIM), grp(ROWS, 1)],
        out_specs=(grp(ROWS, HEAD_DIM), grp(ROWS, 2 * HEAD_DIM)),
        scratch_shapes=[
            slab(HEAD_DIM),
            slab(HEAD_DIM),
            slab(HEAD_DIM),
            slab(2 * HEAD_DIM),
            slab(2 * HEAD_DIM),
            pltpu.VMEM((group * 8, lpad), I32),
            pltpu.VMEM((group * 8, lpad), F32),
            pltpu.SemaphoreType.DMA((2,)),
        ],
    )
    return pl.pallas_call(
        functools.partial(_attn_sample_kernel, group=group, n_new=n_new, past=past),
        grid_spec=grid_spec,
        out_shape=(jax.ShapeDtypeStruct((n_seq, ROWS, HEAD_DIM), F32),
                   jax.ShapeDtypeStruct((n_seq, ROWS, 2 * HEAD_DIM), F32)),
        compiler_params=pltpu.CompilerParams(
            dimension_semantics=("arbitrary",), vmem_limit_bytes=VMEM_LIMIT),
        name="attn_sample",
    )(page_table, cki, cka, cva, ckb, cvb, nki, nka, nva, nkb, nvb, qi, qa, qb2, wcol)


def _merge_kernel(x_ref, oa_ref, ob_ref, sga_ref, sgb_ref, wa_ref, wb_ref, wo_ref, o_ref):
    m = sga_ref[...] * _dot(oa_ref[...], wa_ref[...]) + sgb_ref[...] * _dot(ob_ref[...], wb_ref[...])
    o_ref[...] = x_ref[...] + _dot(m.astype(BF16), wo_ref[...])


def _merge(x, oa, ob, sga, sgb, wa, wb, wo, tm):
    t = x.shape[0]
    row = lambda w: pl.BlockSpec((tm, w), lambda i: (i, 0))
    return pl.pallas_call(
        _merge_kernel,
        grid=(t // tm,),
        in_specs=[row(D_MODEL), row(W_Q), row(W_Q), row(D_MODEL), row(D_MODEL),
                  _const_spec((W_Q, D_MODEL)), _const_spec((W_Q, D_MODEL)), _const_spec((D_MODEL, D_MODEL))],
        out_specs=row(D_MODEL),
        out_shape=jax.ShapeDtypeStruct((t, D_MODEL), F32),
        compiler_params=pltpu.CompilerParams(
            dimension_semantics=("arbitrary",), vmem_limit_bytes=VMEM_LIMIT),
        name="merge",
    )(x, oa, ob, sga, sgb, wa, wb, wo)


def _ffn_weights(w_in, w_out):
    pad = D_FF_PAD - D_FF
    wg = jnp.pad(w_in[:, :D_FF], ((0, 0), (0, pad))).astype(BF16)
    wu = jnp.pad(w_in[:, D_FF:], ((0, 0), (0, pad))).astype(BF16)
    wo = jnp.pad(w_out, ((0, pad), (0, 0))).astype(BF16)
    return wg, wu, wo


def _rope_tables(pos):
    half = HEAD_DIM // 2
    inv = ROPE_THETA ** (-jnp.arange(half, dtype=F32) / half)
    ang = pos.astype(F32)[:, None] * inv[None, :]
    cos, sin = jnp.cos(ang), jnp.sin(ang)
    return (jnp.concatenate([cos, cos, cos, cos], axis=1),
            jnp.concatenate([-sin, sin, -sin, sin], axis=1))


def _head_rows(a, n_seq, n_new):
    w = a.shape[1] // N_HEADS
    a = a.reshape(n_seq, n_new, N_HEADS, w).transpose(0, 2, 1, 3)
    a = jnp.pad(a, ((0, 0), (0, 0), (0, 8 - n_new), (0, 0)))
    return a.reshape(n_seq, ROWS, w)


def _from_head_rows(a, n_seq, n_new):
    w = a.shape[2]
    a = a.reshape(n_seq, N_HEADS, 8, w)[:, :, :n_new].transpose(0, 2, 1, 3)
    return a.reshape(n_seq * n_new, N_HEADS * w)


def kernel(x_prompt, x_sample, cache_k_a, cache_v_a, cache_kidx_a, cache_k_b, cache_v_b, page_table,
           ffn1_norm, ffn1_w_in, ffn1_w_out, mix_norm, w_in, q_norm_a, k_norm_a, q_norm_b, k_norm_b,
           gate_bias, w_up_a, w_up_b, w_out, ffn2_norm, ffn2_w_in, ffn2_w_out):
    n_seq, seq, _ = x_prompt.shape
    n_dec, n_new, _ = x_sample.shape
    depth = ffn1_norm.shape[0]
    n_pool = cache_k_a.shape[1]
    past = page_table.shape[1] * PAGE
    assert depth == 1 and seq % MOBA_BLOCK == 0 and past % MOBA_BLOCK == 0 and n_new <= 8
    assert seq >= 4 * TOPK_A and past + n_new >= 4 * TOPK_A
    l = 0
    tp, ts = n_seq * seq, n_dec * n_new

    xp = x_prompt.reshape(tp, D_MODEL)
    xs = x_sample.reshape(ts, D_MODEL)

    f1 = _ffn_weights(ffn1_w_in[l], ffn1_w_out[l])
    f2 = _ffn_weights(ffn2_w_in[l], ffn2_w_out[l])
    w = w_in[l]
    o = np.cumsum([0, W_Q, HEAD_DIM, HEAD_DIM, W_Q, HEAD_DIM, N_HEADS, W_Q, 2 * HEAD_DIM, 2 * HEAD_DIM,
                   D_MODEL, D_MODEL])
    c = lambda k: w[:, o[k]:o[k + 1]]
    z = lambda n: jnp.zeros((D_MODEL, n), F32)
    wp = jnp.concatenate([c(0), c(6), c(7), c(1), z(64), c(3), c(4), z(64), c(8), c(2), c(5), z(56),
                          c(9), c(10)], axis=1).astype(BF16)
    gain = jnp.concatenate([jnp.tile(q_norm_a[l], N_HEADS), jnp.tile(q_norm_b[l], N_HEADS),
                            jnp.tile(k_norm_b[l], HKV_B), k_norm_a[l], jnp.zeros((64,), F32)])[None, :]
    bias = gate_bias[l][None, :]
    lane = np.arange(LANES)
    bd = jnp.asarray((lane[:, None] // HEAD_DIM) == (lane[None, :] // HEAD_DIM), BF16)
    cos_p, sin_p = _rope_tables(jnp.arange(seq, dtype=I32))
    cos_s, sin_s = _rope_tables(past + (jnp.arange(ts, dtype=I32) % n_new))
    wa, wb, wo = w_up_a[l].astype(BF16), w_up_b[l].astype(BF16), w_out[l].astype(BF16)
    g1, gm, g2 = ffn1_norm[l][None, :], mix_norm[l][None, :], ffn2_norm[l][None, :]

    tm = 512
    xp1 = _ffn(xp, g1, *f1, tm)
    (qa_t, qb_t, qi_t, kb_t, ka_t, ki_t, vb_t, va_t, wi_t, sga_p, sgb_p) = _proj(
        xp1, gm, wp, gain, bias, cos_p, sin_p, bd, 256, seq)
    oa_p, ob_p = _attn_prompt(qi_t, qa_t, qb_t, ki_t, ka_t, kb_t, va_t, vb_t, wi_t)
    xp2 = _merge(xp1, oa_p, ob_p, sga_p, sgb_p, wa, wb, wo, tm)
    yp = _ffn(xp2, g2, *f2, tm)

    xs1 = _ffn(xs, g1, *f1, ts)
    (qa_s, qb_s, qi_s, kb_s, ka_s, ki_s, vb_s, va_s, wi_s, sga_s, sgb_s) = _proj(
        xs1, gm, wp, gain, bias, cos_s, sin_s, bd, ts, None)
    qi_r = _head_rows(qi_s, n_dec, n_new)
    qa_r = _head_rows(qa_s, n_dec, n_new)
    qb_r = _head_rows(qb_s, n_dec, n_new)
    zq = jnp.zeros_like(qb_r)
    first = (jnp.arange(ROWS) < ROWS // HKV_B)[None, :, None]
    qb2 = jnp.concatenate([jnp.where(first, qb_r, zq), jnp.where(first, zq, qb_r)], axis=2)
    wcol = _head_rows(wi_s, n_dec, n_new)
    pages_t = lambda a: jnp.swapaxes(a[l].reshape(n_pool, PAGE, -1), 1, 2)
    new_t = lambda a: jnp.swapaxes(a.reshape(n_dec, n_new, a.shape[1]), 1, 2)
    oa_r, ob_r = _attn_sample(
        page_table,
        pages_t(cache_kidx_a), pages_t(cache_k_a), pages_t(cache_v_a), pages_t(cache_k_b), pages_t(cache_v_b),
        new_t(ki_s), new_t(ka_s), new_t(va_s), new_t(kb_s), new_t(vb_s), qi_r, qa_r, qb2, wcol, SAMPLE_GROUP)
    oa_s = _from_head_rows(oa_r, n_dec, n_new).astype(BF16)
    ob_sel = jnp.where(first, ob_r[:, :, :HEAD_DIM], ob_r[:, :, HEAD_DIM:])
    ob_s = _from_head_rows(ob_sel, n_dec, n_new).astype(BF16)
    xs2 = _merge(xs1, oa_s, ob_s, sga_s, sgb_s, wa, wb, wo, ts)
    ys = _ffn(xs2, g2, *f2, ts)

    d = depth
    tok = lambda a: jnp.swapaxes(a, 1, 2)
    return (yp.reshape(n_seq, seq, D_MODEL), ys.reshape(n_dec, n_new, D_MODEL),
            tok(ka_t).reshape(d, n_seq, seq, 1, HEAD_DIM), tok(va_t).reshape(d, n_seq, seq, 1, HEAD_DIM),
            tok(ki_t).reshape(d, n_seq, seq, HEAD_DIM),
            tok(kb_t).reshape(d, n_seq, seq, HKV_B, HEAD_DIM), tok(vb_t).reshape(d, n_seq, seq, HKV_B, HEAD_DIM),
            ka_s.reshape(d, n_dec, n_new, 1, HEAD_DIM), va_s.reshape(d, n_dec, n_new, 1, HEAD_DIM),
            ki_s.reshape(d, n_dec, n_new, HEAD_DIM),
            kb_s.reshape(d, n_dec, n_new, HKV_B, HEAD_DIM), vb_s.reshape(d, n_dec, n_new, HKV_B, HEAD_DIM))
```

```python
import functools

import numpy as np
import jax
import jax.numpy as jnp
from jax import lax
from jax.experimental import pallas as pl
from jax.experimental.pallas import tpu as pltpu

F32 = jnp.float32
BF16 = jnp.bfloat16
I32 = jnp.int32

D_MODEL = 1024
HEAD_DIM = 64
N_HEADS = 8
HKV_B = 2
G_B = N_HEADS // HKV_B
D_FF = 2752
TOPK_A = 256
MOBA_BLOCK = 256
TOPK_B = 3
PAGE = 128
ROPE_THETA = 10000.0
EPS = 1e-6

LANES = 128
D_FF_PAD = 2816
FF_CHUNK = 512
W_Q = N_HEADS * HEAD_DIM

A_W = 1280
B_W = 640
C_W = 256
G_W = 2 * D_MODEL
P_W = A_W + B_W + C_W + G_W

INT_MIN = np.int32(-2 ** 31)
NEG_INF = float("-inf")
VMEM_LIMIT = 56 * 1024 * 1024


def _dot(a, b):
    return jnp.dot(a, b, preferred_element_type=F32)


def _dot_nt(a, b):
    return lax.dot_general(a, b, (((1,), (1,)), ((), ())), preferred_element_type=F32)


def _split_bf16(x):
    hi = x.astype(BF16)
    lo = (x - hi.astype(F32)).astype(BF16)
    return hi, lo


def _rms(x, g):
    ms = jnp.mean(x * x, axis=-1, keepdims=True)
    return x * lax.rsqrt(ms + EPS) * g


def _const_spec(shape):
    nd = len(shape)
    return pl.BlockSpec(shape, lambda *_: (0,) * nd)


def _ffn_kernel(x_ref, g_ref, wg_ref, wu_ref, wo_ref, o_ref):
    x = x_ref[...]
    h = _rms(x, g_ref[...]).astype(BF16)
    acc = jnp.zeros_like(x)
    for s in range(0, D_FF_PAD, FF_CHUNK):
        e = min(s + FF_CHUNK, D_FF_PAD)
        g = _dot(h, wg_ref[:, s:e])
        u = _dot(h, wu_ref[:, s:e])
        a = (g * jax.nn.sigmoid(g) * u).astype(BF16)
        acc = acc + _dot(a, wo_ref[s:e, :])
    o_ref[...] = x + 0.5 * acc


def _ffn(x, g, wg, wu, wo, tm):
    t = x.shape[0]
    return pl.pallas_call(
        _ffn_kernel,
        grid=(t // tm,),
        in_specs=[
            pl.BlockSpec((tm, D_MODEL), lambda i: (i, 0)),
            _const_spec((1, D_MODEL)),
            _const_spec((D_MODEL, D_FF_PAD)),
            _const_spec((D_MODEL, D_FF_PAD)),
            _const_spec((D_FF_PAD, D_MODEL)),
        ],
        out_specs=pl.BlockSpec((tm, D_MODEL), lambda i: (i, 0)),
        out_shape=jax.ShapeDtypeStruct((t, D_MODEL), F32),
        compiler_params=pltpu.CompilerParams(
            dimension_semantics=("arbitrary",), vmem_limit_bytes=VMEM_LIMIT),
        name="ffn",
    )(x, g, wg, wu, wo)


def _rope(y, cos, sin, first_half):
    r_lo = pltpu.roll(y, 32, 1)
    r_hi = pltpu.roll(y, 96, 1)
    return y * cos + jnp.where(first_half, r_hi, r_lo) * sin


def _proj_kernel(x_ref, gm_ref, wp_ref, gain_ref, bias_ref, cos_ref, sin_ref, bd_ref,
                 qa_ref, qb_ref, qi_ref, kb_ref, ka_ref, ki_ref, vb_ref, va_ref, wi_ref,
                 sga_ref, sgb_ref, *, transposed):
    x = x_ref[...]
    tm = x.shape[0]
    h = _rms(x, gm_ref[...]).astype(BF16)
    cos = cos_ref[...]
    sin = sin_ref[...]
    bd = bd_ref[...]
    lane = lax.broadcasted_iota(I32, (tm, LANES), 1)
    first_half = (lane % HEAD_DIM) < (HEAD_DIM // 2)

    z_a = _dot(h, wp_ref[:, 0:A_W])
    tiles_a = []
    for j in range(A_W // LANES):
        z = z_a[:, j * LANES:(j + 1) * LANES]
        hi, lo = _split_bf16(z * z)
        ms = (_dot(hi, bd) + _dot(lo, bd)) * (1.0 / HEAD_DIM)
        y = z * lax.rsqrt(ms + EPS) * gain_ref[:, j * LANES:(j + 1) * LANES]
        tiles_a.append(_rope(y, cos, sin, first_half))

    z_b = _dot(h, wp_ref[:, A_W:A_W + B_W])
    tiles_b = [_rope(z_b[:, j * LANES:(j + 1) * LANES], cos, sin, first_half)
               for j in range(B_W // LANES)]

    z_c = _dot(h, wp_ref[:, A_W + B_W:A_W + B_W + C_W])
    z_g = _dot(h, wp_ref[:, A_W + B_W + C_W:P_W]) + bias_ref[...]

    scale = HEAD_DIM ** -0.5
    qa = jnp.concatenate(tiles_a[0:4], axis=1) * scale
    qb = jnp.concatenate(tiles_a[4:8], axis=1) * scale
    qi = jnp.concatenate(tiles_b[0:4], axis=1) * scale
    wi_scale = N_HEADS ** -0.5
    if transposed:
        qa_ref[...] = qa.T.astype(BF16)
        qb_ref[...] = qb.T.astype(BF16)
        qi_ref[...] = qi.T.astype(BF16)
        kb_ref[...] = tiles_a[8].T
        ka_ref[...] = tiles_a[9].T[0:HEAD_DIM, :]
        ki_ref[...] = tiles_b[4].T[0:HEAD_DIM, :]
        zc_t = z_c.T
        vb_ref[...] = zc_t[0:LANES, :]
        va_ref[...] = zc_t[LANES:LANES + HEAD_DIM, :]
        wi_ref[...] = zc_t[LANES + HEAD_DIM:LANES + HEAD_DIM + N_HEADS, :] * wi_scale
    else:
        qa_ref[...] = qa.astype(BF16)
        qb_ref[...] = qb.astype(BF16)
        qi_ref[...] = qi.astype(BF16)
        kb_ref[...] = tiles_a[8]
        ka_ref[...] = tiles_a[9][:, 0:HEAD_DIM]
        ki_ref[...] = tiles_b[4][:, 0:HEAD_DIM]
        vb_ref[...] = z_c[:, 0:LANES]
        va_ref[...] = z_c[:, LANES:LANES + HEAD_DIM]
        wi_ref[...] = z_c[:, LANES + HEAD_DIM:LANES + HEAD_DIM + N_HEADS] * wi_scale
    sg = jax.nn.sigmoid(z_g)
    sga_ref[...] = sg[:, 0:D_MODEL]
    sgb_ref[...] = sg[:, D_MODEL:G_W]


def _proj(x, gm, wp, gain, bias, cos_t, sin_t, bd, tm, seq):
    t = x.shape[0]
    n_tab = cos_t.shape[0] // tm
    row = lambda w: pl.BlockSpec((tm, w), lambda i: (i, 0))
    widths = (W_Q, W_Q, W_Q, 2 * HEAD_DIM, HEAD_DIM, HEAD_DIM, 2 * HEAD_DIM, HEAD_DIM, N_HEADS)
    dtypes = (BF16, BF16, BF16, F32, F32, F32, F32, F32, F32)
    if seq is None:
        specs = [row(w) for w in widths]
        shapes = [jax.ShapeDtypeStruct((t, w), d) for w, d in zip(widths, dtypes)]
    else:
        nb = seq // tm
        specs = [pl.BlockSpec((None, w, tm), lambda i: (i // nb, 0, i % nb)) for w in widths]
        shapes = [jax.ShapeDtypeStruct((t // seq, w, seq), d) for w, d in zip(widths, dtypes)]
    out_specs = tuple(specs) + (row(D_MODEL), row(D_MODEL))
    out_shape = tuple(shapes) + (jax.ShapeDtypeStruct((t, D_MODEL), F32), jax.ShapeDtypeStruct((t, D_MODEL), F32))
    transposed = seq is not None
    return pl.pallas_call(
        functools.partial(_proj_kernel, transposed=transposed),
        grid=(t // tm,),
        in_specs=[
            row(D_MODEL),
            _const_spec((1, D_MODEL)),
            _const_spec((D_MODEL, P_W)),
            _const_spec((1, A_W)),
            _const_spec((1, G_W)),
            pl.BlockSpec((tm, LANES), lambda i: (i % n_tab, 0)),
            pl.BlockSpec((tm, LANES), lambda i: (i % n_tab, 0)),
            _const_spec((LANES, LANES)),
        ],
        out_specs=out_specs,
        out_shape=out_shape,
        compiler_params=pltpu.CompilerParams(
            dimension_semantics=("arbitrary",), vmem_limit_bytes=VMEM_LIMIT),
        name="proj",
    )(x, gm, wp, gain, bias, cos_t, sin_t, bd)


def _count(mask, axis):
    return jnp.sum(jnp.where(mask, 1.0, 0.0), axis=axis, keepdims=True)


def _key_to_score(key):
    return lax.bitcast_convert_type(key ^ ((key >> 31) & np.int32(0x7FFFFFFF)), F32)


def _kth_largest(load_scores, axis, vec_shape, k_sel):
    one = np.int32(1)

    def value_bit(it, ans):
        cand = ans | lax.shift_left(one, 31 - it)
        thr = _key_to_score(cand ^ INT_MIN)
        return jnp.where(_count(load_scores() >= thr, axis) >= k_sel, cand, ans)

    return _key_to_score(lax.fori_loop(0, 32, value_bit, jnp.zeros(vec_shape, I32)) ^ INT_MIN)


def _score_to_key(score):
    b = lax.bitcast_convert_type(score, I32)
    return b ^ ((b >> 31) & np.int32(0x7FFFFFFF))


def _count_rows16(mask):
    rows, q = mask.shape
    assert rows % 16 == 0 and rows // 16 <= 256
    ones = jnp.where(mask, jnp.ones((), BF16), jnp.zeros((), BF16))
    part = ones[0:16, :]
    for r in range(16, rows, 16):
        part = part + ones[r:r + 16, :]
    return jnp.sum(part.astype(F32), axis=0, keepdims=True)


def _kth_largest_rows(load_scores, coarse_ref, q_cols, k_sel):
    n = load_scores().shape[0]
    coarse_ref[0:n, :] = load_scores().astype(BF16)
    one = np.int32(1)

    def bf16_of(pattern):
        k = pattern - np.int32(0x8000)
        bits = k ^ ((k >> 15) & np.int32(0x7FFF))
        return lax.bitcast_convert_type(lax.shift_left(bits, 16), F32).astype(BF16)

    def coarse_bit(it, ans):
        cand = ans | lax.shift_left(one, 15 - it)
        return jnp.where(_count_rows16(coarse_ref[0:n, :] >= bf16_of(cand)) >= k_sel, cand, ans)

    coarse = lax.fori_loop(0, 16, coarse_bit, jnp.zeros((1, q_cols), I32))
    base = _score_to_key(bf16_of(coarse).astype(F32)) - np.int32(2 ** 15)

    def fine_bit(it, off):
        cand = off | lax.shift_left(one, 16 - it)
        return jnp.where(_count(load_scores() >= _key_to_score(base + cand), 0) >= k_sel, cand, off)

    return _key_to_score(base + lax.fori_loop(0, 17, fine_bit, jnp.zeros((1, q_cols), I32)))


def _topk_bias(store, load_keys, key_index, axis, vec_shape, k_sel, n_index_bits, coarse_ref=None):
    one = np.int32(1)
    if coarse_ref is None:
        thr = _kth_largest(load_keys, axis, vec_shape, k_sel)
    else:
        thr = _kth_largest_rows(load_keys, coarse_ref, vec_shape[1], k_sel)
    has_tie = jnp.max(_count(load_keys() >= thr, axis)) > k_sel

    @pl.when(jnp.logical_not(has_tie))
    def _():
        store(jnp.where(load_keys() >= thr, 0.0, NEG_INF))

    @pl.when(has_tie)
    def _():
        need = k_sel - _count(load_keys() > thr, axis)

        def index_bit(it, cut):
            cand = cut | lax.shift_left(one, n_index_bits - 1 - it)
            hit = jnp.where(load_keys() == thr, jnp.where(key_index() < cand, 1.0, 0.0), 0.0)
            return jnp.where(jnp.sum(hit, axis=axis, keepdims=True) < need, cand, cut)

        cut = lax.fori_loop(0, n_index_bits, index_bit, jnp.zeros(vec_shape, I32))
        keys = load_keys()
        keep_eq = jnp.where(key_index() <= cut, 0.0, NEG_INF)
        store(jnp.where(keys > thr, 0.0, jnp.where(keys == thr, keep_eq, NEG_INF)))


V_ROWS = HEAD_DIM + 16


def _pipelined_heads(n_chunks, chunk, head_logits, values_t, lgb, store_out):
    rows = lambda c: slice(c * chunk, (c + 1) * chunk)

    def logits_stage(h, slot):
        logits = head_logits(h)
        m = None
        for c in range(n_chunks):
            lg = logits(c)
            lgb[slot, rows(c), :] = lg
            cm = jnp.max(lg, axis=0, keepdims=True)
            m = cm if m is None else jnp.maximum(m, cm)
        return m

    def step(h, m_cur, cur, nxt):
        logits = head_logits(jnp.minimum(h + 1, N_HEADS - 1))
        acc = None
        m_next = None
        for c in range(n_chunks):
            lg = logits(c)
            nxt[rows(c), :] = lg
            cm = jnp.max(lg, axis=0, keepdims=True)
            m_next = cm if m_next is None else jnp.maximum(m_next, cm)
            p = jnp.exp(cur[rows(c), :] - m_cur).astype(BF16)
            pv = _dot(values_t(h, c), p)
            acc = pv if acc is None else acc + pv
        store_out(h, acc[0:HEAD_DIM, :] / acc[HEAD_DIM:HEAD_DIM + 1, :])
        return m_next

    def body(t, m_cur):
        m_mid = step(2 * t, m_cur, lgb.at[0], lgb.at[1])
        return step(2 * t + 1, m_mid, lgb.at[1], lgb.at[0])

    lax.fori_loop(0, N_HEADS // 2, body, logits_stage(0, 0))


def _attn_prompt_kernel(qi_ref, qa_ref, qb_ref, ki_ref, ka_ref, kb_ref, va_ref, vb_ref, wit,
                        oa_ref, ob_ref,
                        kib, kab, kbb, vat, vbt, kmh, kml, sc, sc16, bias, lgb, ot):
    seq = ki_ref.shape[1]
    tq = MOBA_BLOCK
    n_blk = seq // tq

    kk = jnp.concatenate([ka_ref[...], ki_ref[...]], axis=0).T
    kab[...] = kk[:, 0:HEAD_DIM].astype(BF16)
    kib[...] = kk[:, HEAD_DIM:2 * HEAD_DIM].astype(BF16)
    kb = kb_ref[...].T
    ones = jnp.ones((V_ROWS - HEAD_DIM, seq), BF16)
    vat[...] = jnp.concatenate([va_ref[...].astype(BF16), ones], axis=0)
    means = jnp.concatenate(
        [jnp.mean(kb[j * tq:(j + 1) * tq, :], axis=0, keepdims=True) for j in range(n_blk)]
        + [jnp.zeros((kmh.shape[1] - n_blk, 2 * HEAD_DIM), F32)], axis=0)
    for n in range(HKV_B):
        kbb[n] = kb[:, n * HEAD_DIM:(n + 1) * HEAD_DIM].astype(BF16)
        vbt[n] = jnp.concatenate([vb_ref[n * HEAD_DIM:(n + 1) * HEAD_DIM, :].astype(BF16), ones], axis=0)
        hi, lo = _split_bf16(means[:, n * HEAD_DIM:(n + 1) * HEAD_DIM])
        kmh[n] = hi
        kml[n] = lo

    r_loc = lax.broadcasted_iota(I32, (tq, tq), 0)
    c_loc = lax.broadcasted_iota(I32, (tq, tq), 1)
    causal = r_loc <= c_loc

    for i in range(n_blk):
        c0 = i * tq
        lk = c0 + tq
        cols = slice(c0, c0 + tq)

        sc[0:lk, :] = jnp.zeros((lk, tq), F32)

        def idx_head(h, carry):
            off = pl.multiple_of(h * HEAD_DIM, HEAD_DIM)
            s = _dot(kib[0:lk, :], qi_ref[pl.ds(off, HEAD_DIM), cols])
            sc[0:lk, :] += jnp.maximum(s, 0.0) * wit[pl.ds(h, 1), cols]
            return carry

        lax.fori_loop(0, N_HEADS, idx_head, 0)

        if i == 0:
            bias[0:tq, :] = jnp.where(causal, 0.0, NEG_INF)
        else:
            sc[c0:lk, :] = jnp.where(causal, sc[c0:lk, :], NEG_INF)

            def store_bias(v):
                bias[0:lk, :] = v

            _topk_bias(store_bias, lambda: sc[0:lk, :], lambda: lax.broadcasted_iota(I32, (lk, tq), 0),
                       0, (1, tq), TOPK_A, (lk - 1).bit_length(), coarse_ref=sc16)

        blk = lambda c: slice(c * tq, (c + 1) * tq)

        def store_head(h, o):
            ot[pl.ds(pl.multiple_of(h * HEAD_DIM, HEAD_DIM), HEAD_DIM), :] = o

        def dsa_logits(h):
            q_t = qa_ref[pl.ds(pl.multiple_of(h * HEAD_DIM, HEAD_DIM), HEAD_DIM), cols]
            return lambda c: _dot(kab[blk(c), :], q_t) + bias[blk(c), :]

        _pipelined_heads(i + 1, tq, dsa_logits, lambda h, c: vat[:, blk(c)], lgb, store_head)
        oa_ref[cols, :] = ot[...].T.astype(BF16)

        def moba_logits(h):
            n = h // G_B
            q_t = qb_ref[pl.ds(pl.multiple_of(h * HEAD_DIM, HEAD_DIM), HEAD_DIM), cols]
            block_bias = [None] * i
            if i > TOPK_B:
                gate = _dot(kmh[n], q_t) + _dot(kml[n], q_t)
                rows = lax.broadcasted_iota(I32, gate.shape, 0)
                for j in range(i):
                    gj = gate[j:j + 1, :]
                    beats = jnp.where(rows < j, jnp.where(gate >= gj, 1, 0), jnp.where(gate > gj, 1, 0))
                    rank = jnp.sum(jnp.where(rows < i, beats, 0), axis=0, keepdims=True)
                    block_bias[j] = jnp.where(rank < TOPK_B, 0.0, NEG_INF)

            def logits(c):
                lg = _dot(kbb[n, blk(c), :], q_t)
                if c == i:
                    return jnp.where(causal, lg, NEG_INF)
                return lg if block_bias[c] is None else lg + block_bias[c]

            return logits

        _pipelined_heads(i + 1, tq, moba_logits, lambda h, c: vbt[h // G_B, :, blk(c)], lgb, store_head)
        ob_ref[cols, :] = ot[...].T.astype(BF16)


def _attn_prompt(qi_t, qa_t, qb_t, ki_t, ka_t, kb_t, va_t, vb_t, wi_t):
    n_seq, _, seq = qi_t.shape
    tq = MOBA_BLOCK
    fm = lambda w: pl.BlockSpec((None, w, seq), lambda b: (b, 0, 0))
    row = lambda w: pl.BlockSpec((seq, w), lambda b: (b, 0))
    return pl.pallas_call(
        _attn_prompt_kernel,
        grid=(n_seq,),
        in_specs=[fm(W_Q), fm(W_Q), fm(W_Q), fm(HEAD_DIM), fm(HEAD_DIM), fm(2 * HEAD_DIM),
                  fm(HEAD_DIM), fm(2 * HEAD_DIM), fm(N_HEADS)],
        out_specs=(row(W_Q), row(W_Q)),
        out_shape=(jax.ShapeDtypeStruct((n_seq * seq, W_Q), BF16),
                   jax.ShapeDtypeStruct((n_seq * seq, W_Q), BF16)),
        scratch_shapes=[
            pltpu.VMEM((seq, HEAD_DIM), BF16),
            pltpu.VMEM((seq, HEAD_DIM), BF16),
            pltpu.VMEM((HKV_B, seq, HEAD_DIM), BF16),
            pltpu.VMEM((V_ROWS, seq), BF16),
            pltpu.VMEM((HKV_B, V_ROWS, seq), BF16),
            pltpu.VMEM((HKV_B, 16, HEAD_DIM), BF16),
            pltpu.VMEM((HKV_B, 16, HEAD_DIM), BF16),
            pltpu.VMEM((seq, tq), F32),
            pltpu.VMEM((seq, tq), BF16),
            pltpu.VMEM((seq, tq), F32),
            pltpu.VMEM((2, seq, tq), F32),
            pltpu.VMEM((W_Q, tq), F32),
        ],
        compiler_params=pltpu.CompilerParams(
            dimension_semantics=("arbitrary",), vmem_limit_bytes=VMEM_LIMIT),
        name="attn_prompt",
    )(qi_t, qa_t, qb_t, ki_t, ka_t, kb_t, va_t, vb_t, wi_t)


ROWS = N_HEADS * 8
SAMPLE_GROUP = 4


def _attn_sample_kernel(pt_ref, cki_hbm, cka_hbm, cva_hbm, ckb_hbm, cvb_hbm,
                        nki_ref, nka_ref, nva_ref, nkb_ref, nvb_ref,
                        qi_ref, qa_ref, qb_ref, wc_ref,
                        oa_ref, ob_ref,
                        ski, ska, sva, skb, svb, scores, bias, sem,
                        *, group, n_new, past):
    g = pl.program_id(0)
    n_groups = pl.num_programs(0)
    n_pages = past // PAGE
    lpad = ski.shape[3]
    slot = g % 2
    pools = ((cki_hbm, ski), (cka_hbm, ska), (cva_hbm, sva), (ckb_hbm, skb), (cvb_hbm, svb))

    def page_copies(grp, to_slot, t):
        s = t // n_pages
        p = t % n_pages
        page = pt_ref[grp * group + s, p]
        lanes = pl.ds(pl.multiple_of(p * PAGE, PAGE), PAGE)
        return [pltpu.make_async_copy(hbm.at[page], slab.at[to_slot, s, :, lanes], sem.at[to_slot])
                for hbm, slab in pools]

    def start_group(grp, to_slot):
        def body(t, carry):
            for cp in page_copies(grp, to_slot, t):
                cp.start()
            return carry
        lax.fori_loop(0, group * n_pages, body, 0)

    def wait_group(grp, to_slot):
        def body(t, carry):
            for cp in page_copies(grp, to_slot, t):
                cp.wait()
            return carry
        lax.fori_loop(0, group * n_pages, body, 0)

    @pl.when(g == 0)
    def _():
        start_group(0, 0)

    @pl.when(g + 1 < n_groups)
    def _():
        start_group(g + 1, 1 - slot)

    wait_group(g, slot)

    n_blk = past // MOBA_BLOCK
    col = lax.broadcasted_iota(I32, (8, lpad), 1)
    qrow = lax.broadcasted_iota(I32, (8, lpad), 0)
    visible8 = col <= past + jnp.minimum(qrow, n_new - 1)
    for s in range(group):
        for slab, new in ((ski, nki_ref), (ska, nka_ref), (sva, nva_ref), (skb, nkb_ref), (svb, nvb_ref)):
            slab[slot, s, :, past:lpad] = jnp.zeros((slab.shape[2], lpad - past), F32)
            slab[slot, s, :, past:past + n_new] = new[s]
        st = _dot(qi_ref[s], ski[slot, s].astype(BF16))
        st = jnp.maximum(st, 0.0) * wc_ref[s]
        score = st[0:8, :]
        for h in range(1, N_HEADS):
            score = score + st[h * 8:(h + 1) * 8, :]
        scores[s * 8:(s + 1) * 8, :] = jnp.where(visible8, score, NEG_INF)

    rows = group * 8

    def store_bias(v):
        bias[...] = v

    _topk_bias(store_bias, lambda: scores[...], lambda: lax.broadcasted_iota(I32, (rows, lpad), 1),
               1, (rows, 1), TOPK_A, (lpad - 1).bit_length())

    colr = lax.broadcasted_iota(I32, (ROWS, LANES), 1)
    qr = lax.broadcasted_iota(I32, (ROWS, LANES), 0) % 8
    visible_tail = colr <= jnp.minimum(qr, n_new - 1)
    for s in range(group):
        lg = _dot(qa_ref[s], ska[slot, s].astype(BF16)) + jnp.tile(bias[s * 8:(s + 1) * 8, :], (N_HEADS, 1))
        m = jnp.max(lg, axis=1, keepdims=True)
        p = jnp.exp(lg - m)
        l = jnp.sum(p, axis=1, keepdims=True)
        oa_ref[s] = _dot_nt(p.astype(BF16), sva[slot, s].astype(BF16)) / l

        lgb = _dot(qb_ref[s], skb[slot, s].astype(BF16))
        gate = [jnp.sum(lgb[:, j * MOBA_BLOCK:(j + 1) * MOBA_BLOCK], axis=1, keepdims=True)
                for j in range(n_blk)]
        pieces = []
        for j in range(n_blk):
            rank = jnp.zeros((ROWS, 1), I32)
            for j2 in range(n_blk):
                if j2 != j:
                    rank = rank + jnp.where((gate[j2] >= gate[j]) if j2 < j else (gate[j2] > gate[j]), 1, 0)
            pieces.append(lgb[:, j * MOBA_BLOCK:(j + 1) * MOBA_BLOCK] + jnp.where(rank < TOPK_B, 0.0, NEG_INF))
        pieces.append(jnp.where(visible_tail, lgb[:, past:lpad], NEG_INF))
        lgm = jnp.concatenate(pieces, axis=1)
        m = jnp.max(lgm, axis=1, keepdims=True)
        p = jnp.exp(lgm - m)
        l = jnp.sum(p, axis=1, keepdims=True)
        ob_ref[s] = _dot_nt(p.astype(BF16), svb[slot, s].astype(BF16)) / l


def _attn_sample(page_table, cki, cka, cva, ckb, cvb, nki, nka, nva, nkb, nvb, qi, qa, qb2, wcol, group):
    n_seq, n_pages = page_table.shape
    n_new = nki.shape[2]
    past = n_pages * PAGE
    lpad = past + LANES
    hbm = pl.BlockSpec(memory_space=pl.ANY)
    grp = lambda a, b: pl.BlockSpec((group, a, b), lambda g, pt: (g, 0, 0))
    slab = lambda w: pltpu.VMEM((2, group, w, lpad), F32)
    grid_spec = pltpu.PrefetchScalarGridSpec(
        num_scalar_prefetch=1,
        grid=(n_seq // group,),
        in_specs=[hbm, hbm, hbm, hbm, hbm,
                  grp(HEAD_DIM, n_new), grp(HEAD_DIM, n_new), grp(HEAD_DIM, n_new),
                  grp(2 * HEAD_DIM, n_new), grp(2 * HEAD_DIM, n_new),
                  grp(ROWS, HEAD_DIM), grp(ROWS, HEAD_DIM), grp(ROWS, 2 * HEAD_DIM), grp(ROWS, 1)],
        out_specs=(grp(ROWS, HEAD_DIM), grp(ROWS, 2 * HEAD_DIM)),
        scratch_shapes=[
            slab(HEAD_DIM),
            slab(HEAD_DIM),
            slab(HEAD_DIM),
            slab(2 * HEAD_DIM),
            slab(2 * HEAD_DIM),
            pltpu.VMEM((group * 8, lpad), F32),
            pltpu.VMEM((group * 8, lpad), F32),
            pltpu.SemaphoreType.DMA((2,)),
        ],
    )
    return pl.pallas_call(
        functools.partial(_attn_sample_kernel, group=group, n_new=n_new, past=past),
        grid_spec=grid_spec,
        out_shape=(jax.ShapeDtypeStruct((n_seq, ROWS, HEAD_DIM), F32),
                   jax.ShapeDtypeStruct((n_seq, ROWS, 2 * HEAD_DIM), F32)),
        compiler_params=pltpu.CompilerParams(
            dimension_semantics=("arbitrary",), vmem_limit_bytes=VMEM_LIMIT),
        name="attn_sample",
    )(page_table, cki, cka, cva, ckb, cvb, nki, nka, nva, nkb, nvb, qi, qa, qb2, wcol)


def _merge_kernel(x_ref, oa_ref, ob_ref, sga_ref, sgb_ref, wa_ref, wb_ref, wo_ref, o_ref):
    m = sga_ref[...] * _dot(oa_ref[...], wa_ref[...]) + sgb_ref[...] * _dot(ob_ref[...], wb_ref[...])
    o_ref[...] = x_ref[...] + _dot(m.astype(BF16), wo_ref[...])


def _merge(x, oa, ob, sga, sgb, wa, wb, wo, tm):
    t = x.shape[0]
    row = lambda w: pl.BlockSpec((tm, w), lambda i: (i, 0))
    return pl.pallas_call(
        _merge_kernel,
        grid=(t // tm,),
        in_specs=[row(D_MODEL), row(W_Q), row(W_Q), row(D_MODEL), row(D_MODEL),
                  _const_spec((W_Q, D_MODEL)), _const_spec((W_Q, D_MODEL)), _const_spec((D_MODEL, D_MODEL))],
        out_specs=row(D_MODEL),
        out_shape=jax.ShapeDtypeStruct((t, D_MODEL), F32),
        compiler_params=pltpu.CompilerParams(
            dimension_semantics=("arbitrary",), vmem_limit_bytes=VMEM_LIMIT),
        name="merge",
    )(x, oa, ob, sga, sgb, wa, wb, wo)


def _ffn_weights(w_in, w_out):
    pad = D_FF_PAD - D_FF
    wg = jnp.pad(w_in[:, :D_FF], ((0, 0), (0, pad))).astype(BF16)
    wu = jnp.pad(w_in[:, D_FF:], ((0, 0), (0, pad))).astype(BF16)
    wo = jnp.pad(w_out, ((0, pad), (0, 0))).astype(BF16)
    return wg, wu, wo


def _rope_tables(pos):
    half = HEAD_DIM // 2
    inv = ROPE_THETA ** (-jnp.arange(half, dtype=F32) / half)
    ang = pos.astype(F32)[:, None] * inv[None, :]
    cos, sin = jnp.cos(ang), jnp.sin(ang)
    return (jnp.concatenate([cos, cos, cos, cos], axis=1),
            jnp.concatenate([-sin, sin, -sin, sin], axis=1))


def _head_rows(a, n_seq, n_new):
    w = a.shape[1] // N_HEADS
    a = a.reshape(n_seq, n_new, N_HEADS, w).transpose(0, 2, 1, 3)
    a = jnp.pad(a, ((0, 0), (0, 0), (0, 8 - n_new), (0, 0)))
    return a.reshape(n_seq, ROWS, w)


def _from_head_rows(a, n_seq, n_new):
    w = a.shape[2]
    a = a.reshape(n_seq, N_HEADS, 8, w)[:, :, :n_new].transpose(0, 2, 1, 3)
    return a.reshape(n_seq * n_new, N_HEADS * w)


def kernel(x_prompt, x_sample, cache_k_a, cache_v_a, cache_kidx_a, cache_k_b, cache_v_b, page_table,
           ffn1_norm, ffn1_w_in, ffn1_w_out, mix_norm, w_in, q_norm_a, k_norm_a, q_norm_b, k_norm_b,
           gate_bias, w_up_a, w_up_b, w_out, ffn2_norm, ffn2_w_in, ffn2_w_out):
    n_seq, seq, _ = x_prompt.shape
    n_dec, n_new, _ = x_sample.shape
    depth = ffn1_norm.shape[0]
    n_pool = cache_k_a.shape[1]
    past = page_table.shape[1] * PAGE
    assert depth == 1 and seq % MOBA_BLOCK == 0 and past % MOBA_BLOCK == 0 and n_new <= 8
    assert seq >= 4 * TOPK_A and past + n_new >= 4 * TOPK_A
    l = 0
    tp, ts = n_seq * seq, n_dec * n_new

    xp = x_prompt.reshape(tp, D_MODEL)
    xs = x_sample.reshape(ts, D_MODEL)

    f1 = _ffn_weights(ffn1_w_in[l], ffn1_w_out[l])
    f2 = _ffn_weights(ffn2_w_in[l], ffn2_w_out[l])
    w = w_in[l]
    o = np.cumsum([0, W_Q, HEAD_DIM, HEAD_DIM, W_Q, HEAD_DIM, N_HEADS, W_Q, 2 * HEAD_DIM, 2 * HEAD_DIM,
                   D_MODEL, D_MODEL])
    c = lambda k: w[:, o[k]:o[k + 1]]
    z = lambda n: jnp.zeros((D_MODEL, n), F32)
    wp = jnp.concatenate([c(0), c(6), c(7), c(1), z(64), c(3), c(4), z(64), c(8), c(2), c(5), z(56),
                          c(9), c(10)], axis=1).astype(BF16)
    gain = jnp.concatenate([jnp.tile(q_norm_a[l], N_HEADS), jnp.tile(q_norm_b[l], N_HEADS),
                            jnp.tile(k_norm_b[l], HKV_B), k_norm_a[l], jnp.zeros((64,), F32)])[None, :]
    bias = gate_bias[l][None, :]
    lane = np.arange(LANES)
    bd = jnp.asarray((lane[:, None] // HEAD_DIM) == (lane[None, :] // HEAD_DIM), BF16)
    cos_p, sin_p = _rope_tables(jnp.arange(seq, dtype=I32))
    cos_s, sin_s = _rope_tables(past + (jnp.arange(ts, dtype=I32) % n_new))
    wa, wb, wo = w_up_a[l].astype(BF16), w_up_b[l].astype(BF16), w_out[l].astype(BF16)
    g1, gm, g2 = ffn1_norm[l][None, :], mix_norm[l][None, :], ffn2_norm[l][None, :]

    tm = 512
    xp1 = _ffn(xp, g1, *f1, tm)
    (qa_t, qb_t, qi_t, kb_t, ka_t, ki_t, vb_t, va_t, wi_t, sga_p, sgb_p) = _proj(
        xp1, gm, wp, gain, bias, cos_p, sin_p, bd, tm, seq)
    oa_p, ob_p = _attn_prompt(qi_t, qa_t, qb_t, ki_t, ka_t, kb_t, va_t, vb_t, wi_t)
    xp2 = _merge(xp1, oa_p, ob_p, sga_p, sgb_p, wa, wb, wo, tm)
    yp = _ffn(xp2, g2, *f2, tm)

    xs1 = _ffn(xs, g1, *f1, ts)
    (qa_s, qb_s, qi_s, kb_s, ka_s, ki_s, vb_s, va_s, wi_s, sga_s, sgb_s) = _proj(
        xs1, gm, wp, gain, bias, cos_s, sin_s, bd, ts, None)
    qi_r = _head_rows(qi_s, n_dec, n_new)
    qa_r = _head_rows(qa_s, n_dec, n_new)
    qb_r = _head_rows(qb_s, n_dec, n_new)
    zq = jnp.zeros_like(qb_r)
    first = (jnp.arange(ROWS) < ROWS // HKV_B)[None, :, None]
    qb2 = jnp.concatenate([jnp.where(first, qb_r, zq), jnp.where(first, zq, qb_r)], axis=2)
    wcol = _head_rows(wi_s, n_dec, n_new)
    pages_t = lambda a: jnp.swapaxes(a[l].reshape(n_pool, PAGE, -1), 1, 2)
    new_t = lambda a: jnp.swapaxes(a.reshape(n_dec, n_new, a.shape[1]), 1, 2)
    oa_r, ob_r = _attn_sample(
        page_table,
        pages_t(cache_kidx_a), pages_t(cache_k_a), pages_t(cache_v_a), pages_t(cache_k_b), pages_t(cache_v_b),
        new_t(ki_s), new_t(ka_s), new_t(va_s), new_t(kb_s), new_t(vb_s), qi_r, qa_r, qb2, wcol, SAMPLE_GROUP)
    oa_s = _from_head_rows(oa_r, n_dec, n_new).astype(BF16)
    ob_sel = jnp.where(first, ob_r[:, :, :HEAD_DIM], ob_r[:, :, HEAD_DIM:])
    ob_s = _from_head_rows(ob_sel, n_dec, n_new).astype(BF16)
    xs2 = _merge(xs1, oa_s, ob_s, sga_s, sgb_s, wa, wb, wo, ts)
    ys = _ffn(xs2, g2, *f2, ts)

    d = depth
    tok = lambda a: jnp.swapaxes(a, 1, 2)
    return (yp.reshape(n_seq, seq, D_MODEL), ys.reshape(n_dec, n_new, D_MODEL),
            tok(ka_t).reshape(d, n_seq, seq, 1, HEAD_DIM), tok(va_t).reshape(d, n_seq, seq, 1, HEAD_DIM),
            tok(ki_t).reshape(d, n_seq, seq, HEAD_DIM),
            tok(kb_t).reshape(d, n_seq, seq, HKV_B, HEAD_DIM), tok(vb_t).reshape(d, n_seq, seq, HKV_B, HEAD_DIM),
            ka_s.reshape(d, n_dec, n_new, 1, HEAD_DIM), va_s.reshape(d, n_dec, n_new, 1, HEAD_DIM),
            ki_s.reshape(d, n_dec, n_new, HEAD_DIM),
            kb_s.reshape(d, n_dec, n_new, HKV_B, HEAD_DIM), vb_s.reshape(d, n_dec, n_new, HKV_B, HEAD_DIM))
```

```python
import functools

import numpy as np
import jax
import jax.numpy as jnp
from jax import lax
from jax.experimental import pallas as pl
from jax.experimental.pallas import tpu as pltpu

F32 = jnp.float32
BF16 = jnp.bfloat16
I32 = jnp.int32

D_MODEL = 1024
HEAD_DIM = 64
N_HEADS = 8
HKV_B = 2
G_B = N_HEADS // HKV_B
D_FF = 2752
TOPK_A = 256
MOBA_BLOCK = 256
TOPK_B = 3
PAGE = 128
ROPE_THETA = 10000.0
EPS = 1e-6

LANES = 128
D_FF_PAD = 2816
FF_CHUNK = 512
W_Q = N_HEADS * HEAD_DIM

A_W = 1280
B_W = 640
C_W = 256
G_W = 2 * D_MODEL
P_W = A_W + B_W + C_W + G_W

LOG2E = 1.4426950408889634
INT_MIN = np.int32(-2 ** 31)
NEG_INF = float("-inf")
VMEM_LIMIT = 56 * 1024 * 1024


def _dot(a, b):
    return jnp.dot(a, b, preferred_element_type=F32)


def _dot_nt(a, b):
    return lax.dot_general(a, b, (((1,), (1,)), ((), ())), preferred_element_type=F32)


def _split_bf16(x):
    hi = x.astype(BF16)
    lo = (x - hi.astype(F32)).astype(BF16)
    return hi, lo


def _rms(x, g):
    ms = jnp.mean(x * x, axis=-1, keepdims=True)
    return x * lax.rsqrt(ms + EPS) * g


def _const_spec(shape):
    nd = len(shape)
    return pl.BlockSpec(shape, lambda *_: (0,) * nd, pipeline_mode=pl.Buffered(1))


def _half_swiglu(x, g_ref, wg_ref, wu_ref, wo_ref):
    h = _rms(x, g_ref[...]).astype(BF16)
    acc = jnp.zeros_like(x)
    for s in range(0, D_FF_PAD, FF_CHUNK):
        e = min(s + FF_CHUNK, D_FF_PAD)
        g = _dot(h, wg_ref[:, s:e])
        u = _dot(h, wu_ref[:, s:e])
        a = (g * jax.nn.sigmoid(g) * u).astype(BF16)
        acc = acc + _dot(a, wo_ref[s:e, :])
    return x + 0.5 * acc


def _ffn_kernel(x_ref, g_ref, wg_ref, wu_ref, wo_ref, o_ref):
    o_ref[...] = _half_swiglu(x_ref[...], g_ref, wg_ref, wu_ref, wo_ref)


def _ffn(x, g, wg, wu, wo, tm):
    t = x.shape[0]
    return pl.pallas_call(
        _ffn_kernel,
        grid=(t // tm,),
        in_specs=[
            pl.BlockSpec((tm, D_MODEL), lambda i: (i, 0)),
            _const_spec((1, D_MODEL)),
            _const_spec((D_MODEL, D_FF_PAD)),
            _const_spec((D_MODEL, D_FF_PAD)),
            _const_spec((D_FF_PAD, D_MODEL)),
        ],
        out_specs=pl.BlockSpec((tm, D_MODEL), lambda i: (i, 0)),
        out_shape=jax.ShapeDtypeStruct((t, D_MODEL), F32),
        compiler_params=pltpu.CompilerParams(
            dimension_semantics=("arbitrary",), vmem_limit_bytes=VMEM_LIMIT),
        name="ffn",
    )(x, g, wg, wu, wo)


def _rope(y, cos, sin, first_half):
    r_lo = pltpu.roll(y, 32, 1)
    r_hi = pltpu.roll(y, 96, 1)
    return y * cos + jnp.where(first_half, r_hi, r_lo) * sin


def _proj_kernel(x_ref, gm_ref, wp_ref, gain_ref, bias_ref, cos_ref, sin_ref, bd_ref,
                 qa_ref, qb_ref, qi_ref, kb_ref, ka_ref, ki_ref, vb_ref, va_ref, wi_ref,
                 sga_ref, sgb_ref, *, transposed):
    x = x_ref[...]
    tm = x.shape[0]
    h = _rms(x, gm_ref[...]).astype(BF16)
    cos = cos_ref[...]
    sin = sin_ref[...]
    bd = bd_ref[...]
    lane = lax.broadcasted_iota(I32, (tm, LANES), 1)
    first_half = (lane % HEAD_DIM) < (HEAD_DIM // 2)

    z_a = _dot(h, wp_ref[:, 0:A_W])
    tiles_a = []
    for j in range(A_W // LANES):
        z = z_a[:, j * LANES:(j + 1) * LANES]
        hi, lo = _split_bf16(z * z)
        ms = (_dot(hi, bd) + _dot(lo, bd)) * (1.0 / HEAD_DIM)
        y = z * lax.rsqrt(ms + EPS) * gain_ref[:, j * LANES:(j + 1) * LANES]
        tiles_a.append(_rope(y, cos, sin, first_half))

    z_b = _dot(h, wp_ref[:, A_W:A_W + B_W])
    tiles_b = [_rope(z_b[:, j * LANES:(j + 1) * LANES], cos, sin, first_half)
               for j in range(B_W // LANES)]

    z_c = _dot(h, wp_ref[:, A_W + B_W:A_W + B_W + C_W])
    z_g = _dot(h, wp_ref[:, A_W + B_W + C_W:P_W]) + bias_ref[...]

    scale = HEAD_DIM ** -0.5
    qa = jnp.concatenate(tiles_a[0:4], axis=1) * (scale * LOG2E)
    qb = jnp.concatenate(tiles_a[4:8], axis=1) * (scale * LOG2E)
    qi = jnp.concatenate(tiles_b[0:4], axis=1) * scale
    wi_scale = N_HEADS ** -0.5
    if transposed:
        qa_ref[...] = qa.T.astype(BF16)
        qb_ref[...] = qb.T.astype(BF16)
        qi_ref[...] = qi.T.astype(BF16)
        kb_ref[...] = tiles_a[8].T
        ka_ref[...] = tiles_a[9].T[0:HEAD_DIM, :]
        ki_ref[...] = tiles_b[4].T[0:HEAD_DIM, :]
        zc_t = z_c.T
        vb_ref[...] = zc_t[0:LANES, :]
        va_ref[...] = zc_t[LANES:LANES + HEAD_DIM, :]
        wi_ref[...] = zc_t[LANES + HEAD_DIM:LANES + HEAD_DIM + N_HEADS, :] * wi_scale
    else:
        qa_ref[...] = qa.astype(BF16)
        qb_ref[...] = qb.astype(BF16)
        qi_ref[...] = qi.astype(BF16)
        kb_ref[...] = tiles_a[8]
        ka_ref[...] = tiles_a[9][:, 0:HEAD_DIM]
        ki_ref[...] = tiles_b[4][:, 0:HEAD_DIM]
        vb_ref[...] = z_c[:, 0:LANES]
        va_ref[...] = z_c[:, LANES:LANES + HEAD_DIM]
        wi_ref[...] = z_c[:, LANES + HEAD_DIM:LANES + HEAD_DIM + N_HEADS] * wi_scale
    sg = jax.nn.sigmoid(z_g)
    sga_ref[...] = sg[:, 0:D_MODEL]
    sgb_ref[...] = sg[:, D_MODEL:G_W]


def _proj(x, gm, wp, gain, bias, cos_t, sin_t, bd, tm, seq):
    t = x.shape[0]
    n_tab = cos_t.shape[0] // tm
    row = lambda w: pl.BlockSpec((tm, w), lambda i: (i, 0))
    widths = (W_Q, W_Q, W_Q, 2 * HEAD_DIM, HEAD_DIM, HEAD_DIM, 2 * HEAD_DIM, HEAD_DIM, N_HEADS)
    dtypes = (BF16, BF16, BF16, F32, F32, F32, F32, F32, F32)
    if seq is None:
        specs = [row(w) for w in widths]
        shapes = [jax.ShapeDtypeStruct((t, w), d) for w, d in zip(widths, dtypes)]
    else:
        nb = seq // tm
        specs = [pl.BlockSpec((None, w, tm), lambda i: (i // nb, 0, i % nb)) for w in widths]
        shapes = [jax.ShapeDtypeStruct((t // seq, w, seq), d) for w, d in zip(widths, dtypes)]
    out_specs = tuple(specs) + (row(D_MODEL), row(D_MODEL))
    out_shape = tuple(shapes) + (jax.ShapeDtypeStruct((t, D_MODEL), F32), jax.ShapeDtypeStruct((t, D_MODEL), F32))
    transposed = seq is not None
    return pl.pallas_call(
        functools.partial(_proj_kernel, transposed=transposed),
        grid=(t // tm,),
        in_specs=[
            row(D_MODEL),
            _const_spec((1, D_MODEL)),
            _const_spec((D_MODEL, P_W)),
            _const_spec((1, A_W)),
            _const_spec((1, G_W)),
            pl.BlockSpec((tm, LANES), lambda i: (i % n_tab, 0)),
            pl.BlockSpec((tm, LANES), lambda i: (i % n_tab, 0)),
            _const_spec((LANES, LANES)),
        ],
        out_specs=out_specs,
        out_shape=out_shape,
        compiler_params=pltpu.CompilerParams(
            dimension_semantics=("arbitrary",), vmem_limit_bytes=VMEM_LIMIT),
        name="proj",
    )(x, gm, wp, gain, bias, cos_t, sin_t, bd)


def _count(mask, axis):
    return jnp.sum(jnp.where(mask, 1.0, 0.0), axis=axis, keepdims=True)


def _key_to_score(key):
    return lax.bitcast_convert_type(key ^ ((key >> 31) & np.int32(0x7FFFFFFF)), F32)


def _kth_largest(load_scores, axis, vec_shape, k_sel):
    one = np.int32(1)

    def value_bit(it, ans):
        cand = ans | lax.shift_left(one, 31 - it)
        thr = _key_to_score(cand ^ INT_MIN)
        return jnp.where(_count(load_scores() >= thr, axis) >= k_sel, cand, ans)

    return _key_to_score(lax.fori_loop(0, 32, value_bit, jnp.zeros(vec_shape, I32)) ^ INT_MIN)


def _score_to_key(score):
    b = lax.bitcast_convert_type(score, I32)
    return b ^ ((b >> 31) & np.int32(0x7FFFFFFF))


def _count_rows16(mask):
    rows, q = mask.shape
    assert rows % 16 == 0 and rows // 16 <= 256
    ones = jnp.where(mask, jnp.ones((), BF16), jnp.zeros((), BF16))
    part = ones[0:16, :]
    for r in range(16, rows, 16):
        part = part + ones[r:r + 16, :]
    return jnp.sum(part.astype(F32), axis=0, keepdims=True)


def _kth_largest_rows(load_scores, coarse_ref, q_cols, k_sel):
    n = load_scores().shape[0]
    coarse_ref[0:n, :] = load_scores().astype(BF16)
    one = np.int32(1)

    def bf16_of(pattern):
        k = pattern - np.int32(0x8000)
        bits = k ^ ((k >> 15) & np.int32(0x7FFF))
        return lax.bitcast_convert_type(lax.shift_left(bits, 16), F32).astype(BF16)

    def coarse_bit(it, ans):
        cand = ans | lax.shift_left(one, 15 - it)
        return jnp.where(_count_rows16(coarse_ref[0:n, :] >= bf16_of(cand)) >= k_sel, cand, ans)

    coarse = lax.fori_loop(0, 16, coarse_bit, jnp.zeros((1, q_cols), I32))
    base = _score_to_key(bf16_of(coarse).astype(F32)) - np.int32(2 ** 15)

    def fine_bit(it, off):
        cand = off | lax.shift_left(one, 16 - it)
        return jnp.where(_count(load_scores() >= _key_to_score(base + cand), 0) >= k_sel, cand, off)

    return _key_to_score(base + lax.fori_loop(0, 17, fine_bit, jnp.zeros((1, q_cols), I32)))


def _topk_bias(store, load_keys, key_index, axis, vec_shape, k_sel, n_index_bits, coarse_ref=None):
    one = np.int32(1)
    if coarse_ref is None:
        thr = _kth_largest(load_keys, axis, vec_shape, k_sel)
    else:
        thr = _kth_largest_rows(load_keys, coarse_ref, vec_shape[1], k_sel)
    has_tie = jnp.max(_count(load_keys() >= thr, axis)) > k_sel

    @pl.when(jnp.logical_not(has_tie))
    def _():
        store(jnp.where(load_keys() >= thr, 0.0, NEG_INF))

    @pl.when(has_tie)
    def _():
        need = k_sel - _count(load_keys() > thr, axis)

        def index_bit(it, cut):
            cand = cut | lax.shift_left(one, n_index_bits - 1 - it)
            hit = jnp.where(load_keys() == thr, jnp.where(key_index() < cand, 1.0, 0.0), 0.0)
            return jnp.where(jnp.sum(hit, axis=axis, keepdims=True) < need, cand, cut)

        cut = lax.fori_loop(0, n_index_bits, index_bit, jnp.zeros(vec_shape, I32))
        keys = load_keys()
        keep_eq = jnp.where(key_index() <= cut, 0.0, NEG_INF)
        store(jnp.where(keys > thr, 0.0, jnp.where(keys == thr, keep_eq, NEG_INF)))


V_ROWS = HEAD_DIM + 16


def _pipelined_heads(n_chunks, chunk, head_logits, values_t, lgb, store_out):
    rows = lambda c: slice(c * chunk, (c + 1) * chunk)

    def logits_stage(h, slot):
        logits = head_logits(h)
        m = None
        for c in range(n_chunks):
            lg = logits(c)
            lgb[slot, rows(c), :] = lg
            cm = jnp.max(lg, axis=0, keepdims=True)
            m = cm if m is None else jnp.maximum(m, cm)
        return m

    def step(h, m_cur, cur, nxt):
        logits = head_logits(jnp.minimum(h + 1, N_HEADS - 1))
        acc = None
        m_next = None
        for c in range(n_chunks):
            lg = logits(c)
            nxt[rows(c), :] = lg
            cm = jnp.max(lg, axis=0, keepdims=True)
            m_next = cm if m_next is None else jnp.maximum(m_next, cm)
            p = jnp.exp2(cur[rows(c), :] - m_cur).astype(BF16)
            pv = _dot(values_t(h, c), p)
            acc = pv if acc is None else acc + pv
        store_out(h, acc[0:HEAD_DIM, :] / acc[HEAD_DIM:HEAD_DIM + 1, :])
        return m_next

    def body(t, m_cur):
        m_mid = step(2 * t, m_cur, lgb.at[0], lgb.at[1])
        return step(2 * t + 1, m_mid, lgb.at[1], lgb.at[0])

    lax.fori_loop(0, N_HEADS // 2, body, logits_stage(0, 0))


def _attn_prompt_kernel(qi_ref, qa_ref, qb_ref, ki_ref, ka_ref, kb_ref, va_ref, vb_ref, wit,
                        oa_ref, ob_ref,
                        kib, kab, kbb, vat, vbt, kmh, kml, sc, sc16, bias, lgb, ot):
    seq = ki_ref.shape[1]
    tq = MOBA_BLOCK
    n_blk = seq // tq

    kk = jnp.concatenate([ka_ref[...], ki_ref[...]], axis=0).T
    kab[...] = kk[:, 0:HEAD_DIM].astype(BF16)
    kib[...] = kk[:, HEAD_DIM:2 * HEAD_DIM].astype(BF16)
    kb = kb_ref[...].T
    ones = jnp.ones((V_ROWS - HEAD_DIM, seq), BF16)
    vat[...] = jnp.concatenate([va_ref[...].astype(BF16), ones], axis=0)
    means = jnp.concatenate(
        [jnp.mean(kb[j * tq:(j + 1) * tq, :], axis=0, keepdims=True) for j in range(n_blk)]
        + [jnp.zeros((kmh.shape[1] - n_blk, 2 * HEAD_DIM), F32)], axis=0)
    for n in range(HKV_B):
        kbb[n] = kb[:, n * HEAD_DIM:(n + 1) * HEAD_DIM].astype(BF16)
        vbt[n] = jnp.concatenate([vb_ref[n * HEAD_DIM:(n + 1) * HEAD_DIM, :].astype(BF16), ones], axis=0)
        hi, lo = _split_bf16(means[:, n * HEAD_DIM:(n + 1) * HEAD_DIM])
        kmh[n] = hi
        kml[n] = lo

    r_loc = lax.broadcasted_iota(I32, (tq, tq), 0)
    c_loc = lax.broadcasted_iota(I32, (tq, tq), 1)
    causal = r_loc <= c_loc

    for i in range(n_blk):
        c0 = i * tq
        lk = c0 + tq
        cols = slice(c0, c0 + tq)

        sc[0:lk, :] = jnp.zeros((lk, tq), F32)

        def idx_head(h, carry):
            off = pl.multiple_of(h * HEAD_DIM, HEAD_DIM)
            s = _dot(kib[0:lk, :], qi_ref[pl.ds(off, HEAD_DIM), cols])
            sc[0:lk, :] += jnp.maximum(s, 0.0) * wit[pl.ds(h, 1), cols]
            return carry

        lax.fori_loop(0, N_HEADS, idx_head, 0)

        if i == 0:
            bias[0:tq, :] = jnp.where(causal, 0.0, NEG_INF)
        else:
            sc[c0:lk, :] = jnp.where(causal, sc[c0:lk, :], NEG_INF)

            def store_bias(v):
                bias[0:lk, :] = v

            _topk_bias(store_bias, lambda: sc[0:lk, :], lambda: lax.broadcasted_iota(I32, (lk, tq), 0),
                       0, (1, tq), TOPK_A, (lk - 1).bit_length(), coarse_ref=sc16)

        blk = lambda c: slice(c * tq, (c + 1) * tq)

        def store_head(h, o):
            ot[pl.ds(pl.multiple_of(h * HEAD_DIM, HEAD_DIM), HEAD_DIM), :] = o

        def dsa_logits(h):
            q_t = qa_ref[pl.ds(pl.multiple_of(h * HEAD_DIM, HEAD_DIM), HEAD_DIM), cols]
            return lambda c: _dot(kab[blk(c), :], q_t) + bias[blk(c), :]

        _pipelined_heads(i + 1, tq, dsa_logits, lambda h, c: vat[:, blk(c)], lgb, store_head)
        oa_ref[cols, :] = ot[...].T.astype(BF16)

        def moba_logits(h):
            n = h // G_B
            q_t = qb_ref[pl.ds(pl.multiple_of(h * HEAD_DIM, HEAD_DIM), HEAD_DIM), cols]
            block_bias = [None] * i
            if i > TOPK_B:
                gate = _dot(kmh[n], q_t) + _dot(kml[n], q_t)
                rows = lax.broadcasted_iota(I32, gate.shape, 0)
                for j in range(i):
                    gj = gate[j:j + 1, :]
                    beats = jnp.where(rows < j, jnp.where(gate >= gj, 1, 0), jnp.where(gate > gj, 1, 0))
                    rank = jnp.sum(jnp.where(rows < i, beats, 0), axis=0, keepdims=True)
                    block_bias[j] = jnp.where(rank < TOPK_B, 0.0, NEG_INF)

            def logits(c):
                lg = _dot(kbb[n, blk(c), :], q_t)
                if c == i:
                    return jnp.where(causal, lg, NEG_INF)
                return lg if block_bias[c] is None else lg + block_bias[c]

            return logits

        _pipelined_heads(i + 1, tq, moba_logits, lambda h, c: vbt[h // G_B, :, blk(c)], lgb, store_head)
        ob_ref[cols, :] = ot[...].T.astype(BF16)


def _attn_prompt(qi_t, qa_t, qb_t, ki_t, ka_t, kb_t, va_t, vb_t, wi_t):
    n_seq, _, seq = qi_t.shape
    tq = MOBA_BLOCK
    fm = lambda w: pl.BlockSpec((None, w, seq), lambda b: (b, 0, 0))
    row = lambda w: pl.BlockSpec((seq, w), lambda b: (b, 0))
    return pl.pallas_call(
        _attn_prompt_kernel,
        grid=(n_seq,),
        in_specs=[fm(W_Q), fm(W_Q), fm(W_Q), fm(HEAD_DIM), fm(HEAD_DIM), fm(2 * HEAD_DIM),
                  fm(HEAD_DIM), fm(2 * HEAD_DIM), fm(N_HEADS)],
        out_specs=(row(W_Q), row(W_Q)),
        out_shape=(jax.ShapeDtypeStruct((n_seq * seq, W_Q), BF16),
                   jax.ShapeDtypeStruct((n_seq * seq, W_Q), BF16)),
        scratch_shapes=[
            pltpu.VMEM((seq, HEAD_DIM), BF16),
            pltpu.VMEM((seq, HEAD_DIM), BF16),
            pltpu.VMEM((HKV_B, seq, HEAD_DIM), BF16),
            pltpu.VMEM((V_ROWS, seq), BF16),
            pltpu.VMEM((HKV_B, V_ROWS, seq), BF16),
            pltpu.VMEM((HKV_B, 16, HEAD_DIM), BF16),
            pltpu.VMEM((HKV_B, 16, HEAD_DIM), BF16),
            pltpu.VMEM((seq, tq), F32),
            pltpu.VMEM((seq, tq), BF16),
            pltpu.VMEM((seq, tq), F32),
            pltpu.VMEM((2, seq, tq), F32),
            pltpu.VMEM((W_Q, tq), F32),
        ],
        compiler_params=pltpu.CompilerParams(
            dimension_semantics=("arbitrary",), vmem_limit_bytes=VMEM_LIMIT),
        name="attn_prompt",
    )(qi_t, qa_t, qb_t, ki_t, ka_t, kb_t, va_t, vb_t, wi_t)


ROWS = N_HEADS * 8
SAMPLE_GROUP = 4


def _attn_sample_kernel(pt_ref, cki_hbm, cka_hbm, cva_hbm, ckb_hbm, cvb_hbm,
                        nki_ref, nka_ref, nva_ref, nkb_ref, nvb_ref,
                        qi_ref, qa_ref, qb_ref, wc_ref,
                        oa_ref, ob_ref,
                        ski, ska, sva, skb, svb, scores, bias, sem,
                        *, group, n_new, past):
    g = pl.program_id(0)
    n_groups = pl.num_programs(0)
    n_pages = past // PAGE
    lpad = ski.shape[3]
    slot = g % 2
    pools = ((cki_hbm, ski), (cka_hbm, ska), (cva_hbm, sva), (ckb_hbm, skb), (cvb_hbm, svb))

    def page_copies(grp, to_slot, t):
        s = t // n_pages
        p = t % n_pages
        page = pt_ref[grp * group + s, p]
        lanes = pl.ds(pl.multiple_of(p * PAGE, PAGE), PAGE)
        return [pltpu.make_async_copy(hbm.at[page], slab.at[to_slot, s, :, lanes], sem.at[to_slot])
                for hbm, slab in pools]

    def start_group(grp, to_slot):
        def body(t, carry):
            for cp in page_copies(grp, to_slot, t):
                cp.start()
            return carry
        lax.fori_loop(0, group * n_pages, body, 0)

    def wait_group(to_slot):
        for _, slab in pools:
            filled = slab.at[to_slot, :, :, 0:past]
            pltpu.make_async_copy(filled, filled, sem.at[to_slot]).wait()

    @pl.when(g == 0)
    def _():
        start_group(0, 0)

    @pl.when(g + 1 < n_groups)
    def _():
        start_group(g + 1, 1 - slot)

    wait_group(slot)

    n_blk = past // MOBA_BLOCK
    col = lax.broadcasted_iota(I32, (8, lpad), 1)
    qrow = lax.broadcasted_iota(I32, (8, lpad), 0)
    visible8 = col <= past + jnp.minimum(qrow, n_new - 1)
    for s in range(group):
        for slab, new in ((ski, nki_ref), (ska, nka_ref), (sva, nva_ref), (skb, nkb_ref), (svb, nvb_ref)):
            slab[slot, s, :, past:lpad] = jnp.zeros((slab.shape[2], lpad - past), F32)
            slab[slot, s, :, past:past + n_new] = new[s]
        st = _dot(qi_ref[s], ski[slot, s].astype(BF16))
        st = jnp.maximum(st, 0.0) * wc_ref[s]
        score = st[0:8, :]
        for h in range(1, N_HEADS):
            score = score + st[h * 8:(h + 1) * 8, :]
        scores[s * 8:(s + 1) * 8, :] = jnp.where(visible8, score, NEG_INF)

    rows = group * 8

    def store_bias(v):
        bias[...] = v

    _topk_bias(store_bias, lambda: scores[...], lambda: lax.broadcasted_iota(I32, (rows, lpad), 1),
               1, (rows, 1), TOPK_A, (lpad - 1).bit_length())

    colr = lax.broadcasted_iota(I32, (ROWS, LANES), 1)
    qr = lax.broadcasted_iota(I32, (ROWS, LANES), 0) % 8
    visible_tail = colr <= jnp.minimum(qr, n_new - 1)
    for s in range(group):
        lg = _dot(qa_ref[s], ska[slot, s].astype(BF16)) + jnp.tile(bias[s * 8:(s + 1) * 8, :], (N_HEADS, 1))
        m = jnp.max(lg, axis=1, keepdims=True)
        p = jnp.exp2(lg - m)
        l = jnp.sum(p, axis=1, keepdims=True)
        oa_ref[s] = _dot_nt(p.astype(BF16), sva[slot, s].astype(BF16)) / l

        lgb = _dot(qb_ref[s], skb[slot, s].astype(BF16))
        gate = [jnp.sum(lgb[:, j * MOBA_BLOCK:(j + 1) * MOBA_BLOCK], axis=1, keepdims=True)
                for j in range(n_blk)]
        pieces = []
        for j in range(n_blk):
            rank = jnp.zeros((ROWS, 1), I32)
            for j2 in range(n_blk):
                if j2 != j:
                    rank = rank + jnp.where((gate[j2] >= gate[j]) if j2 < j else (gate[j2] > gate[j]), 1, 0)
            pieces.append(lgb[:, j * MOBA_BLOCK:(j + 1) * MOBA_BLOCK] + jnp.where(rank < TOPK_B, 0.0, NEG_INF))
        pieces.append(jnp.where(visible_tail, lgb[:, past:lpad], NEG_INF))
        lgm = jnp.concatenate(pieces, axis=1)
        m = jnp.max(lgm, axis=1, keepdims=True)
        p = jnp.exp2(lgm - m)
        l = jnp.sum(p, axis=1, keepdims=True)
        ob_ref[s] = _dot_nt(p.astype(BF16), svb[slot, s].astype(BF16)) / l


def _attn_sample(page_table, cki, cka, cva, ckb, cvb, nki, nka, nva, nkb, nvb, qi, qa, qb2, wcol, group):
    n_seq, n_pages = page_table.shape
    n_new = nki.shape[2]
    past = n_pages * PAGE
    lpad = past + LANES
    hbm = pl.BlockSpec(memory_space=pl.ANY)
    grp = lambda a, b: pl.BlockSpec((group, a, b), lambda g, pt: (g, 0, 0))
    slab = lambda w: pltpu.VMEM((2, group, w, lpad), F32)
    grid_spec = pltpu.PrefetchScalarGridSpec(
        num_scalar_prefetch=1,
        grid=(n_seq // group,),
        in_specs=[hbm, hbm, hbm, hbm, hbm,
                  grp(HEAD_DIM, n_new), grp(HEAD_DIM, n_new), grp(HEAD_DIM, n_new),
                  grp(2 * HEAD_DIM, n_new), grp(2 * HEAD_DIM, n_new),
                  grp(ROWS, HEAD_DIM), grp(ROWS, HEAD_DIM), grp(ROWS, 2 * HEAD_DIM), grp(ROWS, 1)],
        out_specs=(grp(ROWS, HEAD_DIM), grp(ROWS, 2 * HEAD_DIM)),
        scratch_shapes=[
            slab(HEAD_DIM),
            slab(HEAD_DIM),
            slab(HEAD_DIM),
            slab(2 * HEAD_DIM),
            slab(2 * HEAD_DIM),
            pltpu.VMEM((group * 8, lpad), F32),
            pltpu.VMEM((group * 8, lpad), F32),
            pltpu.SemaphoreType.DMA((2,)),
        ],
    )
    return pl.pallas_call(
        functools.partial(_attn_sample_kernel, group=group, n_new=n_new, past=past),
        grid_spec=grid_spec,
        out_shape=(jax.ShapeDtypeStruct((n_seq, ROWS, HEAD_DIM), F32),
                   jax.ShapeDtypeStruct((n_seq, ROWS, 2 * HEAD_DIM), F32)),
        compiler_params=pltpu.CompilerParams(
            dimension_semantics=("arbitrary",), vmem_limit_bytes=VMEM_LIMIT),
        name="attn_sample",
    )(page_table, cki, cka, cva, ckb, cvb, nki, nka, nva, nkb, nvb, qi, qa, qb2, wcol)


def _merge_ffn_kernel(x_ref, oa_ref, ob_ref, sga_ref, sgb_ref, wa_ref, wb_ref, wo_ref,
                      g_ref, wg_ref, wu_ref, wf_ref, o_ref):
    m = sga_ref[...] * _dot(oa_ref[...], wa_ref[...]) + sgb_ref[...] * _dot(ob_ref[...], wb_ref[...])
    x = x_ref[...] + _dot(m.astype(BF16), wo_ref[...])
    o_ref[...] = _half_swiglu(x, g_ref, wg_ref, wu_ref, wf_ref)


def _merge_ffn(x, oa, ob, sga, sgb, wa, wb, wo, g, wg, wu, wf, tm):
    t = x.shape[0]
    row = lambda w: pl.BlockSpec((tm, w), lambda i: (i, 0))
    return pl.pallas_call(
        _merge_ffn_kernel,
        grid=(t // tm,),
        in_specs=[row(D_MODEL), row(W_Q), row(W_Q), row(D_MODEL), row(D_MODEL),
                  _const_spec((W_Q, D_MODEL)), _const_spec((W_Q, D_MODEL)), _const_spec((D_MODEL, D_MODEL)),
                  _const_spec((1, D_MODEL)), _const_spec((D_MODEL, D_FF_PAD)), _const_spec((D_MODEL, D_FF_PAD)),
                  _const_spec((D_FF_PAD, D_MODEL))],
        out_specs=row(D_MODEL),
        out_shape=jax.ShapeDtypeStruct((t, D_MODEL), F32),
        compiler_params=pltpu.CompilerParams(
            dimension_semantics=("arbitrary",), vmem_limit_bytes=VMEM_LIMIT),
        name="merge_ffn",
    )(x, oa, ob, sga, sgb, wa, wb, wo, g, wg, wu, wf)


def _ffn_weights(w_in, w_out):
    pad = D_FF_PAD - D_FF
    wg = jnp.pad(w_in[:, :D_FF], ((0, 0), (0, pad))).astype(BF16)
    wu = jnp.pad(w_in[:, D_FF:], ((0, 0), (0, pad))).astype(BF16)
    wo = jnp.pad(w_out, ((0, pad), (0, 0))).astype(BF16)
    return wg, wu, wo


def _rope_tables(pos):
    half = HEAD_DIM // 2
    inv = ROPE_THETA ** (-jnp.arange(half, dtype=F32) / half)
    ang = pos.astype(F32)[:, None] * inv[None, :]
    cos, sin = jnp.cos(ang), jnp.sin(ang)
    return (jnp.concatenate([cos, cos, cos, cos], axis=1),
            jnp.concatenate([-sin, sin, -sin, sin], axis=1))


def _head_rows(a, n_seq, n_new):
    w = a.shape[1] // N_HEADS
    a = a.reshape(n_seq, n_new, N_HEADS, w).transpose(0, 2, 1, 3)
    a = jnp.pad(a, ((0, 0), (0, 0), (0, 8 - n_new), (0, 0)))
    return a.reshape(n_seq, ROWS, w)


def _from_head_rows(a, n_seq, n_new):
    w = a.shape[2]
    a = a.reshape(n_seq, N_HEADS, 8, w)[:, :, :n_new].transpose(0, 2, 1, 3)
    return a.reshape(n_seq * n_new, N_HEADS * w)


def kernel(x_prompt, x_sample, cache_k_a, cache_v_a, cache_kidx_a, cache_k_b, cache_v_b, page_table,
           ffn1_norm, ffn1_w_in, ffn1_w_out, mix_norm, w_in, q_norm_a, k_norm_a, q_norm_b, k_norm_b,
           gate_bias, w_up_a, w_up_b, w_out, ffn2_norm, ffn2_w_in, ffn2_w_out):
    n_seq, seq, _ = x_prompt.shape
    n_dec, n_new, _ = x_sample.shape
    depth = ffn1_norm.shape[0]
    n_pool = cache_k_a.shape[1]
    past = page_table.shape[1] * PAGE
    assert depth == 1 and seq % MOBA_BLOCK == 0 and past % MOBA_BLOCK == 0 and n_new <= 8
    assert seq >= 4 * TOPK_A and past + n_new >= 4 * TOPK_A
    l = 0
    tp, ts = n_seq * seq, n_dec * n_new

    xp = x_prompt.reshape(tp, D_MODEL)
    xs = x_sample.reshape(ts, D_MODEL)

    f1 = _ffn_weights(ffn1_w_in[l], ffn1_w_out[l])
    f2 = _ffn_weights(ffn2_w_in[l], ffn2_w_out[l])
    w = w_in[l]
    o = np.cumsum([0, W_Q, HEAD_DIM, HEAD_DIM, W_Q, HEAD_DIM, N_HEADS, W_Q, 2 * HEAD_DIM, 2 * HEAD_DIM,
                   D_MODEL, D_MODEL])
    c = lambda k: w[:, o[k]:o[k + 1]]
    z = lambda n: jnp.zeros((D_MODEL, n), F32)
    wp = jnp.concatenate([c(0), c(6), c(7), c(1), z(64), c(3), c(4), z(64), c(8), c(2), c(5), z(56),
                          c(9), c(10)], axis=1).astype(BF16)
    gain = jnp.concatenate([jnp.tile(q_norm_a[l], N_HEADS), jnp.tile(q_norm_b[l], N_HEADS),
                            jnp.tile(k_norm_b[l], HKV_B), k_norm_a[l], jnp.zeros((64,), F32)])[None, :]
    bias = gate_bias[l][None, :]
    lane = np.arange(LANES)
    bd = jnp.asarray((lane[:, None] // HEAD_DIM) == (lane[None, :] // HEAD_DIM), BF16)
    cos_p, sin_p = _rope_tables(jnp.arange(seq, dtype=I32))
    cos_s, sin_s = _rope_tables(past + (jnp.arange(ts, dtype=I32) % n_new))
    wa, wb, wo = w_up_a[l].astype(BF16), w_up_b[l].astype(BF16), w_out[l].astype(BF16)
    g1, gm, g2 = ffn1_norm[l][None, :], mix_norm[l][None, :], ffn2_norm[l][None, :]

    tm = 512
    xp1 = _ffn(xp, g1, *f1, tm)
    (qa_t, qb_t, qi_t, kb_t, ka_t, ki_t, vb_t, va_t, wi_t, sga_p, sgb_p) = _proj(
        xp1, gm, wp, gain, bias, cos_p, sin_p, bd, tm, seq)
    oa_p, ob_p = _attn_prompt(qi_t, qa_t, qb_t, ki_t, ka_t, kb_t, va_t, vb_t, wi_t)
    yp = _merge_ffn(xp1, oa_p, ob_p, sga_p, sgb_p, wa, wb, wo, g2, *f2, tm)

    xs1 = _ffn(xs, g1, *f1, ts)
    (qa_s, qb_s, qi_s, kb_s, ka_s, ki_s, vb_s, va_s, wi_s, sga_s, sgb_s) = _proj(
        xs1, gm, wp, gain, bias, cos_s, sin_s, bd, ts, None)
    qi_r = _head_rows(qi_s, n_dec, n_new)
    qa_r = _head_rows(qa_s, n_dec, n_new)
    qb_r = _head_rows(qb_s, n_dec, n_new)
    zq = jnp.zeros_like(qb_r)
    first = (jnp.arange(ROWS) < ROWS // HKV_B)[None, :, None]
    qb2 = jnp.concatenate([jnp.where(first, qb_r, zq), jnp.where(first, zq, qb_r)], axis=2)
    wcol = _head_rows(wi_s, n_dec, n_new)
    pages_t = lambda a: jnp.swapaxes(a[l].reshape(n_pool, PAGE, -1), 1, 2)
    new_t = lambda a: jnp.swapaxes(a.reshape(n_dec, n_new, a.shape[1]), 1, 2)
    oa_r, ob_r = _attn_sample(
        page_table,
        pages_t(cache_kidx_a), pages_t(cache_k_a), pages_t(cache_v_a), pages_t(cache_k_b), pages_t(cache_v_b),
        new_t(ki_s), new_t(ka_s), new_t(va_s), new_t(kb_s), new_t(vb_s), qi_r, qa_r, qb2, wcol, SAMPLE_GROUP)
    oa_s = _from_head_rows(oa_r, n_dec, n_new).astype(BF16)
    ob_sel = jnp.where(first, ob_r[:, :, :HEAD_DIM], ob_r[:, :, HEAD_DIM:])
    ob_s = _from_head_rows(ob_sel, n_dec, n_new).astype(BF16)
    ys = _merge_ffn(xs1, oa_s, ob_s, sga_s, sgb_s, wa, wb, wo, g2, *f2, ts)

    d = depth
    tok = lambda a: jnp.swapaxes(a, 1, 2)
    return (yp.reshape(n_seq, seq, D_MODEL), ys.reshape(n_dec, n_new, D_MODEL),
            tok(ka_t).reshape(d, n_seq, seq, 1, HEAD_DIM), tok(va_t).reshape(d, n_seq, seq, 1, HEAD_DIM),
            tok(ki_t).reshape(d, n_seq, seq, HEAD_DIM),
            tok(kb_t).reshape(d, n_seq, seq, HKV_B, HEAD_DIM), tok(vb_t).reshape(d, n_seq, seq, HKV_B, HEAD_DIM),
            ka_s.reshape(d, n_dec, n_new, 1, HEAD_DIM), va_s.reshape(d, n_dec, n_new, 1, HEAD_DIM),
            ki_s.reshape(d, n_dec, n_new, HEAD_DIM),
            kb_s.reshape(d, n_dec, n_new, HKV_B, HEAD_DIM), vb_s.reshape(d, n_dec, n_new, HKV_B, HEAD_DIM))
```

```python
import functools

import numpy as np
import jax
import jax.numpy as jnp
from jax import lax
from jax.experimental import pallas as pl
from jax.experimental.pallas import tpu as pltpu

F32 = jnp.float32
BF16 = jnp.bfloat16
I32 = jnp.int32

D_MODEL = 1024
HEAD_DIM = 64
N_HEADS = 8
HKV_B = 2
G_B = N_HEADS // HKV_B
D_FF = 2752
TOPK_A = 256
MOBA_BLOCK = 256
TOPK_B = 3
PAGE = 128
ROPE_THETA = 10000.0
EPS = 1e-6

LANES = 128
D_FF_PAD = 2816
FF_CHUNK = 512
W_Q = N_HEADS * HEAD_DIM

A_W = 1280
B_W = 640
C_W = 256
G_W = 2 * D_MODEL
P_W = A_W + B_W + C_W + G_W

LOG2E = 1.4426950408889634
INT_MIN = np.int32(-2 ** 31)
NEG_INF = float("-inf")
VMEM_LIMIT = 56 * 1024 * 1024


def _dot(a, b):
    return jnp.dot(a, b, preferred_element_type=F32)


def _dot_nt(a, b):
    return lax.dot_general(a, b, (((1,), (1,)), ((), ())), preferred_element_type=F32)


def _split_bf16(x):
    hi = x.astype(BF16)
    lo = (x - hi.astype(F32)).astype(BF16)
    return hi, lo


def _rms(x, g):
    ms = jnp.mean(x * x, axis=-1, keepdims=True)
    return x * lax.rsqrt(ms + EPS) * g


def _const_spec(shape):
    nd = len(shape)
    return pl.BlockSpec(shape, lambda *_: (0,) * nd, pipeline_mode=pl.Buffered(1))


def _half_swiglu(x, g_ref, wg_ref, wu_ref, wo_ref):
    h = _rms(x, g_ref[...]).astype(BF16)
    acc = jnp.zeros_like(x)
    for s in range(0, D_FF_PAD, FF_CHUNK):
        e = min(s + FF_CHUNK, D_FF_PAD)
        g = _dot(h, wg_ref[:, s:e])
        u = _dot(h, wu_ref[:, s:e])
        a = (g * jax.nn.sigmoid(g) * u).astype(BF16)
        acc = acc + _dot(a, wo_ref[s:e, :])
    return x + 0.5 * acc


def _ffn_kernel(x_ref, g_ref, wg_ref, wu_ref, wo_ref, o_ref):
    o_ref[...] = _half_swiglu(x_ref[...], g_ref, wg_ref, wu_ref, wo_ref)


def _ffn(x, g, wg, wu, wo, tm):
    t = x.shape[0]
    return pl.pallas_call(
        _ffn_kernel,
        grid=(t // tm,),
        in_specs=[
            pl.BlockSpec((tm, D_MODEL), lambda i: (i, 0)),
            _const_spec((1, D_MODEL)),
            _const_spec((D_MODEL, D_FF_PAD)),
            _const_spec((D_MODEL, D_FF_PAD)),
            _const_spec((D_FF_PAD, D_MODEL)),
        ],
        out_specs=pl.BlockSpec((tm, D_MODEL), lambda i: (i, 0)),
        out_shape=jax.ShapeDtypeStruct((t, D_MODEL), F32),
        compiler_params=pltpu.CompilerParams(
            dimension_semantics=("arbitrary",), vmem_limit_bytes=VMEM_LIMIT),
        name="ffn",
    )(x, g, wg, wu, wo)


def _rope(y, cos, sin, first_half):
    r_lo = pltpu.roll(y, 32, 1)
    r_hi = pltpu.roll(y, 96, 1)
    return y * cos + jnp.where(first_half, r_hi, r_lo) * sin


def _proj_kernel(x_ref, gm_ref, wp_ref, gain_ref, bias_ref, cos_ref, sin_ref, bd_ref,
                 qa_ref, qb_ref, qi_ref, kb_ref, ka_ref, ki_ref, vb_ref, va_ref, wi_ref,
                 sga_ref, sgb_ref, *, transposed):
    x = x_ref[...]
    tm = x.shape[0]
    h = _rms(x, gm_ref[...]).astype(BF16)
    cos = cos_ref[...]
    sin = sin_ref[...]
    bd = bd_ref[...]
    lane = lax.broadcasted_iota(I32, (tm, LANES), 1)
    first_half = (lane % HEAD_DIM) < (HEAD_DIM // 2)

    z_a = _dot(h, wp_ref[:, 0:A_W])
    tiles_a = []
    for j in range(A_W // LANES):
        z = z_a[:, j * LANES:(j + 1) * LANES]
        hi, lo = _split_bf16(z * z)
        ms = (_dot(hi, bd) + _dot(lo, bd)) * (1.0 / HEAD_DIM)
        y = z * lax.rsqrt(ms + EPS) * gain_ref[:, j * LANES:(j + 1) * LANES]
        tiles_a.append(_rope(y, cos, sin, first_half))

    z_b = _dot(h, wp_ref[:, A_W:A_W + B_W])
    tiles_b = [_rope(z_b[:, j * LANES:(j + 1) * LANES], cos, sin, first_half)
               for j in range(B_W // LANES)]

    z_c = _dot(h, wp_ref[:, A_W + B_W:A_W + B_W + C_W])
    z_g = _dot(h, wp_ref[:, A_W + B_W + C_W:P_W]) + bias_ref[...]

    scale = HEAD_DIM ** -0.5
    qa = jnp.concatenate(tiles_a[0:4], axis=1) * (scale * LOG2E)
    qb = jnp.concatenate(tiles_a[4:8], axis=1) * (scale * LOG2E)
    qi = jnp.concatenate(tiles_b[0:4], axis=1) * scale
    wi_scale = N_HEADS ** -0.5
    if transposed:
        qa_ref[...] = qa.T.astype(BF16)
        qb_ref[...] = qb.T.astype(BF16)
        qi_ref[...] = qi.T.astype(BF16)
        kb_ref[...] = tiles_a[8].T
        ka_ref[...] = tiles_a[9].T[0:HEAD_DIM, :]
        ki_ref[...] = tiles_b[4].T[0:HEAD_DIM, :]
        zc_t = z_c.T
        vb_ref[...] = zc_t[0:LANES, :]
        va_ref[...] = zc_t[LANES:LANES + HEAD_DIM, :]
        wi_ref[...] = zc_t[LANES + HEAD_DIM:LANES + HEAD_DIM + N_HEADS, :] * wi_scale
    else:
        qa_ref[...] = qa.astype(BF16)
        qb_ref[...] = qb.astype(BF16)
        qi_ref[...] = qi.astype(BF16)
        kb_ref[...] = tiles_a[8]
        ka_ref[...] = tiles_a[9][:, 0:HEAD_DIM]
        ki_ref[...] = tiles_b[4][:, 0:HEAD_DIM]
        vb_ref[...] = z_c[:, 0:LANES]
        va_ref[...] = z_c[:, LANES:LANES + HEAD_DIM]
        wi_ref[...] = z_c[:, LANES + HEAD_DIM:LANES + HEAD_DIM + N_HEADS] * wi_scale
    sg = jax.nn.sigmoid(z_g)
    sga_ref[...] = sg[:, 0:D_MODEL]
    sgb_ref[...] = sg[:, D_MODEL:G_W]


def _proj(x, gm, wp, gain, bias, cos_t, sin_t, bd, tm, seq):
    t = x.shape[0]
    n_tab = cos_t.shape[0] // tm
    row = lambda w: pl.BlockSpec((tm, w), lambda i: (i, 0))
    widths = (W_Q, W_Q, W_Q, 2 * HEAD_DIM, HEAD_DIM, HEAD_DIM, 2 * HEAD_DIM, HEAD_DIM, N_HEADS)
    dtypes = (BF16, BF16, BF16, F32, F32, F32, F32, F32, F32)
    if seq is None:
        specs = [row(w) for w in widths]
        shapes = [jax.ShapeDtypeStruct((t, w), d) for w, d in zip(widths, dtypes)]
    else:
        nb = seq // tm
        specs = [pl.BlockSpec((None, w, tm), lambda i: (i // nb, 0, i % nb)) for w in widths]
        shapes = [jax.ShapeDtypeStruct((t // seq, w, seq), d) for w, d in zip(widths, dtypes)]
    out_specs = tuple(specs) + (row(D_MODEL), row(D_MODEL))
    out_shape = tuple(shapes) + (jax.ShapeDtypeStruct((t, D_MODEL), F32), jax.ShapeDtypeStruct((t, D_MODEL), F32))
    transposed = seq is not None
    return pl.pallas_call(
        functools.partial(_proj_kernel, transposed=transposed),
        grid=(t // tm,),
        in_specs=[
            row(D_MODEL),
            _const_spec((1, D_MODEL)),
            _const_spec((D_MODEL, P_W)),
            _const_spec((1, A_W)),
            _const_spec((1, G_W)),
            pl.BlockSpec((tm, LANES), lambda i: (i % n_tab, 0)),
            pl.BlockSpec((tm, LANES), lambda i: (i % n_tab, 0)),
            _const_spec((LANES, LANES)),
        ],
        out_specs=out_specs,
        out_shape=out_shape,
        compiler_params=pltpu.CompilerParams(
            dimension_semantics=("arbitrary",), vmem_limit_bytes=VMEM_LIMIT),
        name="proj",
    )(x, gm, wp, gain, bias, cos_t, sin_t, bd)


def _count(mask, axis):
    return jnp.sum(jnp.where(mask, 1.0, 0.0), axis=axis, keepdims=True)


def _key_to_score(key):
    return lax.bitcast_convert_type(key ^ ((key >> 31) & np.int32(0x7FFFFFFF)), F32)


N_VALUE_BITS = 32


def _kth_largest(load_scores, axis, vec_shape, k_sel, side_work=None):
    one = np.int32(1)

    def value_bit(it, ans):
        if side_work is not None:
            side_work(it)
        cand = ans | lax.shift_left(one, 31 - it)
        thr = _key_to_score(cand ^ INT_MIN)
        return jnp.where(_count(load_scores() >= thr, axis) >= k_sel, cand, ans)

    return _key_to_score(lax.fori_loop(0, N_VALUE_BITS, value_bit, jnp.zeros(vec_shape, I32)) ^ INT_MIN)


def _score_to_key(score):
    b = lax.bitcast_convert_type(score, I32)
    return b ^ ((b >> 31) & np.int32(0x7FFFFFFF))


def _count_rows16(mask):
    rows, q = mask.shape
    assert rows % 16 == 0 and rows // 16 <= 256
    ones = jnp.where(mask, jnp.ones((), BF16), jnp.zeros((), BF16))
    part = ones[0:16, :]
    for r in range(16, rows, 16):
        part = part + ones[r:r + 16, :]
    return jnp.sum(part.astype(F32), axis=0, keepdims=True)


def _kth_largest_rows(load_scores, coarse_ref, q_cols, k_sel):
    n = load_scores().shape[0]
    coarse_ref[0:n, :] = load_scores().astype(BF16)
    one = np.int32(1)

    def bf16_of(pattern):
        k = pattern - np.int32(0x8000)
        bits = k ^ ((k >> 15) & np.int32(0x7FFF))
        return lax.bitcast_convert_type(lax.shift_left(bits, 16), F32).astype(BF16)

    def coarse_bit(it, ans):
        cand = ans | lax.shift_left(one, 15 - it)
        return jnp.where(_count_rows16(coarse_ref[0:n, :] >= bf16_of(cand)) >= k_sel, cand, ans)

    coarse = lax.fori_loop(0, 16, coarse_bit, jnp.zeros((1, q_cols), I32))
    base = _score_to_key(bf16_of(coarse).astype(F32)) - np.int32(2 ** 15)

    def fine_bit(it, off):
        cand = off | lax.shift_left(one, 16 - it)
        return jnp.where(_count(load_scores() >= _key_to_score(base + cand), 0) >= k_sel, cand, off)

    return _key_to_score(base + lax.fori_loop(0, 17, fine_bit, jnp.zeros((1, q_cols), I32)))


def _topk_bias(store, load_keys, key_index, axis, vec_shape, k_sel, n_index_bits, coarse_ref=None,
               side_work=None):
    one = np.int32(1)
    if coarse_ref is None:
        thr = _kth_largest(load_keys, axis, vec_shape, k_sel, side_work)
    else:
        thr = _kth_largest_rows(load_keys, coarse_ref, vec_shape[1], k_sel)
    has_tie = jnp.max(_count(load_keys() >= thr, axis)) > k_sel

    @pl.when(jnp.logical_not(has_tie))
    def _():
        store(jnp.where(load_keys() >= thr, 0.0, NEG_INF))

    @pl.when(has_tie)
    def _():
        need = k_sel - _count(load_keys() > thr, axis)

        def index_bit(it, cut):
            cand = cut | lax.shift_left(one, n_index_bits - 1 - it)
            hit = jnp.where(load_keys() == thr, jnp.where(key_index() < cand, 1.0, 0.0), 0.0)
            return jnp.where(jnp.sum(hit, axis=axis, keepdims=True) < need, cand, cut)

        cut = lax.fori_loop(0, n_index_bits, index_bit, jnp.zeros(vec_shape, I32))
        keys = load_keys()
        keep_eq = jnp.where(key_index() <= cut, 0.0, NEG_INF)
        store(jnp.where(keys > thr, 0.0, jnp.where(keys == thr, keep_eq, NEG_INF)))


V_ROWS = HEAD_DIM + 16


def _pipelined_heads(n_chunks, chunk, head_logits, values_t, lgb, store_out):
    rows = lambda c: slice(c * chunk, (c + 1) * chunk)

    def logits_stage(h, slot):
        logits = head_logits(h)
        m = None
        for c in range(n_chunks):
            lg = logits(c)
            lgb[slot, rows(c), :] = lg
            cm = jnp.max(lg, axis=0, keepdims=True)
            m = cm if m is None else jnp.maximum(m, cm)
        return m

    def step(h, m_cur, cur, nxt):
        logits = head_logits(jnp.minimum(h + 1, N_HEADS - 1))
        acc = None
        m_next = None
        for c in range(n_chunks):
            lg = logits(c)
            nxt[rows(c), :] = lg
            cm = jnp.max(lg, axis=0, keepdims=True)
            m_next = cm if m_next is None else jnp.maximum(m_next, cm)
            p = jnp.exp2(cur[rows(c), :] - m_cur).astype(BF16)
            pv = _dot(values_t(h, c), p)
            acc = pv if acc is None else acc + pv
        store_out(h, acc[0:HEAD_DIM, :] / acc[HEAD_DIM:HEAD_DIM + 1, :])
        return m_next

    def body(t, m_cur):
        m_mid = step(2 * t, m_cur, lgb.at[0], lgb.at[1])
        return step(2 * t + 1, m_mid, lgb.at[1], lgb.at[0])

    lax.fori_loop(0, N_HEADS // 2, body, logits_stage(0, 0))


def _attn_prompt_kernel(qi_ref, qa_ref, qb_ref, ki_ref, ka_ref, kb_ref, va_ref, vb_ref, wit,
                        oa_ref, ob_ref,
                        kib, kab, kbb, vat, vbt, kmh, kml, sc, sc16, bias, lgb, ot):
    seq = ki_ref.shape[1]
    tq = MOBA_BLOCK
    n_blk = seq // tq

    kk = jnp.concatenate([ka_ref[...], ki_ref[...]], axis=0).T
    kab[...] = kk[:, 0:HEAD_DIM].astype(BF16)
    kib[...] = kk[:, HEAD_DIM:2 * HEAD_DIM].astype(BF16)
    kb = kb_ref[...].T
    ones = jnp.ones((V_ROWS - HEAD_DIM, seq), BF16)
    vat[...] = jnp.concatenate([va_ref[...].astype(BF16), ones], axis=0)
    means = jnp.concatenate(
        [jnp.mean(kb[j * tq:(j + 1) * tq, :], axis=0, keepdims=True) for j in range(n_blk)]
        + [jnp.zeros((kmh.shape[1] - n_blk, 2 * HEAD_DIM), F32)], axis=0)
    for n in range(HKV_B):
        kbb[n] = kb[:, n * HEAD_DIM:(n + 1) * HEAD_DIM].astype(BF16)
        vbt[n] = jnp.concatenate([vb_ref[n * HEAD_DIM:(n + 1) * HEAD_DIM, :].astype(BF16), ones], axis=0)
        hi, lo = _split_bf16(means[:, n * HEAD_DIM:(n + 1) * HEAD_DIM])
        kmh[n] = hi
        kml[n] = lo

    r_loc = lax.broadcasted_iota(I32, (tq, tq), 0)
    c_loc = lax.broadcasted_iota(I32, (tq, tq), 1)
    causal = r_loc <= c_loc

    for i in range(n_blk):
        c0 = i * tq
        lk = c0 + tq
        cols = slice(c0, c0 + tq)

        sc[0:lk, :] = jnp.zeros((lk, tq), F32)

        def idx_head(h, carry):
            off = pl.multiple_of(h * HEAD_DIM, HEAD_DIM)
            s = _dot(kib[0:lk, :], qi_ref[pl.ds(off, HEAD_DIM), cols])
            sc[0:lk, :] += jnp.maximum(s, 0.0) * wit[pl.ds(h, 1), cols]
            return carry

        lax.fori_loop(0, N_HEADS, idx_head, 0)

        if i == 0:
            bias[0:tq, :] = jnp.where(causal, 0.0, NEG_INF)
        else:
            sc[c0:lk, :] = jnp.where(causal, sc[c0:lk, :], NEG_INF)

            def store_bias(v):
                bias[0:lk, :] = v

            _topk_bias(store_bias, lambda: sc[0:lk, :], lambda: lax.broadcasted_iota(I32, (lk, tq), 0),
                       0, (1, tq), TOPK_A, (lk - 1).bit_length(), coarse_ref=sc16)

        blk = lambda c: slice(c * tq, (c + 1) * tq)

        def store_head(h, o):
            ot[pl.ds(pl.multiple_of(h * HEAD_DIM, HEAD_DIM), HEAD_DIM), :] = o

        def dsa_logits(h):
            q_t = qa_ref[pl.ds(pl.multiple_of(h * HEAD_DIM, HEAD_DIM), HEAD_DIM), cols]
            return lambda c: _dot(kab[blk(c), :], q_t) + bias[blk(c), :]

        _pipelined_heads(i + 1, tq, dsa_logits, lambda h, c: vat[:, blk(c)], lgb, store_head)
        oa_ref[cols, :] = ot[...].T.astype(BF16)

        def moba_logits(h):
            n = h // G_B
            q_t = qb_ref[pl.ds(pl.multiple_of(h * HEAD_DIM, HEAD_DIM), HEAD_DIM), cols]
            block_bias = [None] * i
            if i > TOPK_B:
                gate = _dot(kmh[n], q_t) + _dot(kml[n], q_t)
                rows = lax.broadcasted_iota(I32, gate.shape, 0)
                for j in range(i):
                    gj = gate[j:j + 1, :]
                    beats = jnp.where(rows < j, jnp.where(gate >= gj, 1, 0), jnp.where(gate > gj, 1, 0))
                    rank = jnp.sum(jnp.where(rows < i, beats, 0), axis=0, keepdims=True)
                    block_bias[j] = jnp.where(rank < TOPK_B, 0.0, NEG_INF)

            def logits(c):
                lg = _dot(kbb[n, blk(c), :], q_t)
                if c == i:
                    return jnp.where(causal, lg, NEG_INF)
                return lg if block_bias[c] is None else lg + block_bias[c]

            return logits

        _pipelined_heads(i + 1, tq, moba_logits, lambda h, c: vbt[h // G_B, :, blk(c)], lgb, store_head)
        ob_ref[cols, :] = ot[...].T.astype(BF16)


def _attn_prompt(qi_t, qa_t, qb_t, ki_t, ka_t, kb_t, va_t, vb_t, wi_t):
    n_seq, _, seq = qi_t.shape
    tq = MOBA_BLOCK
    fm = lambda w: pl.BlockSpec((None, w, seq), lambda b: (b, 0, 0))
    row = lambda w: pl.BlockSpec((seq, w), lambda b: (b, 0))
    return pl.pallas_call(
        _attn_prompt_kernel,
        grid=(n_seq,),
        in_specs=[fm(W_Q), fm(W_Q), fm(W_Q), fm(HEAD_DIM), fm(HEAD_DIM), fm(2 * HEAD_DIM),
                  fm(HEAD_DIM), fm(2 * HEAD_DIM), fm(N_HEADS)],
        out_specs=(row(W_Q), row(W_Q)),
        out_shape=(jax.ShapeDtypeStruct((n_seq * seq, W_Q), BF16),
                   jax.ShapeDtypeStruct((n_seq * seq, W_Q), BF16)),
        scratch_shapes=[
            pltpu.VMEM((seq, HEAD_DIM), BF16),
            pltpu.VMEM((seq, HEAD_DIM), BF16),
            pltpu.VMEM((HKV_B, seq, HEAD_DIM), BF16),
            pltpu.VMEM((V_ROWS, seq), BF16),
            pltpu.VMEM((HKV_B, V_ROWS, seq), BF16),
            pltpu.VMEM((HKV_B, 16, HEAD_DIM), BF16),
            pltpu.VMEM((HKV_B, 16, HEAD_DIM), BF16),
            pltpu.VMEM((seq, tq), F32),
            pltpu.VMEM((seq, tq), BF16),
            pltpu.VMEM((seq, tq), F32),
            pltpu.VMEM((2, seq, tq), F32),
            pltpu.VMEM((W_Q, tq), F32),
        ],
        compiler_params=pltpu.CompilerParams(
            dimension_semantics=("arbitrary",), vmem_limit_bytes=VMEM_LIMIT),
        name="attn_prompt",
    )(qi_t, qa_t, qb_t, ki_t, ka_t, kb_t, va_t, vb_t, wi_t)


ROWS = N_HEADS * 8
SAMPLE_GROUP = 4


def _attn_sample_kernel(pt_ref, cki_hbm, cka_hbm, cva_hbm, ckb_hbm, cvb_hbm,
                        nki_ref, nka_ref, nva_ref, nkb_ref, nvb_ref,
                        qi_ref, qa_ref, qb_ref, wc_ref,
                        oa_ref, ob_ref,
                        ski, ska, sva, skb, svb, scores, bias, sem,
                        *, group, n_new, past):
    g = pl.program_id(0)
    n_groups = pl.num_programs(0)
    n_pages = past // PAGE
    lpad = ski.shape[3]
    slot = g % 2
    pools = ((cki_hbm, ski), (cka_hbm, ska), (cva_hbm, sva), (ckb_hbm, skb), (cvb_hbm, svb))

    def page_copies(grp, to_slot, t):
        s = t // n_pages
        p = t % n_pages
        page = pt_ref[grp * group + s, p]
        lanes = pl.ds(pl.multiple_of(p * PAGE, PAGE), PAGE)
        return [pltpu.make_async_copy(hbm.at[page], slab.at[to_slot, s, :, lanes], sem.at[to_slot])
                for hbm, slab in pools]

    def start_group(grp, to_slot):
        def body(t, carry):
            for cp in page_copies(grp, to_slot, t):
                cp.start()
            return carry
        lax.fori_loop(0, group * n_pages, body, 0)

    def wait_group(to_slot):
        for _, slab in pools:
            filled = slab.at[to_slot, :, :, 0:past]
            pltpu.make_async_copy(filled, filled, sem.at[to_slot]).wait()

    @pl.when(g == 0)
    def _():
        start_group(0, 0)

    wait_group(slot)

    n_starts = group * n_pages
    assert n_starts % N_VALUE_BITS == 0

    def prefetch_next(it):
        for k in range(n_starts // N_VALUE_BITS):
            for cp in page_copies(g + 1, 1 - slot, it * (n_starts // N_VALUE_BITS) + k):
                cp.start()

    n_blk = past // MOBA_BLOCK
    col = lax.broadcasted_iota(I32, (8, lpad), 1)
    qrow = lax.broadcasted_iota(I32, (8, lpad), 0)
    visible8 = col <= past + jnp.minimum(qrow, n_new - 1)
    for s in range(group):
        for slab, new in ((ski, nki_ref), (ska, nka_ref), (sva, nva_ref), (skb, nkb_ref), (svb, nvb_ref)):
            slab[slot, s, :, past:lpad] = jnp.zeros((slab.shape[2], lpad - past), F32)
            slab[slot, s, :, past:past + n_new] = new[s]
        st = _dot(qi_ref[s], ski[slot, s].astype(BF16))
        st = jnp.maximum(st, 0.0) * wc_ref[s]
        score = st[0:8, :]
        for h in range(1, N_HEADS):
            score = score + st[h * 8:(h + 1) * 8, :]
        scores[s * 8:(s + 1) * 8, :] = jnp.where(visible8, score, NEG_INF)

    rows = group * 8

    def store_bias(v):
        bias[...] = v

    for has_next, side_work in ((g + 1 < n_groups, prefetch_next), (g + 1 == n_groups, None)):
        @pl.when(has_next)
        def _():
            _topk_bias(store_bias, lambda: scores[...], lambda: lax.broadcasted_iota(I32, (rows, lpad), 1),
                       1, (rows, 1), TOPK_A, (lpad - 1).bit_length(), side_work=side_work)

    colr = lax.broadcasted_iota(I32, (ROWS, LANES), 1)
    qr = lax.broadcasted_iota(I32, (ROWS, LANES), 0) % 8
    visible_tail = colr <= jnp.minimum(qr, n_new - 1)
    for s in range(group):
        lg = _dot(qa_ref[s], ska[slot, s].astype(BF16)) + jnp.tile(bias[s * 8:(s + 1) * 8, :], (N_HEADS, 1))
        m = jnp.max(lg, axis=1, keepdims=True)
        p = jnp.exp2(lg - m)
        l = jnp.sum(p, axis=1, keepdims=True)
        oa_ref[s] = _dot_nt(p.astype(BF16), sva[slot, s].astype(BF16)) / l

        lgb = _dot(qb_ref[s], skb[slot, s].astype(BF16))
        gate = [jnp.sum(lgb[:, j * MOBA_BLOCK:(j + 1) * MOBA_BLOCK], axis=1, keepdims=True)
                for j in range(n_blk)]
        pieces = []
        for j in range(n_blk):
            rank = jnp.zeros((ROWS, 1), I32)
            for j2 in range(n_blk):
                if j2 != j:
                    rank = rank + jnp.where((gate[j2] >= gate[j]) if j2 < j else (gate[j2] > gate[j]), 1, 0)
            pieces.append(lgb[:, j * MOBA_BLOCK:(j + 1) * MOBA_BLOCK] + jnp.where(rank < TOPK_B, 0.0, NEG_INF))
        pieces.append(jnp.where(visible_tail, lgb[:, past:lpad], NEG_INF))
        lgm = jnp.concatenate(pieces, axis=1)
        m = jnp.max(lgm, axis=1, keepdims=True)
        p = jnp.exp2(lgm - m)
        l = jnp.sum(p, axis=1, keepdims=True)
        ob_ref[s] = _dot_nt(p.astype(BF16), svb[slot, s].astype(BF16)) / l


def _attn_sample(page_table, cki, cka, cva, ckb, cvb, nki, nka, nva, nkb, nvb, qi, qa, qb2, wcol, group):
    n_seq, n_pages = page_table.shape
    n_new = nki.shape[2]
    past = n_pages * PAGE
    lpad = past + LANES
    hbm = pl.BlockSpec(memory_space=pl.ANY)
    grp = lambda a, b: pl.BlockSpec((group, a, b), lambda g, pt: (g, 0, 0))
    slab = lambda w: pltpu.VMEM((2, group, w, lpad), F32)
    grid_spec = pltpu.PrefetchScalarGridSpec(
        num_scalar_prefetch=1,
        grid=(n_seq // group,),
        in_specs=[hbm, hbm, hbm, hbm, hbm,
                  grp(HEAD_DIM, n_new), grp(HEAD_DIM, n_new), grp(HEAD_DIM, n_new),
                  grp(2 * HEAD_DIM, n_new), grp(2 * HEAD_DIM, n_new),
                  grp(ROWS, HEAD_DIM), grp(ROWS, HEAD_DIM), grp(ROWS, 2 * HEAD_DIM), grp(ROWS, 1)],
        out_specs=(grp(ROWS, HEAD_DIM), grp(ROWS, 2 * HEAD_DIM)),
        scratch_shapes=[
            slab(HEAD_DIM),
            slab(HEAD_DIM),
            slab(HEAD_DIM),
            slab(2 * HEAD_DIM),
            slab(2 * HEAD_DIM),
            pltpu.VMEM((group * 8, lpad), F32),
            pltpu.VMEM((group * 8, lpad), F32),
            pltpu.SemaphoreType.DMA((2,)),
        ],
    )
    return pl.pallas_call(
        functools.partial(_attn_sample_kernel, group=group, n_new=n_new, past=past),
        grid_spec=grid_spec,
        out_shape=(jax.ShapeDtypeStruct((n_seq, ROWS, HEAD_DIM), F32),
                   jax.ShapeDtypeStruct((n_seq, ROWS, 2 * HEAD_DIM), F32)),
        compiler_params=pltpu.CompilerParams(
            dimension_semantics=("arbitrary",), vmem_limit_bytes=VMEM_LIMIT),
        name="attn_sample",
    )(page_table, cki, cka, cva, ckb, cvb, nki, nka, nva, nkb, nvb, qi, qa, qb2, wcol)


def _merge_ffn_kernel(x_ref, oa_ref, ob_ref, sga_ref, sgb_ref, wa_ref, wb_ref, wo_ref,
                      g_ref, wg_ref, wu_ref, wf_ref, o_ref):
    m = sga_ref[...] * _dot(oa_ref[...], wa_ref[...]) + sgb_ref[...] * _dot(ob_ref[...], wb_ref[...])
    x = x_ref[...] + _dot(m.astype(BF16), wo_ref[...])
    o_ref[...] = _half_swiglu(x, g_ref, wg_ref, wu_ref, wf_ref)


def _merge_ffn(x, oa, ob, sga, sgb, wa, wb, wo, g, wg, wu, wf, tm):
    t = x.shape[0]
    row = lambda w: pl.BlockSpec((tm, w), lambda i: (i, 0))
    return pl.pallas_call(
        _merge_ffn_kernel,
        grid=(t // tm,),
        in_specs=[row(D_MODEL), row(W_Q), row(W_Q), row(D_MODEL), row(D_MODEL),
                  _const_spec((W_Q, D_MODEL)), _const_spec((W_Q, D_MODEL)), _const_spec((D_MODEL, D_MODEL)),
                  _const_spec((1, D_MODEL)), _const_spec((D_MODEL, D_FF_PAD)), _const_spec((D_MODEL, D_FF_PAD)),
                  _const_spec((D_FF_PAD, D_MODEL))],
        out_specs=row(D_MODEL),
        out_shape=jax.ShapeDtypeStruct((t, D_MODEL), F32),
        compiler_params=pltpu.CompilerParams(
            dimension_semantics=("arbitrary",), vmem_limit_bytes=VMEM_LIMIT),
        name="merge_ffn",
    )(x, oa, ob, sga, sgb, wa, wb, wo, g, wg, wu, wf)


def _ffn_weights(w_in, w_out):
    pad = D_FF_PAD - D_FF
    wg = jnp.pad(w_in[:, :D_FF], ((0, 0), (0, pad))).astype(BF16)
    wu = jnp.pad(w_in[:, D_FF:], ((0, 0), (0, pad))).astype(BF16)
    wo = jnp.pad(w_out, ((0, pad), (0, 0))).astype(BF16)
    return wg, wu, wo


def _rope_tables(pos):
    half = HEAD_DIM // 2
    inv = ROPE_THETA ** (-jnp.arange(half, dtype=F32) / half)
    ang = pos.astype(F32)[:, None] * inv[None, :]
    cos, sin = jnp.cos(ang), jnp.sin(ang)
    return (jnp.concatenate([cos, cos, cos, cos], axis=1),
            jnp.concatenate([-sin, sin, -sin, sin], axis=1))


def _head_rows(a, n_seq, n_new):
    w = a.shape[1] // N_HEADS
    a = a.reshape(n_seq, n_new, N_HEADS, w).transpose(0, 2, 1, 3)
    a = jnp.pad(a, ((0, 0), (0, 0), (0, 8 - n_new), (0, 0)))
    return a.reshape(n_seq, ROWS, w)


def _from_head_rows(a, n_seq, n_new):
    w = a.shape[2]
    a = a.reshape(n_seq, N_HEADS, 8, w)[:, :, :n_new].transpose(0, 2, 1, 3)
    return a.reshape(n_seq * n_new, N_HEADS * w)


def kernel(x_prompt, x_sample, cache_k_a, cache_v_a, cache_kidx_a, cache_k_b, cache_v_b, page_table,
           ffn1_norm, ffn1_w_in, ffn1_w_out, mix_norm, w_in, q_norm_a, k_norm_a, q_norm_b, k_norm_b,
           gate_bias, w_up_a, w_up_b, w_out, ffn2_norm, ffn2_w_in, ffn2_w_out):
    n_seq, seq, _ = x_prompt.shape
    n_dec, n_new, _ = x_sample.shape
    depth = ffn1_norm.shape[0]
    n_pool = cache_k_a.shape[1]
    past = page_table.shape[1] * PAGE
    assert depth == 1 and seq % MOBA_BLOCK == 0 and past % MOBA_BLOCK == 0 and n_new <= 8
    assert seq >= 4 * TOPK_A and past + n_new >= 4 * TOPK_A
    l = 0
    tp, ts = n_seq * seq, n_dec * n_new

    xp = x_prompt.reshape(tp, D_MODEL)
    xs = x_sample.reshape(ts, D_MODEL)

    f1 = _ffn_weights(ffn1_w_in[l], ffn1_w_out[l])
    f2 = _ffn_weights(ffn2_w_in[l], ffn2_w_out[l])
    w = w_in[l]
    o = np.cumsum([0, W_Q, HEAD_DIM, HEAD_DIM, W_Q, HEAD_DIM, N_HEADS, W_Q, 2 * HEAD_DIM, 2 * HEAD_DIM,
                   D_MODEL, D_MODEL])
    c = lambda k: w[:, o[k]:o[k + 1]]
    z = lambda n: jnp.zeros((D_MODEL, n), F32)
    wp = jnp.concatenate([c(0), c(6), c(7), c(1), z(64), c(3), c(4), z(64), c(8), c(2), c(5), z(56),
                          c(9), c(10)], axis=1).astype(BF16)
    gain = jnp.concatenate([jnp.tile(q_norm_a[l], N_HEADS), jnp.tile(q_norm_b[l], N_HEADS),
                            jnp.tile(k_norm_b[l], HKV_B), k_norm_a[l], jnp.zeros((64,), F32)])[None, :]
    bias = gate_bias[l][None, :]
    lane = np.arange(LANES)
    bd = jnp.asarray((lane[:, None] // HEAD_DIM) == (lane[None, :] // HEAD_DIM), BF16)
    cos_p, sin_p = _rope_tables(jnp.arange(seq, dtype=I32))
    cos_s, sin_s = _rope_tables(past + (jnp.arange(ts, dtype=I32) % n_new))
    wa, wb, wo = w_up_a[l].astype(BF16), w_up_b[l].astype(BF16), w_out[l].astype(BF16)
    g1, gm, g2 = ffn1_norm[l][None, :], mix_norm[l][None, :], ffn2_norm[l][None, :]

    tm = 512
    xp1 = _ffn(xp, g1, *f1, tm)
    (qa_t, qb_t, qi_t, kb_t, ka_t, ki_t, vb_t, va_t, wi_t, sga_p, sgb_p) = _proj(
        xp1, gm, wp, gain, bias, cos_p, sin_p, bd, tm, seq)
    oa_p, ob_p = _attn_prompt(qi_t, qa_t, qb_t, ki_t, ka_t, kb_t, va_t, vb_t, wi_t)
    yp = _merge_ffn(xp1, oa_p, ob_p, sga_p, sgb_p, wa, wb, wo, g2, *f2, tm)

    xs1 = _ffn(xs, g1, *f1, ts)
    (qa_s, qb_s, qi_s, kb_s, ka_s, ki_s, vb_s, va_s, wi_s, sga_s, sgb_s) = _proj(
        xs1, gm, wp, gain, bias, cos_s, sin_s, bd, ts, None)
    qi_r = _head_rows(qi_s, n_dec, n_new)
    qa_r = _head_rows(qa_s, n_dec, n_new)
    qb_r = _head_rows(qb_s, n_dec, n_new)
    zq = jnp.zeros_like(qb_r)
    first = (jnp.arange(ROWS) < ROWS // HKV_B)[None, :, None]
    qb2 = jnp.concatenate([jnp.where(first, qb_r, zq), jnp.where(first, zq, qb_r)], axis=2)
    wcol = _head_rows(wi_s, n_dec, n_new)
    pages_t = lambda a: jnp.swapaxes(a[l].reshape(n_pool, PAGE, -1), 1, 2)
    new_t = lambda a: jnp.swapaxes(a.reshape(n_dec, n_new, a.shape[1]), 1, 2)
    oa_r, ob_r = _attn_sample(
        page_table,
        pages_t(cache_kidx_a), pages_t(cache_k_a), pages_t(cache_v_a), pages_t(cache_k_b), pages_t(cache_v_b),
        new_t(ki_s), new_t(ka_s), new_t(va_s), new_t(kb_s), new_t(vb_s), qi_r, qa_r, qb2, wcol, SAMPLE_GROUP)
    oa_s = _from_head_rows(oa_r, n_dec, n_new).astype(BF16)
    ob_sel = jnp.where(first, ob_r[:, :, :HEAD_DIM], ob_r[:, :, HEAD_DIM:])
    ob_s = _from_head_rows(ob_sel, n_dec, n_new).astype(BF16)
    ys = _merge_ffn(xs1, oa_s, ob_s, sga_s, sgb_s, wa, wb, wo, g2, *f2, ts)

    d = depth
    tok = lambda a: jnp.swapaxes(a, 1, 2)
    return (yp.reshape(n_seq, seq, D_MODEL), ys.reshape(n_dec, n_new, D_MODEL),
            tok(ka_t).reshape(d, n_seq, seq, 1, HEAD_DIM), tok(va_t).reshape(d, n_seq, seq, 1, HEAD_DIM),
            tok(ki_t).reshape(d, n_seq, seq, HEAD_DIM),
            tok(kb_t).reshape(d, n_seq, seq, HKV_B, HEAD_DIM), tok(vb_t).reshape(d, n_seq, seq, HKV_B, HEAD_DIM),
            ka_s.reshape(d, n_dec, n_new, 1, HEAD_DIM), va_s.reshape(d, n_dec, n_new, 1, HEAD_DIM),
            ki_s.reshape(d, n_dec, n_new, HEAD_DIM),
            kb_s.reshape(d, n_dec, n_new, HKV_B, HEAD_DIM), vb_s.reshape(d, n_dec, n_new, HKV_B, HEAD_DIM))
```

```python
import functools

import numpy as np
import jax
import jax.numpy as jnp
from jax import lax
from jax.experimental import pallas as pl
from jax.experimental.pallas import tpu as pltpu

F32 = jnp.float32
BF16 = jnp.bfloat16
I32 = jnp.int32

D_MODEL = 1024
HEAD_DIM = 64
N_HEADS = 8
HKV_B = 2
G_B = N_HEADS // HKV_B
D_FF = 2752
TOPK_A = 256
MOBA_BLOCK = 256
TOPK_B = 3
PAGE = 128
ROPE_THETA = 10000.0
EPS = 1e-6

LANES = 128
D_FF_PAD = 2816
FF_CHUNK = 512
W_Q = N_HEADS * HEAD_DIM

A_W = 1280
B_W = 640
C_W = 256
G_W = 2 * D_MODEL
P_W = A_W + B_W + C_W + G_W

LOG2E = 1.4426950408889634
INT_MIN = np.int32(-2 ** 31)
NEG_INF = float("-inf")
VMEM_LIMIT = 56 * 1024 * 1024


def _dot(a, b):
    return jnp.dot(a, b, preferred_element_type=F32)


def _dot_nt(a, b):
    return lax.dot_general(a, b, (((1,), (1,)), ((), ())), preferred_element_type=F32)


def _split_bf16(x):
    hi = x.astype(BF16)
    lo = (x - hi.astype(F32)).astype(BF16)
    return hi, lo


def _rms(x, g):
    ms = jnp.mean(x * x, axis=-1, keepdims=True)
    return x * lax.rsqrt(ms + EPS) * g


def _const_spec(shape):
    nd = len(shape)
    return pl.BlockSpec(shape, lambda *_: (0,) * nd, pipeline_mode=pl.Buffered(1))


def _half_swiglu(x, g_ref, wg_ref, wu_ref, wo_ref):
    h = _rms(x, g_ref[...]).astype(BF16)
    acc = jnp.zeros_like(x)
    for s in range(0, D_FF_PAD, FF_CHUNK):
        e = min(s + FF_CHUNK, D_FF_PAD)
        g = _dot(h, wg_ref[:, s:e])
        u = _dot(h, wu_ref[:, s:e])
        a = (g * jax.nn.sigmoid(g) * u).astype(BF16)
        acc = acc + _dot(a, wo_ref[s:e, :])
    return x + 0.5 * acc


def _ffn_kernel(x_ref, g_ref, wg_ref, wu_ref, wo_ref, o_ref):
    o_ref[...] = _half_swiglu(x_ref[...], g_ref, wg_ref, wu_ref, wo_ref)


def _ffn(x, g, wg, wu, wo, tm):
    t = x.shape[0]
    return pl.pallas_call(
        _ffn_kernel,
        grid=(t // tm,),
        in_specs=[
            pl.BlockSpec((tm, D_MODEL), lambda i: (i, 0)),
            _const_spec((1, D_MODEL)),
            _const_spec((D_MODEL, D_FF_PAD)),
            _const_spec((D_MODEL, D_FF_PAD)),
            _const_spec((D_FF_PAD, D_MODEL)),
        ],
        out_specs=pl.BlockSpec((tm, D_MODEL), lambda i: (i, 0)),
        out_shape=jax.ShapeDtypeStruct((t, D_MODEL), F32),
        compiler_params=pltpu.CompilerParams(
            dimension_semantics=("arbitrary",), vmem_limit_bytes=VMEM_LIMIT),
        name="ffn",
    )(x, g, wg, wu, wo)


def _rope(y, cos, sin, first_half):
    r_lo = pltpu.roll(y, 32, 1)
    r_hi = pltpu.roll(y, 96, 1)
    return y * cos + jnp.where(first_half, r_hi, r_lo) * sin


def _proj_kernel(x_ref, gm_ref, wp_ref, gain_ref, bias_ref, cos_ref, sin_ref, bd_ref,
                 qa_ref, qb_ref, qi_ref, kb_ref, ka_ref, ki_ref, vb_ref, va_ref, wi_ref,
                 sga_ref, sgb_ref, *, transposed):
    x = x_ref[...]
    tm = x.shape[0]
    h = _rms(x, gm_ref[...]).astype(BF16)
    cos = cos_ref[...]
    sin = sin_ref[...]
    bd = bd_ref[...]
    lane = lax.broadcasted_iota(I32, (tm, LANES), 1)
    first_half = (lane % HEAD_DIM) < (HEAD_DIM // 2)

    z_a = _dot(h, wp_ref[:, 0:A_W])
    tiles_a = []
    for j in range(0, A_W // LANES, 2):
        z2 = z_a[:, j * LANES:(j + 2) * LANES]
        hi, lo = _split_bf16(z2 * z2)
        ms = (_dot(hi, bd) + _dot(lo, bd)) * (1.0 / HEAD_DIM)
        y2 = z2 * lax.rsqrt(ms + EPS) * gain_ref[:, j * LANES:(j + 2) * LANES]
        for t in range(2):
            tiles_a.append(_rope(y2[:, t * LANES:(t + 1) * LANES], cos, sin, first_half))

    z_b = _dot(h, wp_ref[:, A_W:A_W + B_W])
    tiles_b = [_rope(z_b[:, j * LANES:(j + 1) * LANES], cos, sin, first_half)
               for j in range(B_W // LANES)]

    z_c = _dot(h, wp_ref[:, A_W + B_W:A_W + B_W + C_W])
    z_g = _dot(h, wp_ref[:, A_W + B_W + C_W:P_W]) + bias_ref[...]

    scale = HEAD_DIM ** -0.5
    qa = jnp.concatenate(tiles_a[0:4], axis=1) * (scale * LOG2E)
    qb = jnp.concatenate(tiles_a[4:8], axis=1) * (scale * LOG2E)
    qi = jnp.concatenate(tiles_b[0:4], axis=1) * scale
    wi_scale = N_HEADS ** -0.5
    if transposed:
        qa_ref[...] = qa.T.astype(BF16)
        qb_ref[...] = qb.T.astype(BF16)
        qi_ref[...] = qi.T.astype(BF16)
        kb_ref[...] = tiles_a[8].T
        ka_ref[...] = tiles_a[9].T[0:HEAD_DIM, :]
        ki_ref[...] = tiles_b[4].T[0:HEAD_DIM, :]
        zc_t = z_c.T
        vb_ref[...] = zc_t[0:LANES, :]
        va_ref[...] = zc_t[LANES:LANES + HEAD_DIM, :]
        wi_ref[...] = zc_t[LANES + HEAD_DIM:LANES + HEAD_DIM + N_HEADS, :] * wi_scale
    else:
        qa_ref[...] = qa.astype(BF16)
        qb_ref[...] = qb.astype(BF16)
        qi_ref[...] = qi.astype(BF16)
        kb_ref[...] = tiles_a[8]
        ka_ref[...] = tiles_a[9][:, 0:HEAD_DIM]
        ki_ref[...] = tiles_b[4][:, 0:HEAD_DIM]
        vb_ref[...] = z_c[:, 0:LANES]
        va_ref[...] = z_c[:, LANES:LANES + HEAD_DIM]
        wi_ref[...] = z_c[:, LANES + HEAD_DIM:LANES + HEAD_DIM + N_HEADS] * wi_scale
    sg = jax.nn.sigmoid(z_g)
    sga_ref[...] = sg[:, 0:D_MODEL]
    sgb_ref[...] = sg[:, D_MODEL:G_W]


def _proj(x, gm, wp, gain, bias, cos_t, sin_t, bd, tm, seq):
    t = x.shape[0]
    n_tab = cos_t.shape[0] // tm
    row = lambda w: pl.BlockSpec((tm, w), lambda i: (i, 0))
    widths = (W_Q, W_Q, W_Q, 2 * HEAD_DIM, HEAD_DIM, HEAD_DIM, 2 * HEAD_DIM, HEAD_DIM, N_HEADS)
    dtypes = (BF16, BF16, BF16, F32, F32, F32, F32, F32, F32)
    if seq is None:
        specs = [row(w) for w in widths]
        shapes = [jax.ShapeDtypeStruct((t, w), d) for w, d in zip(widths, dtypes)]
    else:
        nb = seq // tm
        specs = [pl.BlockSpec((None, w, tm), lambda i: (i // nb, 0, i % nb)) for w in widths]
        shapes = [jax.ShapeDtypeStruct((t // seq, w, seq), d) for w, d in zip(widths, dtypes)]
    out_specs = tuple(specs) + (row(D_MODEL), row(D_MODEL))
    out_shape = tuple(shapes) + (jax.ShapeDtypeStruct((t, D_MODEL), F32), jax.ShapeDtypeStruct((t, D_MODEL), F32))
    transposed = seq is not None
    return pl.pallas_call(
        functools.partial(_proj_kernel, transposed=transposed),
        grid=(t // tm,),
        in_specs=[
            row(D_MODEL),
            _const_spec((1, D_MODEL)),
            _const_spec((D_MODEL, P_W)),
            _const_spec((1, A_W)),
            _const_spec((1, G_W)),
            pl.BlockSpec((tm, LANES), lambda i: (i % n_tab, 0)),
            pl.BlockSpec((tm, LANES), lambda i: (i % n_tab, 0)),
            _const_spec((2 * LANES, 2 * LANES)),
        ],
        out_specs=out_specs,
        out_shape=out_shape,
        compiler_params=pltpu.CompilerParams(
            dimension_semantics=("arbitrary",), vmem_limit_bytes=VMEM_LIMIT),
        name="proj",
    )(x, gm, wp, gain, bias, cos_t, sin_t, bd)


def _count(mask, axis):
    return jnp.sum(jnp.where(mask, 1.0, 0.0), axis=axis, keepdims=True)


def _key_to_score(key):
    return lax.bitcast_convert_type(key ^ ((key >> 31) & np.int32(0x7FFFFFFF)), F32)


N_VALUE_PASSES = 16


def _kth_largest(load_scores, axis, vec_shape, k_sel, side_work=None):
    def value_bits(it, ans):
        if side_work is not None:
            side_work(it)
        shift = 30 - 2 * it
        best = ans
        for digit in (1, 2, 3):
            cand = ans | lax.shift_left(np.int32(digit), shift)
            thr = _key_to_score(cand ^ INT_MIN)
            best = jnp.where(_count(load_scores() >= thr, axis) >= k_sel, cand, best)
        return best

    return _key_to_score(lax.fori_loop(0, N_VALUE_PASSES, value_bits, jnp.zeros(vec_shape, I32)) ^ INT_MIN)


def _score_to_key(score):
    b = lax.bitcast_convert_type(score, I32)
    return b ^ ((b >> 31) & np.int32(0x7FFFFFFF))


def _count_rows16(mask):
    rows, q = mask.shape
    assert rows % 16 == 0 and rows // 16 <= 256
    ones = jnp.where(mask, jnp.ones((), BF16), jnp.zeros((), BF16))
    part = ones[0:16, :]
    for r in range(16, rows, 16):
        part = part + ones[r:r + 16, :]
    return jnp.sum(part.astype(F32), axis=0, keepdims=True)


def _kth_largest_rows(load_scores, coarse_ref, q_cols, k_sel):
    n = load_scores().shape[0]
    coarse_ref[0:n, :] = load_scores().astype(BF16)
    one = np.int32(1)

    def bf16_of(pattern):
        k = pattern - np.int32(0x8000)
        bits = k ^ ((k >> 15) & np.int32(0x7FFF))
        return lax.bitcast_convert_type(lax.shift_left(bits, 16), F32).astype(BF16)

    def coarse_bit(it, ans):
        cand = ans | lax.shift_left(one, 15 - it)
        return jnp.where(_count_rows16(coarse_ref[0:n, :] >= bf16_of(cand)) >= k_sel, cand, ans)

    coarse = lax.fori_loop(0, 16, coarse_bit, jnp.zeros((1, q_cols), I32))
    base = _score_to_key(bf16_of(coarse).astype(F32)) - np.int32(2 ** 15)

    def fine_bit(it, off):
        cand = off | lax.shift_left(one, 16 - it)
        return jnp.where(_count(load_scores() >= _key_to_score(base + cand), 0) >= k_sel, cand, off)

    return _key_to_score(base + lax.fori_loop(0, 17, fine_bit, jnp.zeros((1, q_cols), I32)))


def _topk_bias(store, load_keys, key_index, axis, vec_shape, k_sel, n_index_bits, coarse_ref=None,
               side_work=None):
    one = np.int32(1)
    if coarse_ref is None:
        thr = _kth_largest(load_keys, axis, vec_shape, k_sel, side_work)
    else:
        thr = _kth_largest_rows(load_keys, coarse_ref, vec_shape[1], k_sel)
    has_tie = jnp.max(_count(load_keys() >= thr, axis)) > k_sel

    @pl.when(jnp.logical_not(has_tie))
    def _():
        store(jnp.where(load_keys() >= thr, 0.0, NEG_INF))

    @pl.when(has_tie)
    def _():
        need = k_sel - _count(load_keys() > thr, axis)

        def index_bit(it, cut):
            cand = cut | lax.shift_left(one, n_index_bits - 1 - it)
            hit = jnp.where(load_keys() == thr, jnp.where(key_index() < cand, 1.0, 0.0), 0.0)
            return jnp.where(jnp.sum(hit, axis=axis, keepdims=True) < need, cand, cut)

        cut = lax.fori_loop(0, n_index_bits, index_bit, jnp.zeros(vec_shape, I32))
        keys = load_keys()
        keep_eq = jnp.where(key_index() <= cut, 0.0, NEG_INF)
        store(jnp.where(keys > thr, 0.0, jnp.where(keys == thr, keep_eq, NEG_INF)))


V_ROWS = HEAD_DIM + 16


def _pipelined_heads(n_chunks, chunk, head_logits, values_t, lgb, store_out):
    rows = lambda c: slice(c * chunk, (c + 1) * chunk)

    def logits_stage(h, slot):
        logits = head_logits(h)
        m = None
        for c in range(n_chunks):
            lg = logits(c)
            lgb[slot, rows(c), :] = lg
            cm = jnp.max(lg, axis=0, keepdims=True)
            m = cm if m is None else jnp.maximum(m, cm)
        return m

    def step(h, m_cur, cur, nxt):
        logits = head_logits(jnp.minimum(h + 1, N_HEADS - 1))
        acc = None
        m_next = None
        for c in range(n_chunks):
            lg = logits(c)
            nxt[rows(c), :] = lg
            cm = jnp.max(lg, axis=0, keepdims=True)
            m_next = cm if m_next is None else jnp.maximum(m_next, cm)
            p = jnp.exp2(cur[rows(c), :] - m_cur).astype(BF16)
            pv = _dot(values_t(h, c), p)
            acc = pv if acc is None else acc + pv
        store_out(h, acc[0:HEAD_DIM, :] / acc[HEAD_DIM:HEAD_DIM + 1, :])
        return m_next

    def body(t, m_cur):
        m_mid = step(2 * t, m_cur, lgb.at[0], lgb.at[1])
        return step(2 * t + 1, m_mid, lgb.at[1], lgb.at[0])

    lax.fori_loop(0, N_HEADS // 2, body, logits_stage(0, 0))


def _attn_prompt_kernel(qi_ref, qa_ref, qb_ref, ki_ref, ka_ref, kb_ref, va_ref, vb_ref, wit,
                        oa_ref, ob_ref,
                        kib, kab, kbb, vat, vbt, kmh, kml, sc, sc16, bias, lgb, ot):
    seq = ki_ref.shape[1]
    tq = MOBA_BLOCK
    n_blk = seq // tq

    kk = jnp.concatenate([ka_ref[...], ki_ref[...]], axis=0).T
    kab[...] = kk[:, 0:HEAD_DIM].astype(BF16)
    kib[...] = kk[:, HEAD_DIM:2 * HEAD_DIM].astype(BF16)
    kb = kb_ref[...].T
    ones = jnp.ones((V_ROWS - HEAD_DIM, seq), BF16)
    vat[...] = jnp.concatenate([va_ref[...].astype(BF16), ones], axis=0)
    means = jnp.concatenate(
        [jnp.mean(kb[j * tq:(j + 1) * tq, :], axis=0, keepdims=True) for j in range(n_blk)]
        + [jnp.zeros((kmh.shape[1] - n_blk, 2 * HEAD_DIM), F32)], axis=0)
    for n in range(HKV_B):
        kbb[n] = kb[:, n * HEAD_DIM:(n + 1) * HEAD_DIM].astype(BF16)
        vbt[n] = jnp.concatenate([vb_ref[n * HEAD_DIM:(n + 1) * HEAD_DIM, :].astype(BF16), ones], axis=0)
        hi, lo = _split_bf16(means[:, n * HEAD_DIM:(n + 1) * HEAD_DIM])
        kmh[n] = hi
        kml[n] = lo

    r_loc = lax.broadcasted_iota(I32, (tq, tq), 0)
    c_loc = lax.broadcasted_iota(I32, (tq, tq), 1)
    causal = r_loc <= c_loc

    for i in range(n_blk):
        c0 = i * tq
        lk = c0 + tq
        cols = slice(c0, c0 + tq)

        sc[0:lk, :] = jnp.zeros((lk, tq), F32)

        def idx_head(h, carry):
            off = pl.multiple_of(h * HEAD_DIM, HEAD_DIM)
            s = _dot(kib[0:lk, :], qi_ref[pl.ds(off, HEAD_DIM), cols])
            sc[0:lk, :] += jnp.maximum(s, 0.0) * wit[pl.ds(h, 1), cols]
            return carry

        lax.fori_loop(0, N_HEADS, idx_head, 0)

        if i == 0:
            bias[0:tq, :] = jnp.where(causal, 0.0, NEG_INF)
        else:
            sc[c0:lk, :] = jnp.where(causal, sc[c0:lk, :], NEG_INF)

            def store_bias(v):
                bias[0:lk, :] = v

            _topk_bias(store_bias, lambda: sc[0:lk, :], lambda: lax.broadcasted_iota(I32, (lk, tq), 0),
                       0, (1, tq), TOPK_A, (lk - 1).bit_length(), coarse_ref=sc16)

        blk = lambda c: slice(c * tq, (c + 1) * tq)

        def store_head(h, o):
            ot[pl.ds(pl.multiple_of(h * HEAD_DIM, HEAD_DIM), HEAD_DIM), :] = o

        def dsa_logits(h):
            q_t = qa_ref[pl.ds(pl.multiple_of(h * HEAD_DIM, HEAD_DIM), HEAD_DIM), cols]
            return lambda c: _dot(kab[blk(c), :], q_t) + bias[blk(c), :]

        _pipelined_heads(i + 1, tq, dsa_logits, lambda h, c: vat[:, blk(c)], lgb, store_head)
        oa_ref[cols, :] = ot[...].T.astype(BF16)

        def moba_logits(h):
            n = h // G_B
            q_t = qb_ref[pl.ds(pl.multiple_of(h * HEAD_DIM, HEAD_DIM), HEAD_DIM), cols]
            block_bias = [None] * i
            if i > TOPK_B:
                gate = _dot(kmh[n], q_t) + _dot(kml[n], q_t)
                rows = lax.broadcasted_iota(I32, gate.shape, 0)
                for j in range(i):
                    gj = gate[j:j + 1, :]
                    beats = jnp.where(rows < j, jnp.where(gate >= gj, 1, 0), jnp.where(gate > gj, 1, 0))
                    rank = jnp.sum(jnp.where(rows < i, beats, 0), axis=0, keepdims=True)
                    block_bias[j] = jnp.where(rank < TOPK_B, 0.0, NEG_INF)

            def logits(c):
                lg = _dot(kbb[n, blk(c), :], q_t)
                if c == i:
                    return jnp.where(causal, lg, NEG_INF)
                return lg if block_bias[c] is None else lg + block_bias[c]

            return logits

        _pipelined_heads(i + 1, tq, moba_logits, lambda h, c: vbt[h // G_B, :, blk(c)], lgb, store_head)
        ob_ref[cols, :] = ot[...].T.astype(BF16)


def _attn_prompt(qi_t, qa_t, qb_t, ki_t, ka_t, kb_t, va_t, vb_t, wi_t):
    n_seq, _, seq = qi_t.shape
    tq = MOBA_BLOCK
    fm = lambda w: pl.BlockSpec((None, w, seq), lambda b: (b, 0, 0))
    row = lambda w: pl.BlockSpec((seq, w), lambda b: (b, 0))
    return pl.pallas_call(
        _attn_prompt_kernel,
        grid=(n_seq,),
        in_specs=[fm(W_Q), fm(W_Q), fm(W_Q), fm(HEAD_DIM), fm(HEAD_DIM), fm(2 * HEAD_DIM),
                  fm(HEAD_DIM), fm(2 * HEAD_DIM), fm(N_HEADS)],
        out_specs=(row(W_Q), row(W_Q)),
        out_shape=(jax.ShapeDtypeStruct((n_seq * seq, W_Q), BF16),
                   jax.ShapeDtypeStruct((n_seq * seq, W_Q), BF16)),
        scratch_shapes=[
            pltpu.VMEM((seq, HEAD_DIM), BF16),
            pltpu.VMEM((seq, HEAD_DIM), BF16),
            pltpu.VMEM((HKV_B, seq, HEAD_DIM), BF16),
            pltpu.VMEM((V_ROWS, seq), BF16),
            pltpu.VMEM((HKV_B, V_ROWS, seq), BF16),
            pltpu.VMEM((HKV_B, 16, HEAD_DIM), BF16),
            pltpu.VMEM((HKV_B, 16, HEAD_DIM), BF16),
            pltpu.VMEM((seq, tq), F32),
            pltpu.VMEM((seq, tq), BF16),
            pltpu.VMEM((seq, tq), F32),
            pltpu.VMEM((2, seq, tq), F32),
            pltpu.VMEM((W_Q, tq), F32),
        ],
        compiler_params=pltpu.CompilerParams(
            dimension_semantics=("arbitrary",), vmem_limit_bytes=VMEM_LIMIT),
        name="attn_prompt",
    )(qi_t, qa_t, qb_t, ki_t, ka_t, kb_t, va_t, vb_t, wi_t)


ROWS = N_HEADS * 8
SAMPLE_GROUP = 4


def _attn_sample_kernel(pt_ref, cki_hbm, cka_hbm, cva_hbm, ckb_hbm, cvb_hbm,
                        nki_ref, nka_ref, nva_ref, nkb_ref, nvb_ref,
                        qi_ref, qa_ref, qb_ref, wc_ref,
                        oa_ref, ob_ref,
                        ski, ska, sva, skb, svb, scores, bias, sem,
                        *, group, n_new, past):
    g = pl.program_id(0)
    n_groups = pl.num_programs(0)
    n_pages = past // PAGE
    lpad = ski.shape[3]
    slot = g % 2
    pools = ((cki_hbm, ski), (cka_hbm, ska), (cva_hbm, sva), (ckb_hbm, skb), (cvb_hbm, svb))

    def page_copies(grp, to_slot, t):
        s = t // n_pages
        p = t % n_pages
        page = pt_ref[grp * group + s, p]
        lanes = pl.ds(pl.multiple_of(p * PAGE, PAGE), PAGE)
        return [pltpu.make_async_copy(hbm.at[page], slab.at[to_slot, s, :, lanes], sem.at[to_slot])
                for hbm, slab in pools]

    def start_group(grp, to_slot):
        def body(t, carry):
            for cp in page_copies(grp, to_slot, t):
                cp.start()
            return carry
        lax.fori_loop(0, group * n_pages, body, 0)

    def wait_group(to_slot):
        for _, slab in pools:
            filled = slab.at[to_slot, :, :, 0:past]
            pltpu.make_async_copy(filled, filled, sem.at[to_slot]).wait()

    @pl.when(g == 0)
    def _():
        start_group(0, 0)

    wait_group(slot)

    n_starts = group * n_pages
    assert n_starts % N_VALUE_PASSES == 0

    def prefetch_next(it):
        for k in range(n_starts // N_VALUE_PASSES):
            for cp in page_copies(g + 1, 1 - slot, it * (n_starts // N_VALUE_PASSES) + k):
                cp.start()

    n_blk = past // MOBA_BLOCK
    col = lax.broadcasted_iota(I32, (8, lpad), 1)
    qrow = lax.broadcasted_iota(I32, (8, lpad), 0)
    visible8 = col <= past + jnp.minimum(qrow, n_new - 1)
    for s in range(group):
        for slab, new in ((ski, nki_ref), (ska, nka_ref), (sva, nva_ref), (skb, nkb_ref), (svb, nvb_ref)):
            slab[slot, s, :, past:lpad] = jnp.zeros((slab.shape[2], lpad - past), F32)
            slab[slot, s, :, past:past + n_new] = new[s]
        st = _dot(qi_ref[s], ski[slot, s].astype(BF16))
        st = jnp.maximum(st, 0.0) * wc_ref[s]
        score = st[0:8, :]
        for h in range(1, N_HEADS):
            score = score + st[h * 8:(h + 1) * 8, :]
        scores[s * 8:(s + 1) * 8, :] = jnp.where(visible8, score, NEG_INF)

    rows = group * 8

    def store_bias(v):
        bias[...] = v

    for has_next, side_work in ((g + 1 < n_groups, prefetch_next), (g + 1 == n_groups, None)):
        @pl.when(has_next)
        def _():
            _topk_bias(store_bias, lambda: scores[...], lambda: lax.broadcasted_iota(I32, (rows, lpad), 1),
                       1, (rows, 1), TOPK_A, (lpad - 1).bit_length(), side_work=side_work)

    colr = lax.broadcasted_iota(I32, (ROWS, LANES), 1)
    qr = lax.broadcasted_iota(I32, (ROWS, LANES), 0) % 8
    visible_tail = colr <= jnp.minimum(qr, n_new - 1)
    for s in range(group):
        lg = _dot(qa_ref[s], ska[slot, s].astype(BF16)) + jnp.tile(bias[s * 8:(s + 1) * 8, :], (N_HEADS, 1))
        m = jnp.max(lg, axis=1, keepdims=True)
        p = jnp.exp2(lg - m)
        l = jnp.sum(p, axis=1, keepdims=True)
        oa_ref[s] = _dot_nt(p.astype(BF16), sva[slot, s].astype(BF16)) / l

        lgb = _dot(qb_ref[s], skb[slot, s].astype(BF16))
        gate = [jnp.sum(lgb[:, j * MOBA_BLOCK:(j + 1) * MOBA_BLOCK], axis=1, keepdims=True)
                for j in range(n_blk)]
        pieces = []
        for j in range(n_blk):
            rank = jnp.zeros((ROWS, 1), I32)
            for j2 in range(n_blk):
                if j2 != j:
                    rank = rank + jnp.where((gate[j2] >= gate[j]) if j2 < j else (gate[j2] > gate[j]), 1, 0)
            pieces.append(lgb[:, j * MOBA_BLOCK:(j + 1) * MOBA_BLOCK] + jnp.where(rank < TOPK_B, 0.0, NEG_INF))
        pieces.append(jnp.where(visible_tail, lgb[:, past:lpad], NEG_INF))
        lgm = jnp.concatenate(pieces, axis=1)
        m = jnp.max(lgm, axis=1, keepdims=True)
        p = jnp.exp2(lgm - m)
        l = jnp.sum(p, axis=1, keepdims=True)
        ob_ref[s] = _dot_nt(p.astype(BF16), svb[slot, s].astype(BF16)) / l


def _attn_sample(page_table, cki, cka, cva, ckb, cvb, nki, nka, nva, nkb, nvb, qi, qa, qb2, wcol, group):
    n_seq, n_pages = page_table.shape
    n_new = nki.shape[2]
    past = n_pages * PAGE
    lpad = past + LANES
    hbm = pl.BlockSpec(memory_space=pl.ANY)
    grp = lambda a, b: pl.BlockSpec((group, a, b), lambda g, pt: (g, 0, 0))
    slab = lambda w: pltpu.VMEM((2, group, w, lpad), F32)
    grid_spec = pltpu.PrefetchScalarGridSpec(
        num_scalar_prefetch=1,
        grid=(n_seq // group,),
        in_specs=[hbm, hbm, hbm, hbm, hbm,
                  grp(HEAD_DIM, n_new), grp(HEAD_DIM, n_new), grp(HEAD_DIM, n_new),
                  grp(2 * HEAD_DIM, n_new), grp(2 * HEAD_DIM, n_new),
                  grp(ROWS, HEAD_DIM), grp(ROWS, HEAD_DIM), grp(ROWS, 2 * HEAD_DIM), grp(ROWS, 1)],
        out_specs=(grp(ROWS, HEAD_DIM), grp(ROWS, 2 * HEAD_DIM)),
        scratch_shapes=[
            slab(HEAD_DIM),
            slab(HEAD_DIM),
            slab(HEAD_DIM),
            slab(2 * HEAD_DIM),
            slab(2 * HEAD_DIM),
            pltpu.VMEM((group * 8, lpad), F32),
            pltpu.VMEM((group * 8, lpad), F32),
            pltpu.SemaphoreType.DMA((2,)),
        ],
    )
    return pl.pallas_call(
        functools.partial(_attn_sample_kernel, group=group, n_new=n_new, past=past),
        grid_spec=grid_spec,
        out_shape=(jax.ShapeDtypeStruct((n_seq, ROWS, HEAD_DIM), F32),
                   jax.ShapeDtypeStruct((n_seq, ROWS, 2 * HEAD_DIM), F32)),
        compiler_params=pltpu.CompilerParams(
            dimension_semantics=("arbitrary",), vmem_limit_bytes=VMEM_LIMIT),
        name="attn_sample",
    )(page_table, cki, cka, cva, ckb, cvb, nki, nka, nva, nkb, nvb, qi, qa, qb2, wcol)


def _merge_ffn_kernel(x_ref, oa_ref, ob_ref, sga_ref, sgb_ref, wa_ref, wb_ref, wo_ref,
                      g_ref, wg_ref, wu_ref, wf_ref, o_ref):
    m = sga_ref[...] * _dot(oa_ref[...], wa_ref[...]) + sgb_ref[...] * _dot(ob_ref[...], wb_ref[...])
    x = x_ref[...] + _dot(m.astype(BF16), wo_ref[...])
    o_ref[...] = _half_swiglu(x, g_ref, wg_ref, wu_ref, wf_ref)


def _merge_ffn(x, oa, ob, sga, sgb, wa, wb, wo, g, wg, wu, wf, tm):
    t = x.shape[0]
    row = lambda w: pl.BlockSpec((tm, w), lambda i: (i, 0))
    return pl.pallas_call(
        _merge_ffn_kernel,
        grid=(t // tm,),
        in_specs=[row(D_MODEL), row(W_Q), row(W_Q), row(D_MODEL), row(D_MODEL),
                  _const_spec((W_Q, D_MODEL)), _const_spec((W_Q, D_MODEL)), _const_spec((D_MODEL, D_MODEL)),
                  _const_spec((1, D_MODEL)), _const_spec((D_MODEL, D_FF_PAD)), _const_spec((D_MODEL, D_FF_PAD)),
                  _const_spec((D_FF_PAD, D_MODEL))],
        out_specs=row(D_MODEL),
        out_shape=jax.ShapeDtypeStruct((t, D_MODEL), F32),
        compiler_params=pltpu.CompilerParams(
            dimension_semantics=("arbitrary",), vmem_limit_bytes=VMEM_LIMIT),
        name="merge_ffn",
    )(x, oa, ob, sga, sgb, wa, wb, wo, g, wg, wu, wf)


def _ffn_weights(w_in, w_out):
    pad = D_FF_PAD - D_FF
    wg = jnp.pad(w_in[:, :D_FF], ((0, 0), (0, pad))).astype(BF16)
    wu = jnp.pad(w_in[:, D_FF:], ((0, 0), (0, pad))).astype(BF16)
    wo = jnp.pad(w_out, ((0, pad), (0, 0))).astype(BF16)
    return wg, wu, wo


def _rope_tables(pos):
    half = HEAD_DIM // 2
    inv = ROPE_THETA ** (-jnp.arange(half, dtype=F32) / half)
    ang = pos.astype(F32)[:, None] * inv[None, :]
    cos, sin = jnp.cos(ang), jnp.sin(ang)
    return (jnp.concatenate([cos, cos, cos, cos], axis=1),
            jnp.concatenate([-sin, sin, -sin, sin], axis=1))


def _head_rows(a, n_seq, n_new):
    w = a.shape[1] // N_HEADS
    a = a.reshape(n_seq, n_new, N_HEADS, w).transpose(0, 2, 1, 3)
    a = jnp.pad(a, ((0, 0), (0, 0), (0, 8 - n_new), (0, 0)))
    return a.reshape(n_seq, ROWS, w)


def _from_head_rows(a, n_seq, n_new):
    w = a.shape[2]
    a = a.reshape(n_seq, N_HEADS, 8, w)[:, :, :n_new].transpose(0, 2, 1, 3)
    return a.reshape(n_seq * n_new, N_HEADS * w)


def kernel(x_prompt, x_sample, cache_k_a, cache_v_a, cache_kidx_a, cache_k_b, cache_v_b, page_table,
           ffn1_norm, ffn1_w_in, ffn1_w_out, mix_norm, w_in, q_norm_a, k_norm_a, q_norm_b, k_norm_b,
           gate_bias, w_up_a, w_up_b, w_out, ffn2_norm, ffn2_w_in, ffn2_w_out):
    n_seq, seq, _ = x_prompt.shape
    n_dec, n_new, _ = x_sample.shape
    depth = ffn1_norm.shape[0]
    n_pool = cache_k_a.shape[1]
    past = page_table.shape[1] * PAGE
    assert depth == 1 and seq % MOBA_BLOCK == 0 and past % MOBA_BLOCK == 0 and n_new <= 8
    assert seq >= 4 * TOPK_A and past + n_new >= 4 * TOPK_A
    l = 0
    tp, ts = n_seq * seq, n_dec * n_new

    xp = x_prompt.reshape(tp, D_MODEL)
    xs = x_sample.reshape(ts, D_MODEL)

    f1 = _ffn_weights(ffn1_w_in[l], ffn1_w_out[l])
    f2 = _ffn_weights(ffn2_w_in[l], ffn2_w_out[l])
    w = w_in[l]
    o = np.cumsum([0, W_Q, HEAD_DIM, HEAD_DIM, W_Q, HEAD_DIM, N_HEADS, W_Q, 2 * HEAD_DIM, 2 * HEAD_DIM,
                   D_MODEL, D_MODEL])
    c = lambda k: w[:, o[k]:o[k + 1]]
    z = lambda n: jnp.zeros((D_MODEL, n), F32)
    wp = jnp.concatenate([c(0), c(6), c(7), c(1), z(64), c(3), c(4), z(64), c(8), c(2), c(5), z(56),
                          c(9), c(10)], axis=1).astype(BF16)
    gain = jnp.concatenate([jnp.tile(q_norm_a[l], N_HEADS), jnp.tile(q_norm_b[l], N_HEADS),
                            jnp.tile(k_norm_b[l], HKV_B), k_norm_a[l], jnp.zeros((64,), F32)])[None, :]
    bias = gate_bias[l][None, :]
    lane = np.arange(2 * LANES)
    bd =jnp.asarray((lane[:, None] // HEAD_DIM) == (lane[None, :] // HEAD_DIM), BF16)
    cos_p, sin_p = _rope_tables(jnp.arange(seq, dtype=I32))
    cos_s, sin_s = _rope_tables(past + (jnp.arange(ts, dtype=I32) % n_new))
    wa, wb, wo = w_up_a[l].astype(BF16), w_up_b[l].astype(BF16), w_out[l].astype(BF16)
    g1, gm, g2 = ffn1_norm[l][None, :], mix_norm[l][None, :], ffn2_norm[l][None, :]

    tm = 512
    xp1 = _ffn(xp, g1, *f1, tm)
    (qa_t, qb_t, qi_t, kb_t, ka_t, ki_t, vb_t, va_t, wi_t, sga_p, sgb_p) = _proj(
        xp1, gm, wp, gain, bias, cos_p, sin_p, bd, tm, seq)
    oa_p, ob_p = _attn_prompt(qi_t, qa_t, qb_t, ki_t, ka_t, kb_t, va_t, vb_t, wi_t)
    yp = _merge_ffn(xp1, oa_p, ob_p, sga_p, sgb_p, wa, wb, wo, g2, *f2, tm)

    xs1 = _ffn(xs, g1, *f1, ts)
    (qa_s, qb_s, qi_s, kb_s, ka_s, ki_s, vb_s, va_s, wi_s, sga_s, sgb_s) = _proj(
        xs1, gm, wp, gain, bias, cos_s, sin_s, bd, ts, None)
    qi_r = _head_rows(qi_s, n_dec, n_new)
    qa_r = _head_rows(qa_s, n_dec, n_new)
    qb_r = _head_rows(qb_s, n_dec, n_new)
    zq = jnp.zeros_like(qb_r)
    first = (jnp.arange(ROWS) < ROWS // HKV_B)[None, :, None]
    qb2 = jnp.concatenate([jnp.where(first, qb_r, zq), jnp.where(first, zq, qb_r)], axis=2)
    wcol = _head_rows(wi_s, n_dec, n_new)
    pages_t = lambda a: jnp.swapaxes(a[l].reshape(n_pool, PAGE, -1), 1, 2)
    new_t = lambda a: jnp.swapaxes(a.reshape(n_dec, n_new, a.shape[1]), 1, 2)
    oa_r, ob_r = _attn_sample(
        page_table,
        pages_t(cache_kidx_a), pages_t(cache_k_a), pages_t(cache_v_a), pages_t(cache_k_b), pages_t(cache_v_b),
        new_t(ki_s), new_t(ka_s), new_t(va_s), new_t(kb_s), new_t(vb_s), qi_r, qa_r, qb2, wcol, SAMPLE_GROUP)
    oa_s = _from_head_rows(oa_r, n_dec, n_new).astype(BF16)
    ob_sel = jnp.where(first, ob_r[:, :, :HEAD_DIM], ob_r[:, :, HEAD_DIM:])
    ob_s = _from_head_rows(ob_sel, n_dec, n_new).astype(BF16)
    ys = _merge_ffn(xs1, oa_s, ob_s, sga_s, sgb_s, wa, wb, wo, g2, *f2, ts)

    d = depth
    tok = lambda a: jnp.swapaxes(a, 1, 2)
    return (yp.reshape(n_seq, seq, D_MODEL), ys.reshape(n_dec, n_new, D_MODEL),
            tok(ka_t).reshape(d, n_seq, seq, 1, HEAD_DIM), tok(va_t).reshape(d, n_seq, seq, 1, HEAD_DIM),
            tok(ki_t).reshape(d, n_seq, seq, HEAD_DIM),
            tok(kb_t).reshape(d, n_seq, seq, HKV_B, HEAD_DIM), tok(vb_t).reshape(d, n_seq, seq, HKV_B, HEAD_DIM),
            ka_s.reshape(d, n_dec, n_new, 1, HEAD_DIM), va_s.reshape(d, n_dec, n_new, 1, HEAD_DIM),
            ki_s.reshape(d, n_dec, n_new, HEAD_DIM),
            kb_s.reshape(d, n_dec, n_new, HKV_B, HEAD_DIM), vb_s.reshape(d, n_dec, n_new, HKV_B, HEAD_DIM))
```

```python
import functools

import numpy as np
import jax
import jax.numpy as jnp
from jax import lax
from jax.experimental import pallas as pl
from jax.experimental.pallas import tpu as pltpu

F32 = jnp.float32
BF16 = jnp.bfloat16
I32 = jnp.int32

D_MODEL = 1024
HEAD_DIM = 64
N_HEADS = 8
HKV_B = 2
G_B = N_HEADS // HKV_B
D_FF = 2752
TOPK_A = 256
MOBA_BLOCK = 256
TOPK_B = 3
PAGE = 128
ROPE_THETA = 10000.0
EPS = 1e-6

LANES = 128
D_FF_PAD = 2816
FF_CHUNK = 512
W_Q = N_HEADS * HEAD_DIM

A_W = 1280
B_W = 640
C_W = 256
G_W = 2 * D_MODEL
P_W = A_W + B_W + C_W + G_W

LOG2E = 1.4426950408889634
INT_MIN = np.int32(-2 ** 31)
NEG_INF = float("-inf")
VMEM_LIMIT = 56 * 1024 * 1024


def _dot(a, b):
    return jnp.dot(a, b, preferred_element_type=F32)


def _dot_nt(a, b):
    return lax.dot_general(a, b, (((1,), (1,)), ((), ())), preferred_element_type=F32)


def _split_bf16(x):
    hi = x.astype(BF16)
    lo = (x - hi.astype(F32)).astype(BF16)
    return hi, lo


def _rms(x, g):
    ms = jnp.mean(x * x, axis=-1, keepdims=True)
    return x * lax.rsqrt(ms + EPS) * g


def _const_spec(shape):
    nd = len(shape)
    return pl.BlockSpec(shape, lambda *_: (0,) * nd, pipeline_mode=pl.Buffered(1))


def _half_swiglu(x, g_ref, wg_ref, wu_ref, wo_ref):
    h = _rms(x, g_ref[...]).astype(BF16)
    acc = jnp.zeros_like(x)
    for s in range(0, D_FF_PAD, FF_CHUNK):
        e = min(s + FF_CHUNK, D_FF_PAD)
        g = _dot(h, wg_ref[:, s:e])
        u = _dot(h, wu_ref[:, s:e])
        a = (g * jax.nn.sigmoid(g) * u).astype(BF16)
        acc = acc + _dot(a, wo_ref[s:e, :])
    return x + 0.5 * acc


def _ffn_kernel(x_ref, g_ref, wg_ref, wu_ref, wo_ref, o_ref):
    o_ref[...] = _half_swiglu(x_ref[...], g_ref, wg_ref, wu_ref, wo_ref)


def _ffn(x, g, wg, wu, wo, tm):
    t = x.shape[0]
    return pl.pallas_call(
        _ffn_kernel,
        grid=(t // tm,),
        in_specs=[
            pl.BlockSpec((tm, D_MODEL), lambda i: (i, 0)),
            _const_spec((1, D_MODEL)),
            _const_spec((D_MODEL, D_FF_PAD)),
            _const_spec((D_MODEL, D_FF_PAD)),
            _const_spec((D_FF_PAD, D_MODEL)),
        ],
        out_specs=pl.BlockSpec((tm, D_MODEL), lambda i: (i, 0)),
        out_shape=jax.ShapeDtypeStruct((t, D_MODEL), F32),
        compiler_params=pltpu.CompilerParams(
            dimension_semantics=("arbitrary",), vmem_limit_bytes=VMEM_LIMIT),
        name="ffn",
    )(x, g, wg, wu, wo)


def _rope(y, cos, sin, first_half):
    r_lo = pltpu.roll(y, 32, 1)
    r_hi = pltpu.roll(y, 96, 1)
    return y * cos + jnp.where(first_half, r_hi, r_lo) * sin


def _proj_kernel(x_ref, gm_ref, wp_ref, gain_ref, bias_ref, cos_ref, sin_ref, bd_ref,
                 qa_ref, qb_ref, qi_ref, kb_ref, ka_ref, ki_ref, vb_ref, va_ref, wi_ref,
                 sga_ref, sgb_ref, *, transposed):
    x = x_ref[...]
    tm = x.shape[0]
    h = _rms(x, gm_ref[...]).astype(BF16)
    cos = cos_ref[...]
    sin = sin_ref[...]
    bd = bd_ref[...]
    lane = lax.broadcasted_iota(I32, (tm, LANES), 1)
    first_half = (lane % HEAD_DIM) < (HEAD_DIM // 2)

    z_a = _dot(h, wp_ref[:, 0:A_W])
    tiles_a = []
    for j in range(0, A_W // LANES, 2):
        z2 = z_a[:, j * LANES:(j + 2) * LANES]
        hi, lo = _split_bf16(z2 * z2)
        ms = (_dot(hi, bd) + _dot(lo, bd)) * (1.0 / HEAD_DIM)
        y2 = z2 * lax.rsqrt(ms + EPS) * gain_ref[:, j * LANES:(j + 2) * LANES]
        for t in range(2):
            tiles_a.append(_rope(y2[:, t * LANES:(t + 1) * LANES], cos, sin, first_half))

    z_b = _dot(h, wp_ref[:, A_W:A_W + B_W])
    tiles_b = [_rope(z_b[:, j * LANES:(j + 1) * LANES], cos, sin, first_half)
               for j in range(B_W // LANES)]

    z_c = _dot(h, wp_ref[:, A_W + B_W:A_W + B_W + C_W])
    z_g = _dot(h, wp_ref[:, A_W + B_W + C_W:P_W]) + bias_ref[...]

    scale = HEAD_DIM ** -0.5
    qa = jnp.concatenate(tiles_a[0:4], axis=1) * (scale * LOG2E)
    qb = jnp.concatenate(tiles_a[4:8], axis=1) * (scale * LOG2E)
    qi = jnp.concatenate(tiles_b[0:4], axis=1) * scale
    wi_scale = N_HEADS ** -0.5
    if transposed:
        qa_ref[...] = qa.T.astype(BF16)
        qb_ref[...] = qb.T.astype(BF16)
        qi_ref[...] = qi.T.astype(BF16)
        kb_ref[...] = tiles_a[8].T
        ka_ref[...] = tiles_a[9].T[0:HEAD_DIM, :]
        ki_ref[...] = tiles_b[4].T[0:HEAD_DIM, :]
        zc_t = z_c.T
        vb_ref[...] = zc_t[0:LANES, :]
        va_ref[...] = zc_t[LANES:LANES + HEAD_DIM, :]
        wi_ref[...] = zc_t[LANES + HEAD_DIM:LANES + HEAD_DIM + N_HEADS, :] * wi_scale
    else:
        qa_ref[...] = qa.astype(BF16)
        qb_ref[...] = qb.astype(BF16)
        qi_ref[...] = qi.astype(BF16)
        kb_ref[...] = tiles_a[8]
        ka_ref[...] = tiles_a[9][:, 0:HEAD_DIM]
        ki_ref[...] = tiles_b[4][:, 0:HEAD_DIM]
        vb_ref[...] = z_c[:, 0:LANES]
        va_ref[...] = z_c[:, LANES:LANES + HEAD_DIM]
        wi_ref[...] = z_c[:, LANES + HEAD_DIM:LANES + HEAD_DIM + N_HEADS] * wi_scale
    sg = jax.nn.sigmoid(z_g)
    sga_ref[...] = sg[:, 0:D_MODEL]
    sgb_ref[...] = sg[:, D_MODEL:G_W]


def _proj(x, gm, wp, gain, bias, cos_t, sin_t, bd, tm, seq):
    t = x.shape[0]
    n_tab = cos_t.shape[0] // tm
    row = lambda w: pl.BlockSpec((tm, w), lambda i: (i, 0))
    widths = (W_Q, W_Q, W_Q, 2 * HEAD_DIM, HEAD_DIM, HEAD_DIM, 2 * HEAD_DIM, HEAD_DIM, N_HEADS)
    dtypes = (BF16, BF16, BF16, F32, F32, F32, F32, F32, F32)
    if seq is None:
        specs = [row(w) for w in widths]
        shapes = [jax.ShapeDtypeStruct((t, w), d) for w, d in zip(widths, dtypes)]
    else:
        nb = seq // tm
        specs = [pl.BlockSpec((None, w, tm), lambda i: (i // nb, 0, i % nb)) for w in widths]
        shapes = [jax.ShapeDtypeStruct((t // seq, w, seq), d) for w, d in zip(widths, dtypes)]
    out_specs = tuple(specs) + (row(D_MODEL), row(D_MODEL))
    out_shape = tuple(shapes) + (jax.ShapeDtypeStruct((t, D_MODEL), F32), jax.ShapeDtypeStruct((t, D_MODEL), F32))
    transposed = seq is not None
    return pl.pallas_call(
        functools.partial(_proj_kernel, transposed=transposed),
        grid=(t // tm,),
        in_specs=[
            row(D_MODEL),
            _const_spec((1, D_MODEL)),
            _const_spec((D_MODEL, P_W)),
            _const_spec((1, A_W)),
            _const_spec((1, G_W)),
            pl.BlockSpec((tm, LANES), lambda i: (i % n_tab, 0)),
            pl.BlockSpec((tm, LANES), lambda i: (i % n_tab, 0)),
            _const_spec((2 * LANES, 2 * LANES)),
        ],
        out_specs=out_specs,
        out_shape=out_shape,
        compiler_params=pltpu.CompilerParams(
            dimension_semantics=("arbitrary",), vmem_limit_bytes=VMEM_LIMIT),
        name="proj",
    )(x, gm, wp, gain, bias, cos_t, sin_t, bd)


def _count(mask, axis):
    return jnp.sum(jnp.where(mask, 1.0, 0.0), axis=axis, keepdims=True)


def _key_to_score(key):
    return lax.bitcast_convert_type(key ^ ((key >> 31) & np.int32(0x7FFFFFFF)), F32)


N_VALUE_PASSES = 16


def _kth_largest(load_scores, axis, vec_shape, k_sel, side_work=None):
    def value_bits(it, ans):
        if side_work is not None:
            side_work(it)
        shift = 30 - 2 * it
        best = ans
        for digit in (1, 2, 3):
            cand = ans | lax.shift_left(np.int32(digit), shift)
            thr = _key_to_score(cand ^ INT_MIN)
            best = jnp.where(_count(load_scores() >= thr, axis) >= k_sel, cand, best)
        return best

    return _key_to_score(lax.fori_loop(0, N_VALUE_PASSES, value_bits, jnp.zeros(vec_shape, I32)) ^ INT_MIN)


def _score_to_key(score):
    b = lax.bitcast_convert_type(score, I32)
    return b ^ ((b >> 31) & np.int32(0x7FFFFFFF))


def _count_rows16(mask):
    rows, q = mask.shape
    assert rows % 16 == 0 and rows // 16 <= 256
    ones = jnp.where(mask, jnp.ones((), BF16), jnp.zeros((), BF16))
    part = ones[0:16, :]
    for r in range(16, rows, 16):
        part = part + ones[r:r + 16, :]
    return jnp.sum(part.astype(F32), axis=0, keepdims=True)


def _kth_largest_rows(load_scores, coarse_ref, q_cols, k_sel):
    n = load_scores().shape[0]
    coarse_ref[0:n, :] = load_scores().astype(BF16)
    one = np.int32(1)

    def bf16_of(pattern):
        k = pattern - np.int32(0x8000)
        bits = k ^ ((k >> 15) & np.int32(0x7FFF))
        return lax.bitcast_convert_type(lax.shift_left(bits, 16), F32).astype(BF16)

    def coarse_bit(it, ans):
        cand = ans | lax.shift_left(one, 15 - it)
        return jnp.where(_count_rows16(coarse_ref[0:n, :] >= bf16_of(cand)) >= k_sel, cand, ans)

    coarse = lax.fori_loop(0, 16, coarse_bit, jnp.zeros((1, q_cols), I32))
    base = _score_to_key(bf16_of(coarse).astype(F32)) - np.int32(2 ** 15)

    def fine_bit(it, off):
        cand = off | lax.shift_left(one, 16 - it)
        return jnp.where(_count(load_scores() >= _key_to_score(base + cand), 0) >= k_sel, cand, off)

    return _key_to_score(base + lax.fori_loop(0, 17, fine_bit, jnp.zeros((1, q_cols), I32)))


def _topk_bias(store, load_keys, key_index, axis, vec_shape, k_sel, n_index_bits, coarse_ref=None,
               side_work=None):
    one = np.int32(1)
    if coarse_ref is None:
        thr = _kth_largest(load_keys, axis, vec_shape, k_sel, side_work)
    else:
        thr = _kth_largest_rows(load_keys, coarse_ref, vec_shape[1], k_sel)
    has_tie = jnp.max(_count(load_keys() >= thr, axis)) > k_sel

    @pl.when(jnp.logical_not(has_tie))
    def _():
        store(jnp.where(load_keys() >= thr, 0.0, NEG_INF))

    @pl.when(has_tie)
    def _():
        need = k_sel - _count(load_keys() > thr, axis)

        def index_bit(it, cut):
            cand = cut | lax.shift_left(one, n_index_bits - 1 - it)
            hit = jnp.where(load_keys() == thr, jnp.where(key_index() < cand, 1.0, 0.0), 0.0)
            return jnp.where(jnp.sum(hit, axis=axis, keepdims=True) < need, cand, cut)

        cut = lax.fori_loop(0, n_index_bits, index_bit, jnp.zeros(vec_shape, I32))
        keys = load_keys()
        keep_eq = jnp.where(key_index() <= cut, 0.0, NEG_INF)
        store(jnp.where(keys > thr, 0.0, jnp.where(keys == thr, keep_eq, NEG_INF)))


V_ROWS = HEAD_DIM + 16


def _pipelined_heads(n_chunks, chunk, head_logits, values_t, lgb, store_out):
    rows = lambda c: slice(c * chunk, (c + 1) * chunk)

    def logits_stage(h, slot):
        logits = head_logits(h)
        m = None
        for c in range(n_chunks):
            lg = logits(c)
            lgb[slot, rows(c), :] = lg
            cm = jnp.max(lg, axis=0, keepdims=True)
            m = cm if m is None else jnp.maximum(m, cm)
        return m

    def step(h, m_cur, cur, nxt):
        logits = head_logits(jnp.minimum(h + 1, N_HEADS - 1))
        acc = None
        m_next = None
        for c in range(n_chunks):
            lg = logits(c)
            nxt[rows(c), :] = lg
            cm = jnp.max(lg, axis=0, keepdims=True)
            m_next = cm if m_next is None else jnp.maximum(m_next, cm)
            p = jnp.exp2(cur[rows(c), :] - m_cur).astype(BF16)
            pv = _dot(values_t(h, c), p)
            acc = pv if acc is None else acc + pv
        store_out(h, acc[0:HEAD_DIM, :] / acc[HEAD_DIM:HEAD_DIM + 1, :])
        return m_next

    def body(t, m_cur):
        m_mid = step(2 * t, m_cur, lgb.at[0], lgb.at[1])
        return step(2 * t + 1, m_mid, lgb.at[1], lgb.at[0])

    lax.fori_loop(0, N_HEADS // 2, body, logits_stage(0, 0))


def _attn_prompt_kernel(qi_ref, qa_ref, qb_ref, ki_ref, ka_ref, kb_ref, va_ref, vb_ref, wit,
                        oa_ref, ob_ref,
                        kib, kab, kbb, vat, vbt, kmh, kml, sc, sc16, bias, lgb, ot):
    seq = ki_ref.shape[1]
    tq = MOBA_BLOCK
    n_blk = seq // tq

    kk = jnp.concatenate([ka_ref[...], ki_ref[...]], axis=0).T
    kab[...] = kk[:, 0:HEAD_DIM].astype(BF16)
    kib[...] = kk[:, HEAD_DIM:2 * HEAD_DIM].astype(BF16)
    kb = kb_ref[...].T
    ones = jnp.ones((V_ROWS - HEAD_DIM, seq), BF16)
    vat[...] = jnp.concatenate([va_ref[...].astype(BF16), ones], axis=0)
    means = jnp.concatenate(
        [jnp.mean(kb[j * tq:(j + 1) * tq, :], axis=0, keepdims=True) for j in range(n_blk)]
        + [jnp.zeros((kmh.shape[1] - n_blk, 2 * HEAD_DIM), F32)], axis=0)
    for n in range(HKV_B):
        kbb[n] = kb[:, n * HEAD_DIM:(n + 1) * HEAD_DIM].astype(BF16)
        vbt[n] = jnp.concatenate([vb_ref[n * HEAD_DIM:(n + 1) * HEAD_DIM, :].astype(BF16), ones], axis=0)
        hi, lo = _split_bf16(means[:, n * HEAD_DIM:(n + 1) * HEAD_DIM])
        kmh[n] = hi
        kml[n] = lo

    r_loc = lax.broadcasted_iota(I32, (tq, tq), 0)
    c_loc = lax.broadcasted_iota(I32, (tq, tq), 1)
    causal = r_loc <= c_loc

    for i in range(n_blk):
        c0 = i * tq
        lk = c0 + tq
        cols = slice(c0, c0 + tq)

        sc[0:lk, :] = jnp.zeros((lk, tq), F32)

        def idx_head(h, carry):
            off = pl.multiple_of(h * HEAD_DIM, HEAD_DIM)
            s = _dot(kib[0:lk, :], qi_ref[pl.ds(off, HEAD_DIM), cols])
            sc[0:lk, :] += jnp.maximum(s, 0.0) * wit[pl.ds(h, 1), cols]
            return carry

        lax.fori_loop(0, N_HEADS, idx_head, 0)

        if i == 0:
            bias[0:tq, :] = jnp.where(causal, 0.0, NEG_INF)
        else:
            sc[c0:lk, :] = jnp.where(causal, sc[c0:lk, :], NEG_INF)

            def store_bias(v):
                bias[0:lk, :] = v

            _topk_bias(store_bias, lambda: sc[0:lk, :], lambda: lax.broadcasted_iota(I32, (lk, tq), 0),
                       0, (1, tq), TOPK_A, (lk - 1).bit_length(), coarse_ref=sc16)

        blk = lambda c: slice(c * tq, (c + 1) * tq)

        def store_head(h, o):
            ot[pl.ds(pl.multiple_of(h * HEAD_DIM, HEAD_DIM), HEAD_DIM), :] = o

        def dsa_logits(h):
            q_t = qa_ref[pl.ds(pl.multiple_of(h * HEAD_DIM, HEAD_DIM), HEAD_DIM), cols]
            return lambda c: _dot(kab[blk(c), :], q_t) + bias[blk(c), :]

        _pipelined_heads(i + 1, tq, dsa_logits, lambda h, c: vat[:, blk(c)], lgb, store_head)
        oa_ref[cols, :] = ot[...].T.astype(BF16)

        def moba_logits(h):
            n = h // G_B
            q_t = qb_ref[pl.ds(pl.multiple_of(h * HEAD_DIM, HEAD_DIM), HEAD_DIM), cols]
            block_bias = [None] * i
            if i > TOPK_B:
                gate = _dot(kmh[n], q_t) + _dot(kml[n], q_t)
                rows = lax.broadcasted_iota(I32, gate.shape, 0)
                for j in range(i):
                    gj = gate[j:j + 1, :]
                    beats = jnp.where(rows < j, jnp.where(gate >= gj, 1, 0), jnp.where(gate > gj, 1, 0))
                    rank = jnp.sum(jnp.where(rows < i, beats, 0), axis=0, keepdims=True)
                    block_bias[j] = jnp.where(rank < TOPK_B, 0.0, NEG_INF)

            def logits(c):
                lg = _dot(kbb[n, blk(c), :], q_t)
                if c == i:
                    return jnp.where(causal, lg, NEG_INF)
                return lg if block_bias[c] is None else lg + block_bias[c]

            return logits

        _pipelined_heads(i + 1, tq, moba_logits, lambda h, c: vbt[h // G_B, :, blk(c)], lgb, store_head)
        ob_ref[cols, :] = ot[...].T.astype(BF16)


def _attn_prompt(qi_t, qa_t, qb_t, ki_t, ka_t, kb_t, va_t, vb_t, wi_t):
    n_seq, _, seq = qi_t.shape
    tq = MOBA_BLOCK
    fm = lambda w: pl.BlockSpec((None, w, seq), lambda b: (b, 0, 0))
    row = lambda w: pl.BlockSpec((seq, w), lambda b: (b, 0))
    return pl.pallas_call(
        _attn_prompt_kernel,
        grid=(n_seq,),
        in_specs=[fm(W_Q), fm(W_Q), fm(W_Q), fm(HEAD_DIM), fm(HEAD_DIM), fm(2 * HEAD_DIM),
                  fm(HEAD_DIM), fm(2 * HEAD_DIM), fm(N_HEADS)],
        out_specs=(row(W_Q), row(W_Q)),
        out_shape=(jax.ShapeDtypeStruct((n_seq * seq, W_Q), BF16),
                   jax.ShapeDtypeStruct((n_seq * seq, W_Q), BF16)),
        scratch_shapes=[
            pltpu.VMEM((seq, HEAD_DIM), BF16),
            pltpu.VMEM((seq, HEAD_DIM), BF16),
            pltpu.VMEM((HKV_B, seq, HEAD_DIM), BF16),
            pltpu.VMEM((V_ROWS, seq), BF16),
            pltpu.VMEM((HKV_B, V_ROWS, seq), BF16),
            pltpu.VMEM((HKV_B, 16, HEAD_DIM), BF16),
            pltpu.VMEM((HKV_B, 16, HEAD_DIM), BF16),
            pltpu.VMEM((seq, tq), F32),
            pltpu.VMEM((seq, tq), BF16),
            pltpu.VMEM((seq, tq), F32),
            pltpu.VMEM((2, seq, tq), F32),
            pltpu.VMEM((W_Q, tq), F32),
        ],
        compiler_params=pltpu.CompilerParams(
            dimension_semantics=("arbitrary",), vmem_limit_bytes=VMEM_LIMIT),
        name="attn_prompt",
    )(qi_t, qa_t, qb_t, ki_t, ka_t, kb_t, va_t, vb_t, wi_t)


SAMPLE_GROUP = 4


def _attn_sample_kernel(pt_ref, cki_hbm, cka_hbm, cva_hbm, ckb_hbm, cvb_hbm,
                        nki_ref, nka_ref, nva_ref, nkb_ref, nvb_ref,
                        qi_ref, qa_ref, qb_ref, wc_ref,
                        oa_ref, ob_ref,
                        ski, ska, sva, skb, svb, scores, bias, sem,
                        *, group, n_new, past):
    g = pl.program_id(0)
    n_groups = pl.num_programs(0)
    n_pages = past // PAGE
    lpad = ski.shape[3]
    slot = g % 2
    pools = ((cki_hbm, ski), (cka_hbm, ska), (cva_hbm, sva), (ckb_hbm, skb), (cvb_hbm, svb))

    def page_copies(grp, to_slot, t):
        s = t // n_pages
        p = t % n_pages
        page = pt_ref[grp * group + s, p]
        lanes = pl.ds(pl.multiple_of(p * PAGE, PAGE), PAGE)
        return [pltpu.make_async_copy(hbm.at[page], slab.at[to_slot, s, :, lanes], sem.at[to_slot])
                for hbm, slab in pools]

    def start_group(grp, to_slot):
        def body(t, carry):
            for cp in page_copies(grp, to_slot, t):
                cp.start()
            return carry
        lax.fori_loop(0, group * n_pages, body, 0)

    def wait_group(to_slot):
        for _, slab in pools:
            filled = slab.at[to_slot, :, :, 0:past]
            pltpu.make_async_copy(filled, filled, sem.at[to_slot]).wait()

    @pl.when(g == 0)
    def _():
        start_group(0, 0)

    wait_group(slot)

    n_starts = group * n_pages
    assert n_starts % N_VALUE_PASSES == 0

    def prefetch_next(it):
        for k in range(n_starts // N_VALUE_PASSES):
            for cp in page_copies(g + 1, 1 - slot, it * (n_starts // N_VALUE_PASSES) + k):
                cp.start()

    n_blk = past // MOBA_BLOCK
    col = lax.broadcasted_iota(I32, (8, lpad), 1)
    qrow = lax.broadcasted_iota(I32, (8, lpad), 0)
    visible8 = col <= past + jnp.minimum(qrow, n_new - 1)
    for s in range(group):
        for slab, new in ((ski, nki_ref), (ska, nka_ref), (sva, nva_ref), (skb, nkb_ref), (svb, nvb_ref)):
            slab[slot, s, :, past:lpad] = jnp.zeros((slab.shape[2], lpad - past), F32)
            slab[slot, s, :, past:past + n_new] = new[s]
    st = jnp.concatenate([_dot(qi_ref[s], ski[slot, s].astype(BF16)) for s in range(group)], axis=0)
    st = jnp.maximum(st, 0.0) * jnp.concatenate([wc_ref[s] for s in range(group)], axis=0)
    for s in range(group):
        per_q = [jnp.sum(st[(s * n_new + q) * N_HEADS:(s * n_new + q + 1) * N_HEADS, :], axis=0, keepdims=True)
                 for q in range(n_new)]
        score = jnp.concatenate(per_q + [per_q[-1]] * (8 - n_new), axis=0)
        scores[s * 8:(s + 1) * 8, :] = jnp.where(visible8, score, NEG_INF)

    rows = group * 8

    def store_bias(v):
        bias[...] = v

    for has_next, side_work in ((g + 1 < n_groups, prefetch_next), (g + 1 == n_groups, None)):
        @pl.when(has_next)
        def _():
            _topk_bias(store_bias, lambda: scores[...], lambda: lax.broadcasted_iota(I32, (rows, lpad), 1),
                       1, (rows, 1), TOPK_A, (lpad - 1).bit_length(), side_work=side_work)

    n_rows = n_new * N_HEADS
    all_rows = group * n_rows
    seq_rows = lambda s: slice(s * n_rows, (s + 1) * n_rows)

    def softmax_rows(lg):
        m = jnp.max(lg, axis=1, keepdims=True)
        p = jnp.exp2(lg - m)
        return p.astype(BF16), jnp.sum(p, axis=1, keepdims=True)

    lg = jnp.concatenate([_dot(qa_ref[s], ska[slot, s].astype(BF16)) for s in range(group)], axis=0)
    mask_rows = jnp.concatenate(
        [jnp.broadcast_to(bias[s * 8 + q:s * 8 + q + 1, :], (N_HEADS, lpad))
         for s in range(group) for q in range(n_new)], axis=0)
    p, l = softmax_rows(lg + mask_rows)
    for s in range(group):
        oa_ref[s] = _dot_nt(p[seq_rows(s), :], sva[slot, s].astype(BF16)) / l[seq_rows(s), :]

    lgb = jnp.concatenate([_dot(qb_ref[s], skb[slot, s].astype(BF16)) for s in range(group)], axis=0)
    gate = [jnp.sum(lgb[:, j * MOBA_BLOCK:(j + 1) * MOBA_BLOCK], axis=1, keepdims=True) for j in range(n_blk)]
    pieces = []
    for j in range(n_blk):
        rank = jnp.zeros((all_rows, 1), I32)
        for j2 in range(n_blk):
            if j2 != j:
                rank = rank + jnp.where((gate[j2] >= gate[j]) if j2 < j else (gate[j2] > gate[j]), 1, 0)
        pieces.append(lgb[:, j * MOBA_BLOCK:(j + 1) * MOBA_BLOCK] + jnp.where(rank < TOPK_B, 0.0, NEG_INF))
    colr = lax.broadcasted_iota(I32, (all_rows, LANES), 1)
    query = (lax.broadcasted_iota(I32, (all_rows, LANES), 0) % n_rows) // N_HEADS
    pieces.append(jnp.where(colr <= query, lgb[:, past:lpad], NEG_INF))
    p, l = softmax_rows(jnp.concatenate(pieces, axis=1))
    for s in range(group):
        ob_ref[s] = _dot_nt(p[seq_rows(s), :], svb[slot, s].astype(BF16)) / l[seq_rows(s), :]


def _attn_sample(page_table, cki, cka, cva, ckb, cvb, nki, nka, nva, nkb, nvb, qi, qa, qb2, wcol, group):
    n_seq, n_pages = page_table.shape
    n_new = nki.shape[2]
    n_rows = n_new * N_HEADS
    assert n_rows % 16 == 0
    past = n_pages * PAGE
    lpad = past + LANES
    hbm = pl.BlockSpec(memory_space=pl.ANY)
    grp = lambda a, b: pl.BlockSpec((group, a, b), lambda g, pt: (g, 0, 0))
    slab = lambda w: pltpu.VMEM((2, group, w, lpad), F32)
    grid_spec = pltpu.PrefetchScalarGridSpec(
        num_scalar_prefetch=1,
        grid=(n_seq // group,),
        in_specs=[hbm, hbm, hbm, hbm, hbm,
                  grp(HEAD_DIM, n_new), grp(HEAD_DIM, n_new), grp(HEAD_DIM, n_new),
                  grp(2 * HEAD_DIM, n_new), grp(2 * HEAD_DIM, n_new),
                  grp(n_rows, HEAD_DIM), grp(n_rows, HEAD_DIM), grp(n_rows, 2 * HEAD_DIM), grp(n_rows, 1)],
        out_specs=(grp(n_rows, HEAD_DIM), grp(n_rows, 2 * HEAD_DIM)),
        scratch_shapes=[
            slab(HEAD_DIM),
            slab(HEAD_DIM),
            slab(HEAD_DIM),
            slab(2 * HEAD_DIM),
            slab(2 * HEAD_DIM),
            pltpu.VMEM((group * 8, lpad), F32),
            pltpu.VMEM((group * 8, lpad), F32),
            pltpu.SemaphoreType.DMA((2,)),
        ],
    )
    return pl.pallas_call(
        functools.partial(_attn_sample_kernel, group=group, n_new=n_new, past=past),
        grid_spec=grid_spec,
        out_shape=(jax.ShapeDtypeStruct((n_seq, n_rows, HEAD_DIM), F32),
                   jax.ShapeDtypeStruct((n_seq, n_rows, 2 * HEAD_DIM), F32)),
        compiler_params=pltpu.CompilerParams(
            dimension_semantics=("arbitrary",), vmem_limit_bytes=VMEM_LIMIT),
        name="attn_sample",
    )(page_table, cki, cka, cva, ckb, cvb, nki, nka, nva, nkb, nvb, qi, qa, qb2, wcol)


def _merge_ffn_kernel(x_ref, oa_ref, ob_ref, sga_ref, sgb_ref, wa_ref, wb_ref, wo_ref,
                      g_ref, wg_ref, wu_ref, wf_ref, o_ref):
    m = sga_ref[...] * _dot(oa_ref[...], wa_ref[...]) + sgb_ref[...] * _dot(ob_ref[...], wb_ref[...])
    x = x_ref[...] + _dot(m.astype(BF16), wo_ref[...])
    o_ref[...] = _half_swiglu(x, g_ref, wg_ref, wu_ref, wf_ref)


def _merge_ffn(x, oa, ob, sga, sgb, wa, wb, wo, g, wg, wu, wf, tm):
    t = x.shape[0]
    row = lambda w: pl.BlockSpec((tm, w), lambda i: (i, 0))
    return pl.pallas_call(
        _merge_ffn_kernel,
        grid=(t // tm,),
        in_specs=[row(D_MODEL), row(W_Q), row(W_Q), row(D_MODEL), row(D_MODEL),
                  _const_spec((W_Q, D_MODEL)), _const_spec((W_Q, D_MODEL)), _const_spec((D_MODEL, D_MODEL)),
                  _const_spec((1, D_MODEL)), _const_spec((D_MODEL, D_FF_PAD)), _const_spec((D_MODEL, D_FF_PAD)),
                  _const_spec((D_FF_PAD, D_MODEL))],
        out_specs=row(D_MODEL),
        out_shape=jax.ShapeDtypeStruct((t, D_MODEL), F32),
        compiler_params=pltpu.CompilerParams(
            dimension_semantics=("arbitrary",), vmem_limit_bytes=VMEM_LIMIT),
        name="merge_ffn",
    )(x, oa, ob, sga, sgb, wa, wb, wo, g, wg, wu, wf)


def _ffn_weights(w_in, w_out):
    pad = D_FF_PAD - D_FF
    wg = jnp.pad(w_in[:, :D_FF], ((0, 0), (0, pad))).astype(BF16)
    wu = jnp.pad(w_in[:, D_FF:], ((0, 0), (0, pad))).astype(BF16)
    wo = jnp.pad(w_out, ((0, pad), (0, 0))).astype(BF16)
    return wg, wu, wo


def _rope_tables(pos):
    half = HEAD_DIM // 2
    inv = ROPE_THETA ** (-jnp.arange(half, dtype=F32) / half)
    ang = pos.astype(F32)[:, None] * inv[None, :]
    cos, sin = jnp.cos(ang), jnp.sin(ang)
    return (jnp.concatenate([cos, cos, cos, cos], axis=1),
            jnp.concatenate([-sin, sin, -sin, sin], axis=1))


def _head_rows(a, n_seq, n_new):
    return a.reshape(n_seq, n_new * N_HEADS, a.shape[1] // N_HEADS)


def _from_head_rows(a, n_seq, n_new):
    return a.reshape(n_seq * n_new, N_HEADS * a.shape[2])


def kernel(x_prompt, x_sample, cache_k_a, cache_v_a, cache_kidx_a, cache_k_b, cache_v_b, page_table,
           ffn1_norm, ffn1_w_in, ffn1_w_out, mix_norm, w_in, q_norm_a, k_norm_a, q_norm_b, k_norm_b,
           gate_bias, w_up_a, w_up_b, w_out, ffn2_norm, ffn2_w_in, ffn2_w_out):
    n_seq, seq, _ = x_prompt.shape
    n_dec, n_new, _ = x_sample.shape
    depth = ffn1_norm.shape[0]
    n_pool = cache_k_a.shape[1]
    past = page_table.shape[1] * PAGE
    assert depth == 1 and seq % MOBA_BLOCK == 0 and past % MOBA_BLOCK == 0 and n_new <= 8
    assert seq >= 4 * TOPK_A and past + n_new >= 4 * TOPK_A
    l = 0
    tp, ts = n_seq * seq, n_dec * n_new

    xp = x_prompt.reshape(tp, D_MODEL)
    xs = x_sample.reshape(ts, D_MODEL)

    f1 = _ffn_weights(ffn1_w_in[l], ffn1_w_out[l])
    f2 = _ffn_weights(ffn2_w_in[l], ffn2_w_out[l])
    w = w_in[l]
    o = np.cumsum([0, W_Q, HEAD_DIM, HEAD_DIM, W_Q, HEAD_DIM, N_HEADS, W_Q, 2 * HEAD_DIM, 2 * HEAD_DIM,
                   D_MODEL, D_MODEL])
    c = lambda k: w[:, o[k]:o[k + 1]]
    z = lambda n: jnp.zeros((D_MODEL, n), F32)
    wp = jnp.concatenate([c(0), c(6), c(7), c(1), z(64), c(3), c(4), z(64), c(8), c(2), c(5), z(56),
                          c(9), c(10)], axis=1).astype(BF16)
    gain = jnp.concatenate([jnp.tile(q_norm_a[l], N_HEADS), jnp.tile(q_norm_b[l], N_HEADS),
                            jnp.tile(k_norm_b[l], HKV_B), k_norm_a[l], jnp.zeros((64,), F32)])[None, :]
    bias = gate_bias[l][None, :]
    lane = np.arange(2 * LANES)
    bd =jnp.asarray((lane[:, None] // HEAD_DIM) == (lane[None, :] // HEAD_DIM), BF16)
    cos_p, sin_p = _rope_tables(jnp.arange(seq, dtype=I32))
    cos_s, sin_s = _rope_tables(past + (jnp.arange(ts, dtype=I32) % n_new))
    wa, wb, wo = w_up_a[l].astype(BF16), w_up_b[l].astype(BF16), w_out[l].astype(BF16)
    g1, gm, g2 = ffn1_norm[l][None, :], mix_norm[l][None, :], ffn2_norm[l][None, :]

    tm = 512
    xp1 = _ffn(xp, g1, *f1, tm)
    (qa_t, qb_t, qi_t, kb_t, ka_t, ki_t, vb_t, va_t, wi_t, sga_p, sgb_p) = _proj(
        xp1, gm, wp, gain, bias, cos_p, sin_p, bd, tm, seq)
    oa_p, ob_p = _attn_prompt(qi_t, qa_t, qb_t, ki_t, ka_t, kb_t, va_t, vb_t, wi_t)
    yp = _merge_ffn(xp1, oa_p, ob_p, sga_p, sgb_p, wa, wb, wo, g2, *f2, tm)

    xs1 = _ffn(xs, g1, *f1, ts)
    (qa_s, qb_s, qi_s, kb_s, ka_s, ki_s, vb_s, va_s, wi_s, sga_s, sgb_s) = _proj(
        xs1, gm, wp, gain, bias, cos_s, sin_s, bd, ts, None)
    qi_r = _head_rows(qi_s, n_dec, n_new)
    qa_r = _head_rows(qa_s, n_dec, n_new)
    qb_r = _head_rows(qb_s, n_dec, n_new)
    zq = jnp.zeros_like(qb_r)
    first = ((jnp.arange(n_new * N_HEADS) % N_HEADS) < G_B)[None, :, None]
    qb2 = jnp.concatenate([jnp.where(first, qb_r, zq), jnp.where(first, zq, qb_r)], axis=2)
    wcol = _head_rows(wi_s, n_dec, n_new)
    pages_t = lambda a: jnp.swapaxes(a[l].reshape(n_pool, PAGE, -1), 1, 2)
    new_t = lambda a: jnp.swapaxes(a.reshape(n_dec, n_new, a.shape[1]), 1, 2)
    oa_r, ob_r = _attn_sample(
        page_table,
        pages_t(cache_kidx_a), pages_t(cache_k_a), pages_t(cache_v_a), pages_t(cache_k_b), pages_t(cache_v_b),
        new_t(ki_s), new_t(ka_s), new_t(va_s), new_t(kb_s), new_t(vb_s), qi_r, qa_r, qb2, wcol, SAMPLE_GROUP)
    oa_s = _from_head_rows(oa_r, n_dec, n_new).astype(BF16)
    ob_sel = jnp.where(first, ob_r[:, :, :HEAD_DIM], ob_r[:, :, HEAD_DIM:])
    ob_s = _from_head_rows(ob_sel, n_dec, n_new).astype(BF16)
    ys = _merge_ffn(xs1, oa_s, ob_s, sga_s, sgb_s, wa, wb, wo, g2, *f2, ts)

    d = depth
    tok = lambda a: jnp.swapaxes(a, 1, 2)
    return (yp.reshape(n_seq, seq, D_MODEL), ys.reshape(n_dec, n_new, D_MODEL),
            tok(ka_t).reshape(d, n_seq, seq, 1, HEAD_DIM), tok(va_t).reshape(d, n_seq, seq, 1, HEAD_DIM),
            tok(ki_t).reshape(d, n_seq, seq, HEAD_DIM),
            tok(kb_t).reshape(d, n_seq, seq, HKV_B, HEAD_DIM), tok(vb_t).reshape(d, n_seq, seq, HKV_B, HEAD_DIM),
            ka_s.reshape(d, n_dec, n_new, 1, HEAD_DIM), va_s.reshape(d, n_dec, n_new, 1, HEAD_DIM),
            ki_s.reshape(d, n_dec, n_new, HEAD_DIM),
            kb_s.reshape(d, n_dec, n_new, HKV_B, HEAD_DIM), vb_s.reshape(d, n_dec, n_new, HKV_B, HEAD_DIM))
```

```python
import functools

import numpy as np
import jax
import jax.numpy as jnp
from jax import lax
from jax.experimental import pallas as pl
from jax.experimental.pallas import tpu as pltpu

F32 = jnp.float32
BF16 = jnp.bfloat16
I32 = jnp.int32

D_MODEL = 1024
HEAD_DIM = 64
N_HEADS = 8
HKV_B = 2
G_B = N_HEADS // HKV_B
D_FF = 2752
TOPK_A = 256
MOBA_BLOCK = 256
TOPK_B = 3
PAGE = 128
ROPE_THETA = 10000.0
EPS = 1e-6

LANES = 128
D_FF_PAD = 2816
FF_CHUNK = 512
W_Q = N_HEADS * HEAD_DIM

A_W = 1280
B_W = 640
C_W = 256
G_W = 2 * D_MODEL
P_W = A_W + B_W + C_W + G_W

LOG2E = 1.4426950408889634
INT_MIN = np.int32(-2 ** 31)
NEG_INF = float("-inf")
VMEM_LIMIT = 56 * 1024 * 1024


def _dot(a, b):
    return jnp.dot(a, b, preferred_element_type=F32)


def _dot_nt(a, b):
    return lax.dot_general(a, b, (((1,), (1,)), ((), ())), preferred_element_type=F32)


def _split_bf16(x):
    hi = x.astype(BF16)
    lo = (x - hi.astype(F32)).astype(BF16)
    return hi, lo


def _rms(x, g):
    ms = jnp.mean(x * x, axis=-1, keepdims=True)
    return x * lax.rsqrt(ms + EPS) * g


def _const_spec(shape):
    nd = len(shape)
    return pl.BlockSpec(shape, lambda *_: (0,) * nd, pipeline_mode=pl.Buffered(1))


def _half_swiglu(x, g_ref, wg_ref, wu_ref, wo_ref):
    h = _rms(x, g_ref[...]).astype(BF16)
    acc = jnp.zeros_like(x)
    for s in range(0, D_FF_PAD, FF_CHUNK):
        e = min(s + FF_CHUNK, D_FF_PAD)
        g = _dot(h, wg_ref[:, s:e])
        u = _dot(h, wu_ref[:, s:e])
        a = (g * jax.nn.sigmoid(g) * u).astype(BF16)
        acc = acc + _dot(a, wo_ref[s:e, :])
    return x + 0.5 * acc


def _ffn_kernel(x_ref, g_ref, wg_ref, wu_ref, wo_ref, o_ref):
    o_ref[...] = _half_swiglu(x_ref[...], g_ref, wg_ref, wu_ref, wo_ref)


def _ffn(x, g, wg, wu, wo, tm):
    t = x.shape[0]
    return pl.pallas_call(
        _ffn_kernel,
        grid=(t // tm,),
        in_specs=[
            pl.BlockSpec((tm, D_MODEL), lambda i: (i, 0)),
            _const_spec((1, D_MODEL)),
            _const_spec((D_MODEL, D_FF_PAD)),
            _const_spec((D_MODEL, D_FF_PAD)),
            _const_spec((D_FF_PAD, D_MODEL)),
        ],
        out_specs=pl.BlockSpec((tm, D_MODEL), lambda i: (i, 0)),
        out_shape=jax.ShapeDtypeStruct((t, D_MODEL), F32),
        compiler_params=pltpu.CompilerParams(
            dimension_semantics=("arbitrary",), vmem_limit_bytes=VMEM_LIMIT),
        name="ffn",
    )(x, g, wg, wu, wo)


def _rope(y, cos, sin, first_half):
    r_lo = pltpu.roll(y, 32, 1)
    r_hi = pltpu.roll(y, 96, 1)
    return y * cos + jnp.where(first_half, r_hi, r_lo) * sin


def _proj_kernel(x_ref, gm_ref, wp_ref, gain_ref, bias_ref, cos_ref, sin_ref, bd_ref,
                 qa_ref, qb_ref, qi_ref, kb_ref, ka_ref, ki_ref, vb_ref, va_ref, wi_ref,
                 sga_ref, sgb_ref, *, transposed):
    x = x_ref[...]
    tm = x.shape[0]
    h = _rms(x, gm_ref[...]).astype(BF16)
    cos = cos_ref[...]
    sin = sin_ref[...]
    bd = bd_ref[...]
    lane = lax.broadcasted_iota(I32, (tm, LANES), 1)
    first_half = (lane % HEAD_DIM) < (HEAD_DIM // 2)

    z_a = _dot(h, wp_ref[:, 0:A_W])
    tiles_a = []
    for j in range(0, A_W // LANES, 2):
        z2 = z_a[:, j * LANES:(j + 2) * LANES]
        hi, lo = _split_bf16(z2 * z2)
        ms = (_dot(hi, bd) + _dot(lo, bd)) * (1.0 / HEAD_DIM)
        y2 = z2 * lax.rsqrt(ms + EPS) * gain_ref[:, j * LANES:(j + 2) * LANES]
        for t in range(2):
            tiles_a.append(_rope(y2[:, t * LANES:(t + 1) * LANES], cos, sin, first_half))

    z_b = _dot(h, wp_ref[:, A_W:A_W + B_W])
    tiles_b = [_rope(z_b[:, j * LANES:(j + 1) * LANES], cos, sin, first_half)
               for j in range(B_W // LANES)]

    z_c = _dot(h, wp_ref[:, A_W + B_W:A_W + B_W + C_W])
    z_g = _dot(h, wp_ref[:, A_W + B_W + C_W:P_W]) + bias_ref[...]

    scale = HEAD_DIM ** -0.5
    qa = jnp.concatenate(tiles_a[0:4], axis=1) * (scale * LOG2E)
    qb = jnp.concatenate(tiles_a[4:8], axis=1) * (scale * LOG2E)
    qi = jnp.concatenate(tiles_b[0:4], axis=1) * scale
    wi_scale = N_HEADS ** -0.5
    if transposed:
        qa_ref[...] = qa.T.astype(BF16)
        qb_ref[...] = qb.T.astype(BF16)
        qi_ref[...] = qi.T.astype(BF16)
        kb_ref[...] = tiles_a[8].T
        ka_ref[...] = tiles_a[9].T[0:HEAD_DIM, :]
        ki_ref[...] = tiles_b[4].T[0:HEAD_DIM, :]
        zc_t = z_c.T
        vb_ref[...] = zc_t[0:LANES, :]
        va_ref[...] = zc_t[LANES:LANES + HEAD_DIM, :]
        wi_ref[...] = zc_t[LANES + HEAD_DIM:LANES + HEAD_DIM + N_HEADS, :] * wi_scale
    else:
        qa_ref[...] = qa.astype(BF16)
        qb_ref[...] = qb.astype(BF16)
        qi_ref[...] = qi.astype(BF16)
        kb_ref[...] = tiles_a[8]
        ka_ref[...] = tiles_a[9][:, 0:HEAD_DIM]
        ki_ref[...] = tiles_b[4][:, 0:HEAD_DIM]
        vb_ref[...] = z_c[:, 0:LANES]
        va_ref[...] = z_c[:, LANES:LANES + HEAD_DIM]
        wi_ref[...] = z_c[:, LANES + HEAD_DIM:LANES + HEAD_DIM + N_HEADS] * wi_scale
    sg = jax.nn.sigmoid(z_g)
    sga_ref[...] = sg[:, 0:D_MODEL]
    sgb_ref[...] = sg[:, D_MODEL:G_W]


def _proj(x, gm, wp, gain, bias, cos_t, sin_t, bd, tm, seq):
    t = x.shape[0]
    n_tab = cos_t.shape[0] // tm
    row = lambda w: pl.BlockSpec((tm, w), lambda i: (i, 0))
    widths = (W_Q, W_Q, W_Q, 2 * HEAD_DIM, HEAD_DIM, HEAD_DIM, 2 * HEAD_DIM, HEAD_DIM, N_HEADS)
    dtypes = (BF16, BF16, BF16, F32, F32, F32, F32, F32, F32)
    if seq is None:
        specs = [row(w) for w in widths]
        shapes = [jax.ShapeDtypeStruct((t, w), d) for w, d in zip(widths, dtypes)]
    else:
        nb = seq // tm
        specs = [pl.BlockSpec((None, w, tm), lambda i: (i // nb, 0, i % nb)) for w in widths]
        shapes = [jax.ShapeDtypeStruct((t // seq, w, seq), d) for w, d in zip(widths, dtypes)]
    out_specs = tuple(specs) + (row(D_MODEL), row(D_MODEL))
    out_shape = tuple(shapes) + (jax.ShapeDtypeStruct((t, D_MODEL), F32), jax.ShapeDtypeStruct((t, D_MODEL), F32))
    transposed = seq is not None
    return pl.pallas_call(
        functools.partial(_proj_kernel, transposed=transposed),
        grid=(t // tm,),
        in_specs=[
            row(D_MODEL),
            _const_spec((1, D_MODEL)),
            _const_spec((D_MODEL, P_W)),
            _const_spec((1, A_W)),
            _const_spec((1, G_W)),
            pl.BlockSpec((tm, LANES), lambda i: (i % n_tab, 0)),
            pl.BlockSpec((tm, LANES), lambda i: (i % n_tab, 0)),
            _const_spec((2 * LANES, 2 * LANES)),
        ],
        out_specs=out_specs,
        out_shape=out_shape,
        compiler_params=pltpu.CompilerParams(
            dimension_semantics=("arbitrary",), vmem_limit_bytes=VMEM_LIMIT),
        name="proj",
    )(x, gm, wp, gain, bias, cos_t, sin_t, bd)


def _count(mask, axis):
    return jnp.sum(jnp.where(mask, 1.0, 0.0), axis=axis, keepdims=True)


def _key_to_score(key):
    return lax.bitcast_convert_type(key ^ ((key >> 31) & np.int32(0x7FFFFFFF)), F32)


N_VALUE_PASSES = 16


def _kth_largest(load_scores, axis, vec_shape, k_sel, side_work=None):
    def value_bits(it, ans):
        if side_work is not None:
            side_work(it)
        shift = 30 - 2 * it
        best = ans
        for digit in (1, 2, 3):
            cand = ans | lax.shift_left(np.int32(digit), shift)
            thr = _key_to_score(cand ^ INT_MIN)
            best = jnp.where(_count(load_scores() >= thr, axis) >= k_sel, cand, best)
        return best

    return _key_to_score(lax.fori_loop(0, N_VALUE_PASSES, value_bits, jnp.zeros(vec_shape, I32)) ^ INT_MIN)


def _score_to_key(score):
    b = lax.bitcast_convert_type(score, I32)
    return b ^ ((b >> 31) & np.int32(0x7FFFFFFF))


def _count_rows16(mask):
    rows, q = mask.shape
    assert rows % 16 == 0 and rows // 16 <= 256
    ones = jnp.where(mask, jnp.ones((), BF16), jnp.zeros((), BF16))
    part = ones[0:16, :]
    for r in range(16, rows, 16):
        part = part + ones[r:r + 16, :]
    return jnp.sum(part.astype(F32), axis=0, keepdims=True)


def _kth_largest_rows(load_scores, coarse_ref, q_cols, k_sel):
    n = load_scores().shape[0]
    coarse_ref[0:n, :] = load_scores().astype(BF16)
    one = np.int32(1)

    def bf16_of(pattern):
        k = pattern - np.int32(0x8000)
        bits = k ^ ((k >> 15) & np.int32(0x7FFF))
        return lax.bitcast_convert_type(lax.shift_left(bits, 16), F32).astype(BF16)

    def coarse_bit(it, ans):
        cand = ans | lax.shift_left(one, 15 - it)
        return jnp.where(_count_rows16(coarse_ref[0:n, :] >= bf16_of(cand)) >= k_sel, cand, ans)

    coarse = lax.fori_loop(0, 16, coarse_bit, jnp.zeros((1, q_cols), I32))
    base = _score_to_key(bf16_of(coarse).astype(F32)) - np.int32(2 ** 15)

    def fine_bit(it, off):
        cand = off | lax.shift_left(one, 16 - it)
        return jnp.where(_count(load_scores() >= _key_to_score(base + cand), 0) >= k_sel, cand, off)

    return _key_to_score(base + lax.fori_loop(0, 17, fine_bit, jnp.zeros((1, q_cols), I32)))


def _topk_bias(store, load_keys, key_index, axis, vec_shape, k_sel, n_index_bits, coarse_ref=None,
               side_work=None):
    one = np.int32(1)
    if coarse_ref is None:
        thr = _kth_largest(load_keys, axis, vec_shape, k_sel, side_work)
    else:
        thr = _kth_largest_rows(load_keys, coarse_ref, vec_shape[1], k_sel)
    has_tie = jnp.max(_count(load_keys() >= thr, axis)) > k_sel

    @pl.when(jnp.logical_not(has_tie))
    def _():
        store(jnp.where(load_keys() >= thr, 0.0, NEG_INF))

    @pl.when(has_tie)
    def _():
        need = k_sel - _count(load_keys() > thr, axis)

        def index_bit(it, cut):
            cand = cut | lax.shift_left(one, n_index_bits - 1 - it)
            hit = jnp.where(load_keys() == thr, jnp.where(key_index() < cand, 1.0, 0.0), 0.0)
            return jnp.where(jnp.sum(hit, axis=axis, keepdims=True) < need, cand, cut)

        cut = lax.fori_loop(0, n_index_bits, index_bit, jnp.zeros(vec_shape, I32))
        keys = load_keys()
        keep_eq = jnp.where(key_index() <= cut, 0.0, NEG_INF)
        store(jnp.where(keys > thr, 0.0, jnp.where(keys == thr, keep_eq, NEG_INF)))


HEADS_PER_TRIP = 4
V_ROWS = HEAD_DIM + 16


def _pipelined_heads(n_chunks, chunk, head_logits, values_t, lgb, store_out):
    rows = lambda c: slice(c * chunk, (c + 1) * chunk)

    def logits_stage(h, slot):
        logits = head_logits(h)
        m = None
        for c in range(n_chunks):
            lg = logits(c)
            lgb[slot, rows(c), :] = lg
            cm = jnp.max(lg, axis=0, keepdims=True)
            m = cm if m is None else jnp.maximum(m, cm)
        return m

    def step(h, m_cur, cur, nxt):
        logits = head_logits(jnp.minimum(h + 1, N_HEADS - 1))
        acc = None
        m_next = None
        for c in range(n_chunks):
            lg = logits(c)
            nxt[rows(c), :] = lg
            cm = jnp.max(lg, axis=0, keepdims=True)
            m_next = cm if m_next is None else jnp.maximum(m_next, cm)
            p = jnp.exp2(cur[rows(c), :] - m_cur).astype(BF16)
            pv = _dot(values_t(h, c), p)
            acc = pv if acc is None else acc + pv
        store_out(h, acc[0:HEAD_DIM, :] / acc[HEAD_DIM:HEAD_DIM + 1, :])
        return m_next

    def body(t, m):
        for k in range(HEADS_PER_TRIP):
            m = step(HEADS_PER_TRIP * t + k, m, lgb.at[k % 2], lgb.at[1 - k % 2])
        return m

    lax.fori_loop(0, N_HEADS // HEADS_PER_TRIP, body, logits_stage(0, 0))


def _attn_prompt_kernel(qi_ref, qa_ref, qb_ref, ki_ref, ka_ref, kb_ref, va_ref, vb_ref, wit,
                        oa_ref, ob_ref,
                        kib, kab, kbb, vat, vbt, kmh, kml, sc, sc16, bias, lgb, ot):
    seq = ki_ref.shape[1]
    tq = MOBA_BLOCK
    n_blk = seq // tq

    kk = jnp.concatenate([ka_ref[...], ki_ref[...]], axis=0).T
    kab[...] = kk[:, 0:HEAD_DIM].astype(BF16)
    kib[...] = kk[:, HEAD_DIM:2 * HEAD_DIM].astype(BF16)
    kb = kb_ref[...].T
    ones = jnp.ones((V_ROWS - HEAD_DIM, seq), BF16)
    vat[...] = jnp.concatenate([va_ref[...].astype(BF16), ones], axis=0)
    means = jnp.concatenate(
        [jnp.mean(kb[j * tq:(j + 1) * tq, :], axis=0, keepdims=True) for j in range(n_blk)]
        + [jnp.zeros((kmh.shape[1] - n_blk, 2 * HEAD_DIM), F32)], axis=0)
    for n in range(HKV_B):
        kbb[n] = kb[:, n * HEAD_DIM:(n + 1) * HEAD_DIM].astype(BF16)
        vbt[n] = jnp.concatenate([vb_ref[n * HEAD_DIM:(n + 1) * HEAD_DIM, :].astype(BF16), ones], axis=0)
        hi, lo = _split_bf16(means[:, n * HEAD_DIM:(n + 1) * HEAD_DIM])
        kmh[n] = hi
        kml[n] = lo

    r_loc = lax.broadcasted_iota(I32, (tq, tq), 0)
    c_loc = lax.broadcasted_iota(I32, (tq, tq), 1)
    causal = r_loc <= c_loc

    for i in range(n_blk):
        c0 = i * tq
        lk = c0 + tq
        cols = slice(c0, c0 + tq)

        sc[0:lk, :] = jnp.zeros((lk, tq), F32)

        def idx_head(h, carry):
            off = pl.multiple_of(h * HEAD_DIM, HEAD_DIM)
            s = _dot(kib[0:lk, :], qi_ref[pl.ds(off, HEAD_DIM), cols])
            sc[0:lk, :] += jnp.maximum(s, 0.0) * wit[pl.ds(h, 1), cols]
            return carry

        lax.fori_loop(0, N_HEADS, idx_head, 0)

        if i == 0:
            bias[0:tq, :] = jnp.where(causal, 0.0, NEG_INF)
        else:
            sc[c0:lk, :] = jnp.where(causal, sc[c0:lk, :], NEG_INF)

            def store_bias(v):
                bias[0:lk, :] = v

            _topk_bias(store_bias, lambda: sc[0:lk, :], lambda: lax.broadcasted_iota(I32, (lk, tq), 0),
                       0, (1, tq), TOPK_A, (lk - 1).bit_length(), coarse_ref=sc16)

        blk = lambda c: slice(c * tq, (c + 1) * tq)

        def store_head(h, o):
            ot[pl.ds(pl.multiple_of(h * HEAD_DIM, HEAD_DIM), HEAD_DIM), :] = o

        def dsa_logits(h):
            q_t = qa_ref[pl.ds(pl.multiple_of(h * HEAD_DIM, HEAD_DIM), HEAD_DIM), cols]
            return lambda c: _dot(kab[blk(c), :], q_t) + bias[blk(c), :]

        _pipelined_heads(i + 1, tq, dsa_logits, lambda h, c: vat[:, blk(c)], lgb, store_head)
        oa_ref[cols, :] = ot[...].T.astype(BF16)

        def moba_logits(h):
            n = h // G_B
            q_t = qb_ref[pl.ds(pl.multiple_of(h * HEAD_DIM, HEAD_DIM), HEAD_DIM), cols]
            block_bias = [None] * i
            if i > TOPK_B:
                gate = _dot(kmh[n], q_t) + _dot(kml[n], q_t)
                rows = lax.broadcasted_iota(I32, gate.shape, 0)
                for j in range(i):
                    gj = gate[j:j + 1, :]
                    beats = jnp.where(rows < j, jnp.where(gate >= gj, 1, 0), jnp.where(gate > gj, 1, 0))
                    rank = jnp.sum(jnp.where(rows < i, beats, 0), axis=0, keepdims=True)
                    block_bias[j] = jnp.where(rank < TOPK_B, 0.0, NEG_INF)

            def logits(c):
                lg = _dot(kbb[n, blk(c), :], q_t)
                if c == i:
                    return jnp.where(causal, lg, NEG_INF)
                return lg if block_bias[c] is None else lg + block_bias[c]

            return logits

        _pipelined_heads(i + 1, tq, moba_logits, lambda h, c: vbt[h // G_B, :, blk(c)], lgb, store_head)
        ob_ref[cols, :] = ot[...].T.astype(BF16)


def _attn_prompt(qi_t, qa_t, qb_t, ki_t, ka_t, kb_t, va_t, vb_t, wi_t):
    n_seq, _, seq = qi_t.shape
    tq = MOBA_BLOCK
    fm = lambda w: pl.BlockSpec((None, w, seq), lambda b: (b, 0, 0))
    row = lambda w: pl.BlockSpec((seq, w), lambda b: (b, 0))
    return pl.pallas_call(
        _attn_prompt_kernel,
        grid=(n_seq,),
        in_specs=[fm(W_Q), fm(W_Q), fm(W_Q), fm(HEAD_DIM), fm(HEAD_DIM), fm(2 * HEAD_DIM),
                  fm(HEAD_DIM), fm(2 * HEAD_DIM), fm(N_HEADS)],
        out_specs=(row(W_Q), row(W_Q)),
        out_shape=(jax.ShapeDtypeStruct((n_seq * seq, W_Q), BF16),
                   jax.ShapeDtypeStruct((n_seq * seq, W_Q), BF16)),
        scratch_shapes=[
            pltpu.VMEM((seq, HEAD_DIM), BF16),
            pltpu.VMEM((seq, HEAD_DIM), BF16),
            pltpu.VMEM((HKV_B, seq, HEAD_DIM), BF16),
            pltpu.VMEM((V_ROWS, seq), BF16),
            pltpu.VMEM((HKV_B, V_ROWS, seq), BF16),
            pltpu.VMEM((HKV_B, 16, HEAD_DIM), BF16),
            pltpu.VMEM((HKV_B, 16, HEAD_DIM), BF16),
            pltpu.VMEM((seq, tq), F32),
            pltpu.VMEM((seq, tq), BF16),
            pltpu.VMEM((seq, tq), F32),
            pltpu.VMEM((2, seq, tq), F32),
            pltpu.VMEM((W_Q, tq), F32),
        ],
        compiler_params=pltpu.CompilerParams(
            dimension_semantics=("arbitrary",), vmem_limit_bytes=VMEM_LIMIT),
        name="attn_prompt",
    )(qi_t, qa_t, qb_t, ki_t, ka_t, kb_t, va_t, vb_t, wi_t)


SAMPLE_GROUP = 4


def _attn_sample_kernel(pt_ref, cki_hbm, cka_hbm, cva_hbm, ckb_hbm, cvb_hbm,
                        new_ref,
                        qi_ref, qa_ref, qb_ref, wc_ref,
                        oa_ref, ob_ref,
                        ski, ska, sva, skb, svb, scores, bias, sem,
                        *, group, n_new, past):
    g = pl.program_id(0)
    n_groups = pl.num_programs(0)
    n_pages = past // PAGE
    lpad = ski.shape[3]
    slot = g % 2
    pools = ((cki_hbm, ski), (cka_hbm, ska), (cva_hbm, sva), (ckb_hbm, skb), (cvb_hbm, svb))

    def page_copies(grp, to_slot, t):
        s = t // n_pages
        p = t % n_pages
        page = pt_ref[grp * group + s, p]
        lanes = pl.ds(pl.multiple_of(p * PAGE, PAGE), PAGE)
        return [pltpu.make_async_copy(hbm.at[page], slab.at[to_slot, s, :, lanes], sem.at[to_slot])
                for hbm, slab in pools]

    def start_group(grp, to_slot):
        def body(t, carry):
            for cp in page_copies(grp, to_slot, t):
                cp.start()
            return carry
        lax.fori_loop(0, group * n_pages, body, 0)

    def wait_group(to_slot):
        for _, slab in pools:
            filled = slab.at[to_slot, :, :, 0:past]
            pltpu.make_async_copy(filled, filled, sem.at[to_slot]).wait()

    @pl.when(g == 0)
    def _():
        start_group(0, 0)

    wait_group(slot)

    n_starts = group * n_pages
    assert n_starts % N_VALUE_PASSES == 0

    def prefetch_next(it):
        for k in range(n_starts // N_VALUE_PASSES):
            for cp in page_copies(g + 1, 1 - slot, it * (n_starts // N_VALUE_PASSES) + k):
                cp.start()

    n_blk = past // MOBA_BLOCK
    col = lax.broadcasted_iota(I32, (8, lpad), 1)
    qrow = lax.broadcasted_iota(I32, (8, lpad), 0)
    visible8 = col <= past + jnp.minimum(qrow, n_new - 1)
    for s in range(group):
        r0 = 0
        for _, slab in pools:
            w = slab.shape[2]
            slab[slot, s, :, past:lpad] = jnp.zeros((w, lpad - past), F32)
            slab[slot, s, :, past:past + n_new] = new_ref[s, r0:r0 + w, :]
            r0 += w
    st = jnp.concatenate([_dot(qi_ref[s], ski[slot, s].astype(BF16)) for s in range(group)], axis=0)
    st = jnp.maximum(st, 0.0) * jnp.concatenate([wc_ref[s] for s in range(group)], axis=0)
    for s in range(group):
        per_q = [jnp.sum(st[(s * n_new + q) * N_HEADS:(s * n_new + q + 1) * N_HEADS, :], axis=0, keepdims=True)
                 for q in range(n_new)]
        score = jnp.concatenate(per_q + [per_q[-1]] * (8 - n_new), axis=0)
        scores[s * 8:(s + 1) * 8, :] = jnp.where(visible8, score, NEG_INF)

    rows = group * 8

    def store_bias(v):
        bias[...] = v

    for has_next, side_work in ((g + 1 < n_groups, prefetch_next), (g + 1 == n_groups, None)):
        @pl.when(has_next)
        def _():
            _topk_bias(store_bias, lambda: scores[...], lambda: lax.broadcasted_iota(I32, (rows, lpad), 1),
                       1, (rows, 1), TOPK_A, (lpad - 1).bit_length(), side_work=side_work)

    n_rows = n_new * N_HEADS
    all_rows = group * n_rows
    seq_rows = lambda s: slice(s * n_rows, (s + 1) * n_rows)

    def softmax_rows(lg):
        m = jnp.max(lg, axis=1, keepdims=True)
        p = jnp.exp2(lg - m)
        return p.astype(BF16), jnp.sum(p, axis=1, keepdims=True)

    lg = jnp.concatenate([_dot(qa_ref[s], ska[slot, s].astype(BF16)) for s in range(group)], axis=0)
    mask_rows = jnp.concatenate(
        [jnp.broadcast_to(bias[s * 8 + q:s * 8 + q + 1, :], (N_HEADS, lpad))
         for s in range(group) for q in range(n_new)], axis=0)
    p, l = softmax_rows(lg + mask_rows)
    for s in range(group):
        oa_ref[s] = _dot_nt(p[seq_rows(s), :], sva[slot, s].astype(BF16)) / l[seq_rows(s), :]

    lgb = jnp.concatenate([_dot(qb_ref[s], skb[slot, s].astype(BF16)) for s in range(group)], axis=0)
    gate = [jnp.sum(lgb[:, j * MOBA_BLOCK:(j + 1) * MOBA_BLOCK], axis=1, keepdims=True) for j in range(n_blk)]
    pieces = []
    for j in range(n_blk):
        rank = jnp.zeros((all_rows, 1), I32)
        for j2 in range(n_blk):
            if j2 != j:
                rank = rank + jnp.where((gate[j2] >= gate[j]) if j2 < j else (gate[j2] > gate[j]), 1, 0)
        pieces.append(lgb[:, j * MOBA_BLOCK:(j + 1) * MOBA_BLOCK] + jnp.where(rank < TOPK_B, 0.0, NEG_INF))
    colr = lax.broadcasted_iota(I32, (all_rows, LANES), 1)
    query = (lax.broadcasted_iota(I32, (all_rows, LANES), 0) % n_rows) // N_HEADS
    pieces.append(jnp.where(colr <= query, lgb[:, past:lpad], NEG_INF))
    p, l = softmax_rows(jnp.concatenate(pieces, axis=1))
    for s in range(group):
        ob_ref[s] = _dot_nt(p[seq_rows(s), :], svb[slot, s].astype(BF16)) / l[seq_rows(s), :]


def _attn_sample(page_table, cki, cka, cva, ckb, cvb, new_rows, qi, qa, qb2, wcol, group):
    n_seq, n_pages = page_table.shape
    n_new = new_rows.shape[2]
    new_w = new_rows.shape[1]
    assert new_w == 3 * HEAD_DIM + 2 * 2 * HEAD_DIM
    n_rows = n_new * N_HEADS
    assert n_rows % 16 == 0
    past = n_pages * PAGE
    lpad = past + LANES
    hbm = pl.BlockSpec(memory_space=pl.ANY)
    grp = lambda a, b: pl.BlockSpec((group, a, b), lambda g, pt: (g, 0, 0))
    slab = lambda w: pltpu.VMEM((2, group, w, lpad), F32)
    grid_spec = pltpu.PrefetchScalarGridSpec(
        num_scalar_prefetch=1,
        grid=(n_seq // group,),
        in_specs=[hbm, hbm, hbm, hbm, hbm,
                  grp(new_w, n_new),
                  grp(n_rows, HEAD_DIM), grp(n_rows, HEAD_DIM), grp(n_rows, 2 * HEAD_DIM), grp(n_rows, 1)],
        out_specs=(grp(n_rows, HEAD_DIM), grp(n_rows, 2 * HEAD_DIM)),
        scratch_shapes=[
            slab(HEAD_DIM),
            slab(HEAD_DIM),
            slab(HEAD_DIM),
            slab(2 * HEAD_DIM),
            slab(2 * HEAD_DIM),
            pltpu.VMEM((group * 8, lpad), F32),
            pltpu.VMEM((group * 8, lpad), F32),
            pltpu.SemaphoreType.DMA((2,)),
        ],
    )
    return pl.pallas_call(
        functools.partial(_attn_sample_kernel, group=group, n_new=n_new, past=past),
        grid_spec=grid_spec,
        out_shape=(jax.ShapeDtypeStruct((n_seq, n_rows, HEAD_DIM), F32),
                   jax.ShapeDtypeStruct((n_seq, n_rows, 2 * HEAD_DIM), F32)),
        compiler_params=pltpu.CompilerParams(
            dimension_semantics=("arbitrary",), vmem_limit_bytes=VMEM_LIMIT),
        name="attn_sample",
    )(page_table, cki, cka, cva, ckb, cvb, new_rows, qi, qa, qb2, wcol)


def _merge_ffn_kernel(x_ref, oa_ref, ob_ref, sga_ref, sgb_ref, wa_ref, wb_ref, wo_ref,
                      g_ref, wg_ref, wu_ref, wf_ref, o_ref):
    m = sga_ref[...] * _dot(oa_ref[...], wa_ref[...]) + sgb_ref[...] * _dot(ob_ref[...], wb_ref[...])
    x = x_ref[...] + _dot(m.astype(BF16), wo_ref[...])
    o_ref[...] = _half_swiglu(x, g_ref, wg_ref, wu_ref, wf_ref)


def _merge_ffn(x, oa, ob, sga, sgb, wa, wb, wo, g, wg, wu, wf, tm):
    t = x.shape[0]
    row = lambda w: pl.BlockSpec((tm, w), lambda i: (i, 0))
    return pl.pallas_call(
        _merge_ffn_kernel,
        grid=(t // tm,),
        in_specs=[row(D_MODEL), row(W_Q), row(W_Q), row(D_MODEL), row(D_MODEL),
                  _const_spec((W_Q, D_MODEL)), _const_spec((W_Q, D_MODEL)), _const_spec((D_MODEL, D_MODEL)),
                  _const_spec((1, D_MODEL)), _const_spec((D_MODEL, D_FF_PAD)), _const_spec((D_MODEL, D_FF_PAD)),
                  _const_spec((D_FF_PAD, D_MODEL))],
        out_specs=row(D_MODEL),
        out_shape=jax.ShapeDtypeStruct((t, D_MODEL), F32),
        compiler_params=pltpu.CompilerParams(
            dimension_semantics=("arbitrary",), vmem_limit_bytes=VMEM_LIMIT),
        name="merge_ffn",
    )(x, oa, ob, sga, sgb, wa, wb, wo, g, wg, wu, wf)


def _ffn_weights(w_in, w_out):
    pad = D_FF_PAD - D_FF
    wg = jnp.pad(w_in[:, :D_FF], ((0, 0), (0, pad))).astype(BF16)
    wu = jnp.pad(w_in[:, D_FF:], ((0, 0), (0, pad))).astype(BF16)
    wo = jnp.pad(w_out, ((0, pad), (0, 0))).astype(BF16)
    return wg, wu, wo


def _rope_tables(pos):
    half = HEAD_DIM // 2
    inv = ROPE_THETA ** (-jnp.arange(half, dtype=F32) / half)
    ang = pos.astype(F32)[:, None] * inv[None, :]
    cos, sin = jnp.cos(ang), jnp.sin(ang)
    return (jnp.concatenate([cos, cos, cos, cos], axis=1),
            jnp.concatenate([-sin, sin, -sin, sin], axis=1))


def _head_rows(a, n_seq, n_new):
    return a.reshape(n_seq, n_new * N_HEADS, a.shape[1] // N_HEADS)


def _from_head_rows(a, n_seq, n_new):
    return a.reshape(n_seq * n_new, N_HEADS * a.shape[2])


def kernel(x_prompt, x_sample, cache_k_a, cache_v_a, cache_kidx_a, cache_k_b, cache_v_b, page_table,
           ffn1_norm, ffn1_w_in, ffn1_w_out, mix_norm, w_in, q_norm_a, k_norm_a, q_norm_b, k_norm_b,
           gate_bias, w_up_a, w_up_b, w_out, ffn2_norm, ffn2_w_in, ffn2_w_out):
    n_seq, seq, _ = x_prompt.shape
    n_dec, n_new, _ = x_sample.shape
    depth = ffn1_norm.shape[0]
    n_pool = cache_k_a.shape[1]
    past = page_table.shape[1] * PAGE
    assert depth == 1 and seq % MOBA_BLOCK == 0 and past % MOBA_BLOCK == 0 and n_new <= 8
    assert seq >= 4 * TOPK_A and past + n_new >= 4 * TOPK_A
    l = 0
    tp, ts = n_seq * seq, n_dec * n_new

    xp = x_prompt.reshape(tp, D_MODEL)
    xs = x_sample.reshape(ts, D_MODEL)

    f1 = _ffn_weights(ffn1_w_in[l], ffn1_w_out[l])
    f2 = _ffn_weights(ffn2_w_in[l], ffn2_w_out[l])
    w = w_in[l]
    o = np.cumsum([0, W_Q, HEAD_DIM, HEAD_DIM, W_Q, HEAD_DIM, N_HEADS, W_Q, 2 * HEAD_DIM, 2 * HEAD_DIM,
                   D_MODEL, D_MODEL])
    c = lambda k: w[:, o[k]:o[k + 1]]
    z = lambda n: jnp.zeros((D_MODEL, n), F32)
    wp = jnp.concatenate([c(0), c(6), c(7), c(1), z(64), c(3), c(4), z(64), c(8), c(2), c(5), z(56),
                          c(9), c(10)], axis=1).astype(BF16)
    gain = jnp.concatenate([jnp.tile(q_norm_a[l], N_HEADS), jnp.tile(q_norm_b[l], N_HEADS),
                            jnp.tile(k_norm_b[l], HKV_B), k_norm_a[l], jnp.zeros((64,), F32)])[None, :]
    bias = gate_bias[l][None, :]
    lane = np.arange(2 * LANES)
    bd =jnp.asarray((lane[:, None] // HEAD_DIM) == (lane[None, :] // HEAD_DIM), BF16)
    cos_p, sin_p = _rope_tables(jnp.arange(seq, dtype=I32))
    cos_s, sin_s = _rope_tables(past + (jnp.arange(ts, dtype=I32) % n_new))
    wa, wb, wo = w_up_a[l].astype(BF16), w_up_b[l].astype(BF16), w_out[l].astype(BF16)
    g1, gm, g2 = ffn1_norm[l][None, :], mix_norm[l][None, :], ffn2_norm[l][None, :]

    tm = 512
    xp1 = _ffn(xp, g1, *f1, tm)
    (qa_t, qb_t, qi_t, kb_t, ka_t, ki_t, vb_t, va_t, wi_t, sga_p, sgb_p) = _proj(
        xp1, gm, wp, gain, bias, cos_p, sin_p, bd, tm, seq)
    oa_p, ob_p = _attn_prompt(qi_t, qa_t, qb_t, ki_t, ka_t, kb_t, va_t, vb_t, wi_t)
    yp = _merge_ffn(xp1, oa_p, ob_p, sga_p, sgb_p, wa, wb, wo, g2, *f2, tm)

    xs1 = _ffn(xs, g1, *f1, ts)
    (qa_s, qb_s, qi_s, kb_s, ka_s, ki_s, vb_s, va_s, wi_s, sga_s, sgb_s) = _proj(
        xs1, gm, wp, gain, bias, cos_s, sin_s, bd, ts, None)
    qi_r = _head_rows(qi_s, n_dec, n_new)
    qa_r = _head_rows(qa_s, n_dec, n_new)
    qb_r = _head_rows(qb_s, n_dec, n_new)
    zq = jnp.zeros_like(qb_r)
    first = ((jnp.arange(n_new * N_HEADS) % N_HEADS) < G_B)[None, :, None]
    qb2 = jnp.concatenate([jnp.where(first, qb_r, zq), jnp.where(first, zq, qb_r)], axis=2)
    wcol = _head_rows(wi_s, n_dec, n_new)
    pages_t = lambda a: jnp.swapaxes(a[l].reshape(n_pool, PAGE, -1), 1, 2)
    new_rows = jnp.concatenate([ki_s, ka_s, va_s, kb_s, vb_s], axis=1)
    new_rows = jnp.swapaxes(new_rows.reshape(n_dec, n_new, new_rows.shape[1]), 1, 2)
    oa_r, ob_r = _attn_sample(
        page_table,
        pages_t(cache_kidx_a), pages_t(cache_k_a), pages_t(cache_v_a), pages_t(cache_k_b), pages_t(cache_v_b),
        new_rows, qi_r, qa_r, qb2, wcol, SAMPLE_GROUP)
    oa_s = _from_head_rows(oa_r, n_dec, n_new).astype(BF16)
    ob_sel = jnp.where(first, ob_r[:, :, :HEAD_DIM], ob_r[:, :, HEAD_DIM:])
    ob_s = _from_head_rows(ob_sel, n_dec, n_new).astype(BF16)
    ys = _merge_ffn(xs1, oa_s, ob_s, sga_s, sgb_s, wa, wb, wo, g2, *f2, ts)

    d = depth
    tok = lambda a: jnp.swapaxes(a, 1, 2)
    return (yp.reshape(n_seq, seq, D_MODEL), ys.reshape(n_dec, n_new, D_MODEL),
            tok(ka_t).reshape(d, n_seq, seq, 1, HEAD_DIM), tok(va_t).reshape(d, n_seq, seq, 1, HEAD_DIM),
            tok(ki_t).reshape(d, n_seq, seq, HEAD_DIM),
            tok(kb_t).reshape(d, n_seq, seq, HKV_B, HEAD_DIM), tok(vb_t).reshape(d, n_seq, seq, HKV_B, HEAD_DIM),
            ka_s.reshape(d, n_dec, n_new, 1, HEAD_DIM), va_s.reshape(d, n_dec, n_new, 1, HEAD_DIM),
            ki_s.reshape(d, n_dec, n_new, HEAD_DIM),
            kb_s.reshape(d, n_dec, n_new, HKV_B, HEAD_DIM), vb_s.reshape(d, n_dec, n_new, HKV_B, HEAD_DIM))
```

```python
import functools

import numpy as np
import jax
import jax.numpy as jnp
from jax import lax
from jax.experimental import pallas as pl
from jax.experimental.pallas import tpu as pltpu

F32 = jnp.float32
BF16 = jnp.bfloat16
I32 = jnp.int32

D_MODEL = 1024
HEAD_DIM = 64
N_HEADS = 8
HKV_B = 2
G_B = N_HEADS // HKV_B
D_FF = 2752
TOPK_A = 256
MOBA_BLOCK = 256
TOPK_B = 3
PAGE = 128
ROPE_THETA = 10000.0
EPS = 1e-6

LANES = 128
D_FF_PAD = 2816
FF_CHUNK = 512
W_Q = N_HEADS * HEAD_DIM

A_W = 1280
B_W = 640
C_W = 256
G_W = 2 * D_MODEL
P_W = A_W + B_W + C_W + G_W

LOG2E = 1.4426950408889634
INT_MIN = np.int32(-2 ** 31)
NEG_INF = float("-inf")
VMEM_LIMIT = 56 * 1024 * 1024


def _dot(a, b):
    return jnp.dot(a, b, preferred_element_type=F32)


def _dot_nt(a, b):
    return lax.dot_general(a, b, (((1,), (1,)), ((), ())), preferred_element_type=F32)


def _split_bf16(x):
    hi = x.astype(BF16)
    lo = (x - hi.astype(F32)).astype(BF16)
    return hi, lo


def _rms(x, g):
    ms = jnp.mean(x * x, axis=-1, keepdims=True)
    return x * lax.rsqrt(ms + EPS) * g


def _const_spec(shape):
    nd = len(shape)
    return pl.BlockSpec(shape, lambda *_: (0,) * nd, pipeline_mode=pl.Buffered(1))


def _half_swiglu(x, g_ref, wg_ref, wu_ref, wo_ref):
    h = _rms(x, g_ref[...]).astype(BF16)
    acc = jnp.zeros_like(x)
    for s in range(0, D_FF_PAD, FF_CHUNK):
        e = min(s + FF_CHUNK, D_FF_PAD)
        g = _dot(h, wg_ref[:, s:e])
        u = _dot(h, wu_ref[:, s:e])
        a = (g * jax.nn.sigmoid(g) * u).astype(BF16)
        acc = acc + _dot(a, wo_ref[s:e, :])
    return x + 0.5 * acc


def _ffn_kernel(x_ref, g_ref, wg_ref, wu_ref, wo_ref, o_ref):
    o_ref[...] = _half_swiglu(x_ref[...], g_ref, wg_ref, wu_ref, wo_ref)


def _ffn(x, g, wg, wu, wo, tm):
    t = x.shape[0]
    return pl.pallas_call(
        _ffn_kernel,
        grid=(t // tm,),
        in_specs=[
            pl.BlockSpec((tm, D_MODEL), lambda i: (i, 0)),
            _const_spec((1, D_MODEL)),
            _const_spec((D_MODEL, D_FF_PAD)),
            _const_spec((D_MODEL, D_FF_PAD)),
            _const_spec((D_FF_PAD, D_MODEL)),
        ],
        out_specs=pl.BlockSpec((tm, D_MODEL), lambda i: (i, 0)),
        out_shape=jax.ShapeDtypeStruct((t, D_MODEL), F32),
        compiler_params=pltpu.CompilerParams(
            dimension_semantics=("arbitrary",), vmem_limit_bytes=VMEM_LIMIT),
        name="ffn",
    )(x, g, wg, wu, wo)


def _rope(y, cos, sin, first_half):
    r_lo = pltpu.roll(y, 32, 1)
    r_hi = pltpu.roll(y, 96, 1)
    return y * cos + jnp.where(first_half, r_hi, r_lo) * sin


def _proj_kernel(x_ref, gm_ref, wp_ref, gain_ref, bias_ref, cos_ref, sin_ref, bd_ref,
                 qa_ref, qb_ref, qi_ref, kb_ref, ka_ref, ki_ref, vb_ref, va_ref, wi_ref,
                 sga_ref, sgb_ref, *, transposed):
    x = x_ref[...]
    tm = x.shape[0]
    h = _rms(x, gm_ref[...]).astype(BF16)
    cos = cos_ref[...]
    sin = sin_ref[...]
    bd = bd_ref[...]
    lane = lax.broadcasted_iota(I32, (tm, LANES), 1)
    first_half = (lane % HEAD_DIM) < (HEAD_DIM // 2)

    z_a = _dot(h, wp_ref[:, 0:A_W])
    tiles_a = []
    for j in range(0, A_W // LANES, 2):
        z2 = z_a[:, j * LANES:(j + 2) * LANES]
        hi, lo = _split_bf16(z2 * z2)
        ms = (_dot(hi, bd) + _dot(lo, bd)) * (1.0 / HEAD_DIM)
        y2 = z2 * lax.rsqrt(ms + EPS) * gain_ref[:, j * LANES:(j + 2) * LANES]
        for t in range(2):
            tiles_a.append(_rope(y2[:, t * LANES:(t + 1) * LANES], cos, sin, first_half))

    z_b = _dot(h, wp_ref[:, A_W:A_W + B_W])
    tiles_b = [_rope(z_b[:, j * LANES:(j + 1) * LANES], cos, sin, first_half)
               for j in range(B_W // LANES)]

    z_c = _dot(h, wp_ref[:, A_W + B_W:A_W + B_W + C_W])
    z_g = _dot(h, wp_ref[:, A_W + B_W + C_W:P_W]) + bias_ref[...]

    scale = HEAD_DIM ** -0.5
    qa = jnp.concatenate(tiles_a[0:4], axis=1) * (scale * LOG2E)
    qb = jnp.concatenate(tiles_a[4:8], axis=1) * (scale * LOG2E)
    qi = jnp.concatenate(tiles_b[0:4], axis=1) * scale
    wi_scale = N_HEADS ** -0.5
    if transposed:
        qa_ref[...] = qa.T.astype(BF16)
        qb_ref[...] = qb.T.astype(BF16)
        qi_ref[...] = qi.T.astype(BF16)
        kb_ref[...] = tiles_a[8].T
        ka_ref[...] = tiles_a[9].T[0:HEAD_DIM, :]
        ki_ref[...] = tiles_b[4].T[0:HEAD_DIM, :]
        zc_t = z_c.T
        vb_ref[...] = zc_t[0:LANES, :]
        va_ref[...] = zc_t[LANES:LANES + HEAD_DIM, :]
        wi_ref[...] = zc_t[LANES + HEAD_DIM:LANES + HEAD_DIM + N_HEADS, :] * wi_scale
    else:
        qa_ref[...] = qa.astype(BF16)
        qb_ref[...] = qb.astype(BF16)
        qi_ref[...] = qi.astype(BF16)
        kb_ref[...] = tiles_a[8]
        ka_ref[...] = tiles_a[9][:, 0:HEAD_DIM]
        ki_ref[...] = tiles_b[4][:, 0:HEAD_DIM]
        vb_ref[...] = z_c[:, 0:LANES]
        va_ref[...] = z_c[:, LANES:LANES + HEAD_DIM]
        wi_ref[...] = z_c[:, LANES + HEAD_DIM:LANES + HEAD_DIM + N_HEADS] * wi_scale
    sg = jax.nn.sigmoid(z_g)
    sga_ref[...] = sg[:, 0:D_MODEL]
    sgb_ref[...] = sg[:, D_MODEL:G_W]


def _proj(x, gm, wp, gain, bias, cos_t, sin_t, bd, tm, seq):
    t = x.shape[0]
    n_tab = cos_t.shape[0] // tm
    row = lambda w: pl.BlockSpec((tm, w), lambda i: (i, 0))
    widths = (W_Q, W_Q, W_Q, 2 * HEAD_DIM, HEAD_DIM, HEAD_DIM, 2 * HEAD_DIM, HEAD_DIM, N_HEADS)
    dtypes = (BF16, BF16, BF16, F32, F32, F32, F32, F32, F32)
    if seq is None:
        specs = [row(w) for w in widths]
        shapes = [jax.ShapeDtypeStruct((t, w), d) for w, d in zip(widths, dtypes)]
    else:
        nb = seq // tm
        specs = [pl.BlockSpec((None, w, tm), lambda i: (i // nb, 0, i % nb)) for w in widths]
        shapes = [jax.ShapeDtypeStruct((t // seq, w, seq), d) for w, d in zip(widths, dtypes)]
    out_specs = tuple(specs) + (row(D_MODEL), row(D_MODEL))
    out_shape = tuple(shapes) + (jax.ShapeDtypeStruct((t, D_MODEL), F32), jax.ShapeDtypeStruct((t, D_MODEL), F32))
    transposed = seq is not None
    return pl.pallas_call(
        functools.partial(_proj_kernel, transposed=transposed),
        grid=(t // tm,),
        in_specs=[
            row(D_MODEL),
            _const_spec((1, D_MODEL)),
            _const_spec((D_MODEL, P_W)),
            _const_spec((1, A_W)),
            _const_spec((1, G_W)),
            pl.BlockSpec((tm, LANES), lambda i: (i % n_tab, 0)),
            pl.BlockSpec((tm, LANES), lambda i: (i % n_tab, 0)),
            _const_spec((2 * LANES, 2 * LANES)),
        ],
        out_specs=out_specs,
        out_shape=out_shape,
        compiler_params=pltpu.CompilerParams(
            dimension_semantics=("arbitrary",), vmem_limit_bytes=VMEM_LIMIT),
        name="proj",
    )(x, gm, wp, gain, bias, cos_t, sin_t, bd)


def _count(mask, axis):
    return jnp.sum(jnp.where(mask, 1.0, 0.0), axis=axis, keepdims=True)


def _key_to_score(key):
    return lax.bitcast_convert_type(key ^ ((key >> 31) & np.int32(0x7FFFFFFF)), F32)


N_VALUE_PASSES = 16


def _kth_largest(load_scores, axis, vec_shape, k_sel, side_work=None):
    def value_bits(it, ans):
        if side_work is not None:
            side_work(it)
        shift = 30 - 2 * it
        best = ans
        for digit in (1, 2, 3):
            cand = ans | lax.shift_left(np.int32(digit), shift)
            thr = _key_to_score(cand ^ INT_MIN)
            best = jnp.where(_count(load_scores() >= thr, axis) >= k_sel, cand, best)
        return best

    return _key_to_score(lax.fori_loop(0, N_VALUE_PASSES, value_bits, jnp.zeros(vec_shape, I32)) ^ INT_MIN)


def _score_to_key(score):
    b = lax.bitcast_convert_type(score, I32)
    return b ^ ((b >> 31) & np.int32(0x7FFFFFFF))


def _count_rows16(mask):
    rows, q = mask.shape
    assert rows % 16 == 0 and rows // 16 <= 256
    ones = jnp.where(mask, jnp.ones((), BF16), jnp.zeros((), BF16))
    part = ones[0:16, :]
    for r in range(16, rows, 16):
        part = part + ones[r:r + 16, :]
    return jnp.sum(part.astype(F32), axis=0, keepdims=True)


def _kth_largest_rows(load_scores, coarse_ref, q_cols, k_sel):
    n = load_scores().shape[0]
    coarse_ref[0:n, :] = load_scores().astype(BF16)
    one = np.int32(1)

    def bf16_of(pattern):
        k = pattern - np.int32(0x8000)
        bits = k ^ ((k >> 15) & np.int32(0x7FFF))
        return lax.bitcast_convert_type(lax.shift_left(bits, 16), F32).astype(BF16)

    def coarse_bit(it, ans):
        cand = ans | lax.shift_left(one, 15 - it)
        return jnp.where(_count_rows16(coarse_ref[0:n, :] >= bf16_of(cand)) >= k_sel, cand, ans)

    coarse = lax.fori_loop(0, 16, coarse_bit, jnp.zeros((1, q_cols), I32))
    base = _score_to_key(bf16_of(coarse).astype(F32)) - np.int32(2 ** 15)

    def fine_bit(it, off):
        cand = off | lax.shift_left(one, 16 - it)
        return jnp.where(_count(load_scores() >= _key_to_score(base + cand), 0) >= k_sel, cand, off)

    return _key_to_score(base + lax.fori_loop(0, 17, fine_bit, jnp.zeros((1, q_cols), I32)))


def _topk_bias(store, load_keys, key_index, axis, vec_shape, k_sel, n_index_bits, coarse_ref=None,
               side_work=None):
    one = np.int32(1)
    if coarse_ref is None:
        thr = _kth_largest(load_keys, axis, vec_shape, k_sel, side_work)
    else:
        thr = _kth_largest_rows(load_keys, coarse_ref, vec_shape[1], k_sel)
    has_tie = jnp.max(_count(load_keys() >= thr, axis)) > k_sel

    @pl.when(jnp.logical_not(has_tie))
    def _():
        store(jnp.where(load_keys() >= thr, 0.0, NEG_INF))

    @pl.when(has_tie)
    def _():
        need = k_sel - _count(load_keys() > thr, axis)

        def index_bit(it, cut):
            cand = cut | lax.shift_left(one, n_index_bits - 1 - it)
            hit = jnp.where(load_keys() == thr, jnp.where(key_index() < cand, 1.0, 0.0), 0.0)
            return jnp.where(jnp.sum(hit, axis=axis, keepdims=True) < need, cand, cut)

        cut = lax.fori_loop(0, n_index_bits, index_bit, jnp.zeros(vec_shape, I32))
        keys = load_keys()
        keep_eq = jnp.where(key_index() <= cut, 0.0, NEG_INF)
        store(jnp.where(keys > thr, 0.0, jnp.where(keys == thr, keep_eq, NEG_INF)))


HEADS_PER_TRIP = 2
V_ROWS = HEAD_DIM + 16


def _pipelined_heads(n_chunks, chunk, head_logits, values_t, lgb, store_out):
    rows = lambda c: slice(c * chunk, (c + 1) * chunk)

    def logits_stage(h, slot):
        logits = head_logits(h)
        m = None
        for c in range(n_chunks):
            lg = logits(c)
            lgb[slot, rows(c), :] = lg
            cm = jnp.max(lg, axis=0, keepdims=True)
            m = cm if m is None else jnp.maximum(m, cm)
        return m

    def step(h, m_cur, cur, nxt):
        logits = head_logits(jnp.minimum(h + 1, N_HEADS - 1))
        acc = None
        m_next = None
        for c in range(n_chunks):
            lg = logits(c)
            nxt[rows(c), :] = lg
            cm = jnp.max(lg, axis=0, keepdims=True)
            m_next = cm if m_next is None else jnp.maximum(m_next, cm)
            p = jnp.exp2(cur[rows(c), :] - m_cur).astype(BF16)
            pv = _dot(values_t(h, c), p)
            acc = pv if acc is None else acc + pv
        store_out(h, acc[0:HEAD_DIM, :] / acc[HEAD_DIM:HEAD_DIM + 1, :])
        return m_next

    def body(t, m):
        for k in range(HEADS_PER_TRIP):
            m = step(HEADS_PER_TRIP * t + k, m, lgb.at[k % 2], lgb.at[1 - k % 2])
        return m

    lax.fori_loop(0, N_HEADS // HEADS_PER_TRIP, body, logits_stage(0, 0))


def _attn_prompt_kernel(qi_ref, qa_ref, qb_ref, ki_ref, ka_ref, kb_ref, va_ref, vb_ref, wit,
                        oa_ref, ob_ref,
                        kib, kab, kbb, vat, vbt, kmh, kml, sc, sc16, bias, lgb, ot):
    seq = ki_ref.shape[1]
    tq = MOBA_BLOCK
    n_blk = seq // tq

    kk = jnp.concatenate([ka_ref[...], ki_ref[...]], axis=0).T
    kab[...] = kk[:, 0:HEAD_DIM].astype(BF16)
    kib[...] = kk[:, HEAD_DIM:2 * HEAD_DIM].astype(BF16)
    kb = kb_ref[...].T
    ones = jnp.ones((V_ROWS - HEAD_DIM, seq), BF16)
    vat[...] = jnp.concatenate([va_ref[...].astype(BF16), ones], axis=0)
    means = jnp.concatenate(
        [jnp.mean(kb[j * tq:(j + 1) * tq, :], axis=0, keepdims=True) for j in range(n_blk)]
        + [jnp.zeros((kmh.shape[1] - n_blk, 2 * HEAD_DIM), F32)], axis=0)
    for n in range(HKV_B):
        kbb[n] = kb[:, n * HEAD_DIM:(n + 1) * HEAD_DIM].astype(BF16)
        vbt[n] = jnp.concatenate([vb_ref[n * HEAD_DIM:(n + 1) * HEAD_DIM, :].astype(BF16), ones], axis=0)
        hi, lo = _split_bf16(means[:, n * HEAD_DIM:(n + 1) * HEAD_DIM])
        kmh[n] = hi
        kml[n] = lo

    r_loc = lax.broadcasted_iota(I32, (tq, tq), 0)
    c_loc = lax.broadcasted_iota(I32, (tq, tq), 1)
    causal = r_loc <= c_loc

    for i in range(n_blk):
        c0 = i * tq
        lk = c0 + tq
        cols = slice(c0, c0 + tq)

        sc[0:lk, :] = jnp.zeros((lk, tq), F32)

        def idx_head(h, carry):
            off = pl.multiple_of(h * HEAD_DIM, HEAD_DIM)
            s = _dot(kib[0:lk, :], qi_ref[pl.ds(off, HEAD_DIM), cols])
            sc[0:lk, :] += jnp.maximum(s, 0.0) * wit[pl.ds(h, 1), cols]
            return carry

        lax.fori_loop(0, N_HEADS, idx_head, 0)

        if i == 0:
            bias[0:tq, :] = jnp.where(causal, 0.0, NEG_INF)
        else:
            sc[c0:lk, :] = jnp.where(causal, sc[c0:lk, :], NEG_INF)

            def store_bias(v):
                bias[0:lk, :] = v

            _topk_bias(store_bias, lambda: sc[0:lk, :], lambda: lax.broadcasted_iota(I32, (lk, tq), 0),
                       0, (1, tq), TOPK_A, (lk - 1).bit_length(), coarse_ref=sc16)

        blk = lambda c: slice(c * tq, (c + 1) * tq)

        def store_head(h, o):
            ot[pl.ds(pl.multiple_of(h * HEAD_DIM, HEAD_DIM), HEAD_DIM), :] = o

        def dsa_logits(h):
            q_t = qa_ref[pl.ds(pl.multiple_of(h * HEAD_DIM, HEAD_DIM), HEAD_DIM), cols]
            return lambda c: _dot(kab[blk(c), :], q_t) + bias[blk(c), :]

        _pipelined_heads(i + 1, tq, dsa_logits, lambda h, c: vat[:, blk(c)], lgb, store_head)
        oa_ref[cols, :] = ot[...].T.astype(BF16)

        def moba_logits(h):
            n = h // G_B
            q_t = qb_ref[pl.ds(pl.multiple_of(h * HEAD_DIM, HEAD_DIM), HEAD_DIM), cols]
            block_bias = [None] * i
            if i > TOPK_B:
                gate = _dot(kmh[n], q_t) + _dot(kml[n], q_t)
                rows = lax.broadcasted_iota(I32, gate.shape, 0)
                for j in range(i):
                    gj = gate[j:j + 1, :]
                    beats = jnp.where(rows < j, jnp.where(gate >= gj, 1, 0), jnp.where(gate > gj, 1, 0))
                    rank = jnp.sum(jnp.where(rows < i, beats, 0), axis=0, keepdims=True)
                    block_bias[j] = jnp.where(rank < TOPK_B, 0.0, NEG_INF)

            def logits(c):
                lg = _dot(kbb[n, blk(c), :], q_t)
                if c == i:
                    return jnp.where(causal, lg, NEG_INF)
                return lg if block_bias[c] is None else lg + block_bias[c]

            return logits

        _pipelined_heads(i + 1, tq, moba_logits, lambda h, c: vbt[h // G_B, :, blk(c)], lgb, store_head)
        ob_ref[cols, :] = ot[...].T.astype(BF16)


def _attn_prompt(qi_t, qa_t, qb_t, ki_t, ka_t, kb_t, va_t, vb_t, wi_t):
    n_seq, _, seq = qi_t.shape
    tq = MOBA_BLOCK
    fm = lambda w: pl.BlockSpec((None, w, seq), lambda b: (b, 0, 0))
    row = lambda w: pl.BlockSpec((seq, w), lambda b: (b, 0))
    return pl.pallas_call(
        _attn_prompt_kernel,
        grid=(n_seq,),
        in_specs=[fm(W_Q), fm(W_Q), fm(W_Q), fm(HEAD_DIM), fm(HEAD_DIM), fm(2 * HEAD_DIM),
                  fm(HEAD_DIM), fm(2 * HEAD_DIM), fm(N_HEADS)],
        out_specs=(row(W_Q), row(W_Q)),
        out_shape=(jax.ShapeDtypeStruct((n_seq * seq, W_Q), BF16),
                   jax.ShapeDtypeStruct((n_seq * seq, W_Q), BF16)),
        scratch_shapes=[
            pltpu.VMEM((seq, HEAD_DIM), BF16),
            pltpu.VMEM((seq, HEAD_DIM), BF16),
            pltpu.VMEM((HKV_B, seq, HEAD_DIM), BF16),
            pltpu.VMEM((V_ROWS, seq), BF16),
            pltpu.VMEM((HKV_B, V_ROWS, seq), BF16),
            pltpu.VMEM((HKV_B, 16, HEAD_DIM), BF16),
            pltpu.VMEM((HKV_B, 16, HEAD_DIM), BF16),
            pltpu.VMEM((seq, tq), F32),
            pltpu.VMEM((seq, tq), BF16),
            pltpu.VMEM((seq, tq), F32),
            pltpu.VMEM((2, seq, tq), F32),
            pltpu.VMEM((W_Q, tq), F32),
        ],
        compiler_params=pltpu.CompilerParams(
            dimension_semantics=("arbitrary",), vmem_limit_bytes=VMEM_LIMIT),
        name="attn_prompt",
    )(qi_t, qa_t, qb_t, ki_t, ka_t, kb_t, va_t, vb_t, wi_t)


SAMPLE_GROUP = 4


def _attn_sample_kernel(pt_ref, cki_hbm, cka_hbm, cva_hbm, ckb_hbm, cvb_hbm,
                        new_ref,
                        qi_ref, qa_ref, qb_ref, wc_ref,
                        oa_ref, ob_ref,
                        ski, ska, sva, skb, svb, scores, bias, sem,
                        *, group, n_new, past):
    g = pl.program_id(0)
    n_groups = pl.num_programs(0)
    n_pages = past // PAGE
    lpad = ski.shape[3]
    slot = g % 2
    pools = ((cki_hbm, ski), (cka_hbm, ska), (cva_hbm, sva), (ckb_hbm, skb), (cvb_hbm, svb))

    def page_copies(grp, to_slot, t):
        s = t // n_pages
        p = t % n_pages
        page = pt_ref[grp * group + s, p]
        lanes = pl.ds(pl.multiple_of(p * PAGE, PAGE), PAGE)
        return [pltpu.make_async_copy(hbm.at[page], slab.at[to_slot, s, :, lanes], sem.at[to_slot])
                for hbm, slab in pools]

    def start_group(grp, to_slot):
        def body(t, carry):
            for cp in page_copies(grp, to_slot, t):
                cp.start()
            return carry
        lax.fori_loop(0, group * n_pages, body, 0)

    def wait_group(to_slot):
        for _, slab in pools:
            filled = slab.at[to_slot, :, :, 0:past]
            pltpu.make_async_copy(filled, filled, sem.at[to_slot]).wait()

    @pl.when(g == 0)
    def _():
        start_group(0, 0)

    wait_group(slot)

    n_starts = group * n_pages
    assert n_starts % N_VALUE_PASSES == 0

    def prefetch_next(it):
        for k in range(n_starts // N_VALUE_PASSES):
            for cp in page_copies(g + 1, 1 - slot, it * (n_starts // N_VALUE_PASSES) + k):
                cp.start()

    n_blk = past // MOBA_BLOCK
    col = lax.broadcasted_iota(I32, (8, lpad), 1)
    qrow = lax.broadcasted_iota(I32, (8, lpad), 0)
    visible8 = col <= past + jnp.minimum(qrow, n_new - 1)
    for s in range(group):
        r0 = 0
        for _, slab in pools:
            w = slab.shape[2]
            slab[slot, s, :, past:lpad] = jnp.zeros((w, lpad - past), F32)
            slab[slot, s, :, past:past + n_new] = new_ref[s, r0:r0 + w, :]
            r0 += w
    st = jnp.concatenate([_dot(qi_ref[s], ski[slot, s].astype(BF16)) for s in range(group)], axis=0)
    st = jnp.maximum(st, 0.0) * jnp.concatenate([wc_ref[s] for s in range(group)], axis=0)
    for s in range(group):
        per_q = [jnp.sum(st[(s * n_new + q) * N_HEADS:(s * n_new + q + 1) * N_HEADS, :], axis=0, keepdims=True)
                 for q in range(n_new)]
        score = jnp.concatenate(per_q + [per_q[-1]] * (8 - n_new), axis=0)
        scores[s * 8:(s + 1) * 8, :] = jnp.where(visible8, score, NEG_INF)

    rows = group * 8

    def store_bias(v):
        bias[...] = v

    for has_next, side_work in ((g + 1 < n_groups, prefetch_next), (g + 1 == n_groups, None)):
        @pl.when(has_next)
        def _():
            _topk_bias(store_bias, lambda: scores[...], lambda: lax.broadcasted_iota(I32, (rows, lpad), 1),
                       1, (rows, 1), TOPK_A, (lpad - 1).bit_length(), side_work=side_work)

    n_rows = n_new * N_HEADS
    all_rows = group * n_rows
    seq_rows = lambda s: slice(s * n_rows, (s + 1) * n_rows)

    def softmax_rows(lg):
        m = jnp.max(lg, axis=1, keepdims=True)
        p = jnp.exp2(lg - m)
        return p.astype(BF16), jnp.sum(p, axis=1, keepdims=True)

    lg = jnp.concatenate([_dot(qa_ref[s], ska[slot, s].astype(BF16)) for s in range(group)], axis=0)
    mask_rows = jnp.concatenate(
        [jnp.broadcast_to(bias[s * 8 + q:s * 8 + q + 1, :], (N_HEADS, lpad))
         for s in range(group) for q in range(n_new)], axis=0)
    p, l = softmax_rows(lg + mask_rows)
    for s in range(group):
        oa_ref[s] = _dot_nt(p[seq_rows(s), :], sva[slot, s].astype(BF16)) / l[seq_rows(s), :]

    lgb = jnp.concatenate([_dot(qb_ref[s], skb[slot, s].astype(BF16)) for s in range(group)], axis=0)
    gate = [jnp.sum(lgb[:, j * MOBA_BLOCK:(j + 1) * MOBA_BLOCK], axis=1, keepdims=True) for j in range(n_blk)]
    pieces = []
    for j in range(n_blk):
        rank = jnp.zeros((all_rows, 1), I32)
        for j2 in range(n_blk):
            if j2 != j:
                rank = rank + jnp.where((gate[j2] >= gate[j]) if j2 < j else (gate[j2] > gate[j]), 1, 0)
        pieces.append(lgb[:, j * MOBA_BLOCK:(j + 1) * MOBA_BLOCK] + jnp.where(rank < TOPK_B, 0.0, NEG_INF))
    colr = lax.broadcasted_iota(I32, (all_rows, LANES), 1)
    query = (lax.broadcasted_iota(I32, (all_rows, LANES), 0) % n_rows) // N_HEADS
    pieces.append(jnp.where(colr <= query, lgb[:, past:lpad], NEG_INF))
    p, l = softmax_rows(jnp.concatenate(pieces, axis=1))
    for s in range(group):
        ob_ref[s] = _dot_nt(p[seq_rows(s), :], svb[slot, s].astype(BF16)) / l[seq_rows(s), :]


def _attn_sample(page_table, cki, cka, cva, ckb, cvb, new_rows, qi, qa, qb2, wcol, group):
    n_seq, n_pages = page_table.shape
    n_new = new_rows.shape[2]
    new_w = new_rows.shape[1]
    assert new_w == 3 * HEAD_DIM + 2 * 2 * HEAD_DIM
    n_rows = n_new * N_HEADS
    assert n_rows % 16 == 0
    past = n_pages * PAGE
    lpad = past + LANES
    hbm = pl.BlockSpec(memory_space=pl.ANY)
    grp = lambda a, b: pl.BlockSpec((group, a, b), lambda g, pt: (g, 0, 0))
    slab = lambda w: pltpu.VMEM((2, group, w, lpad), F32)
    grid_spec = pltpu.PrefetchScalarGridSpec(
        num_scalar_prefetch=1,
        grid=(n_seq // group,),
        in_specs=[hbm, hbm, hbm, hbm, hbm,
                  grp(new_w, n_new),
                  grp(n_rows, HEAD_DIM), grp(n_rows, HEAD_DIM), grp(n_rows, 2 * HEAD_DIM), grp(n_rows, 1)],
        out_specs=(grp(n_rows, HEAD_DIM), grp(n_rows, 2 * HEAD_DIM)),
        scratch_shapes=[
            slab(HEAD_DIM),
            slab(HEAD_DIM),
            slab(HEAD_DIM),
            slab(2 * HEAD_DIM),
            slab(2 * HEAD_DIM),
            pltpu.VMEM((group * 8, lpad), F32),
            pltpu.VMEM((group * 8, lpad), F32),
            pltpu.SemaphoreType.DMA((2,)),
        ],
    )
    return pl.pallas_call(
        functools.partial(_attn_sample_kernel, group=group, n_new=n_new, past=past),
        grid_spec=grid_spec,
        out_shape=(jax.ShapeDtypeStruct((n_seq, n_rows, HEAD_DIM), F32),
                   jax.ShapeDtypeStruct((n_seq, n_rows, 2 * HEAD_DIM), F32)),
        compiler_params=pltpu.CompilerParams(
            dimension_semantics=("arbitrary",), vmem_limit_bytes=VMEM_LIMIT),
        name="attn_sample",
    )(page_table, cki, cka, cva, ckb, cvb, new_rows, qi, qa, qb2, wcol)


def _merge_ffn_kernel(x_ref, oa_ref, ob_ref, sga_ref, sgb_ref, wa_ref, wb_ref, wo_ref,
                      g_ref, wg_ref, wu_ref, wf_ref, o_ref):
    m = sga_ref[...] * _dot(oa_ref[...], wa_ref[...]) + sgb_ref[...] * _dot(ob_ref[...], wb_ref[...])
    x = x_ref[...] + _dot(m.astype(BF16), wo_ref[...])
    o_ref[...] = _half_swiglu(x, g_ref, wg_ref, wu_ref, wf_ref)


def _merge_ffn(x, oa, ob, sga, sgb, wa, wb, wo, g, wg, wu, wf, tm):
    t = x.shape[0]
    row = lambda w: pl.BlockSpec((tm, w), lambda i: (i, 0))
    return pl.pallas_call(
        _merge_ffn_kernel,
        grid=(t // tm,),
        in_specs=[row(D_MODEL), row(W_Q), row(W_Q), row(D_MODEL), row(D_MODEL),
                  _const_spec((W_Q, D_MODEL)), _const_spec((W_Q, D_MODEL)), _const_spec((D_MODEL, D_MODEL)),
                  _const_spec((1, D_MODEL)), _const_spec((D_MODEL, D_FF_PAD)), _const_spec((D_MODEL, D_FF_PAD)),
                  _const_spec((D_FF_PAD, D_MODEL))],
        out_specs=row(D_MODEL),
        out_shape=jax.ShapeDtypeStruct((t, D_MODEL), F32),
        compiler_params=pltpu.CompilerParams(
            dimension_semantics=("arbitrary",), vmem_limit_bytes=VMEM_LIMIT),
        name="merge_ffn",
    )(x, oa, ob, sga, sgb, wa, wb, wo, g, wg, wu, wf)


def _ffn_weights(w_in, w_out):
    pad = D_FF_PAD - D_FF
    wg = jnp.pad(w_in[:, :D_FF], ((0, 0), (0, pad))).astype(BF16)
    wu = jnp.pad(w_in[:, D_FF:], ((0, 0), (0, pad))).astype(BF16)
    wo = jnp.pad(w_out, ((0, pad), (0, 0))).astype(BF16)
    return wg, wu, wo


def _rope_tables(pos):
    half = HEAD_DIM // 2
    inv = ROPE_THETA ** (-jnp.arange(half, dtype=F32) / half)
    ang = pos.astype(F32)[:, None] * inv[None, :]
    cos, sin = jnp.cos(ang), jnp.sin(ang)
    return (jnp.concatenate([cos, cos, cos, cos], axis=1),
            jnp.concatenate([-sin, sin, -sin, sin], axis=1))


def _head_rows(a, n_seq, n_new):
    return a.reshape(n_seq, n_new * N_HEADS, a.shape[1] // N_HEADS)


def _from_head_rows(a, n_seq, n_new):
    return a.reshape(n_seq * n_new, N_HEADS * a.shape[2])


def kernel(x_prompt, x_sample, cache_k_a, cache_v_a, cache_kidx_a, cache_k_b, cache_v_b, page_table,
           ffn1_norm, ffn1_w_in, ffn1_w_out, mix_norm, w_in, q_norm_a, k_norm_a, q_norm_b, k_norm_b,
           gate_bias, w_up_a, w_up_b, w_out, ffn2_norm, ffn2_w_in, ffn2_w_out):
    n_seq, seq, _ = x_prompt.shape
    n_dec, n_new, _ = x_sample.shape
    depth = ffn1_norm.shape[0]
    n_pool = cache_k_a.shape[1]
    past = page_table.shape[1] * PAGE
    assert depth == 1 and seq % MOBA_BLOCK == 0 and past % MOBA_BLOCK == 0 and n_new <= 8
    assert seq >= 4 * TOPK_A and past + n_new >= 4 * TOPK_A
    l = 0
    tp, ts = n_seq * seq, n_dec * n_new

    xp = x_prompt.reshape(tp, D_MODEL)
    xs = x_sample.reshape(ts, D_MODEL)

    f1 = _ffn_weights(ffn1_w_in[l], ffn1_w_out[l])
    f2 = _ffn_weights(ffn2_w_in[l], ffn2_w_out[l])
    w = w_in[l]
    o = np.cumsum([0, W_Q, HEAD_DIM, HEAD_DIM, W_Q, HEAD_DIM, N_HEADS, W_Q, 2 * HEAD_DIM, 2 * HEAD_DIM,
                   D_MODEL, D_MODEL])
    c = lambda k: w[:, o[k]:o[k + 1]]
    z = lambda n: jnp.zeros((D_MODEL, n), F32)
    wp = jnp.concatenate([c(0), c(6), c(7), c(1), z(64), c(3), c(4), z(64), c(8), c(2), c(5), z(56),
                          c(9), c(10)], axis=1).astype(BF16)
    gain = jnp.concatenate([jnp.tile(q_norm_a[l], N_HEADS), jnp.tile(q_norm_b[l], N_HEADS),
                            jnp.tile(k_norm_b[l], HKV_B), k_norm_a[l], jnp.zeros((64,), F32)])[None, :]
    bias = gate_bias[l][None, :]
    lane = np.arange(2 * LANES)
    bd =jnp.asarray((lane[:, None] // HEAD_DIM) == (lane[None, :] // HEAD_DIM), BF16)
    cos_p, sin_p = _rope_tables(jnp.arange(seq, dtype=I32))
    cos_s, sin_s = _rope_tables(past + (jnp.arange(ts, dtype=I32) % n_new))
    wa, wb, wo = w_up_a[l].astype(BF16), w_up_b[l].astype(BF16), w_out[l].astype(BF16)
    g1, gm, g2 = ffn1_norm[l][None, :], mix_norm[l][None, :], ffn2_norm[l][None, :]

    tm = 512
    xp1 = _ffn(xp, g1, *f1, tm)
    (qa_t, qb_t, qi_t, kb_t, ka_t, ki_t, vb_t, va_t, wi_t, sga_p, sgb_p) = _proj(
        xp1, gm, wp, gain, bias, cos_p, sin_p, bd, tm, seq)
    oa_p, ob_p = _attn_prompt(qi_t, qa_t, qb_t, ki_t, ka_t, kb_t, va_t, vb_t, wi_t)
    yp = _merge_ffn(xp1, oa_p, ob_p, sga_p, sgb_p, wa, wb, wo, g2, *f2, tm)

    xs1 = _ffn(xs, g1, *f1, ts)
    (qa_s, qb_s, qi_s, kb_s, ka_s, ki_s, vb_s, va_s, wi_s, sga_s, sgb_s) = _proj(
        xs1, gm, wp, gain, bias, cos_s, sin_s, bd, ts, None)
    qi_r = _head_rows(qi_s, n_dec, n_new)
    qa_r = _head_rows(qa_s, n_dec, n_new)
    qb_r = _head_rows(qb_s, n_dec, n_new)
    zq = jnp.zeros_like(qb_r)
    first = ((jnp.arange(n_new * N_HEADS) % N_HEADS) < G_B)[None, :, None]
    qb2 = jnp.concatenate([jnp.where(first, qb_r, zq), jnp.where(first, zq, qb_r)], axis=2)
    wcol = _head_rows(wi_s, n_dec, n_new)
    pages_t = lambda a: jnp.swapaxes(a[l].reshape(n_pool, PAGE, -1), 1, 2)
    new_rows = jnp.concatenate([ki_s, ka_s, va_s, kb_s, vb_s], axis=1)
    new_rows = jnp.swapaxes(new_rows.reshape(n_dec, n_new, new_rows.shape[1]), 1, 2)
    oa_r, ob_r = _attn_sample(
        page_table,
        pages_t(cache_kidx_a), pages_t(cache_k_a), pages_t(cache_v_a), pages_t(cache_k_b), pages_t(cache_v_b),
        new_rows, qi_r, qa_r, qb2, wcol, SAMPLE_GROUP)
    oa_s = _from_head_rows(oa_r, n_dec, n_new).astype(BF16)
    ob_sel = jnp.where(first, ob_r[:, :, :HEAD_DIM], ob_r[:, :, HEAD_DIM:])
    ob_s = _from_head_rows(ob_sel, n_dec, n_new).astype(BF16)
    ys = _merge_ffn(xs1, oa_s, ob_s, sga_s, sgb_s, wa, wb, wo, g2, *f2, ts)

    d = depth
    tok = lambda a: jnp.swapaxes(a, 1, 2)
    return (yp.reshape(n_seq, seq, D_MODEL), ys.reshape(n_dec, n_new, D_MODEL),
            tok(ka_t).reshape(d, n_seq, seq, 1, HEAD_DIM), tok(va_t).reshape(d, n_seq, seq, 1, HEAD_DIM),
            tok(ki_t).reshape(d, n_seq, seq, HEAD_DIM),
            tok(kb_t).reshape(d, n_seq, seq, HKV_B, HEAD_DIM), tok(vb_t).reshape(d, n_seq, seq, HKV_B, HEAD_DIM),
            ka_s.reshape(d, n_dec, n_new, 1, HEAD_DIM), va_s.reshape(d, n_dec, n_new, 1, HEAD_DIM),
            ki_s.reshape(d, n_dec, n_new, HEAD_DIM),
            kb_s.reshape(d, n_dec, n_new, HKV_B, HEAD_DIM), vb_s.reshape(d, n_dec, n_new, HKV_B, HEAD_DIM))
```

```python
import functools

import numpy as np
import jax
import jax.numpy as jnp
from jax import lax
from jax.experimental import pallas as pl
from jax.experimental.pallas import tpu as pltpu

F32 = jnp.float32
BF16 = jnp.bfloat16
I32 = jnp.int32

D_MODEL = 1024
HEAD_DIM = 64
N_HEADS = 8
HKV_B = 2
G_B = N_HEADS // HKV_B
D_FF = 2752
TOPK_A = 256
MOBA_BLOCK = 256
TOPK_B = 3
PAGE = 128
ROPE_THETA = 10000.0
EPS = 1e-6

LANES = 128
D_FF_PAD = 2816
FF_CHUNK = 512
W_Q = N_HEADS * HEAD_DIM

A_W = 1280
B_W = 640
C_W = 256
G_W = 2 * D_MODEL
P_W = A_W + B_W + C_W + G_W

LOG2E = 1.4426950408889634
INT_MIN = np.int32(-2 ** 31)
NEG_INF = float("-inf")
VMEM_LIMIT = 56 * 1024 * 1024


def _dot(a, b):
    return jnp.dot(a, b, preferred_element_type=F32)


def _dot_nt(a, b):
    return lax.dot_general(a, b, (((1,), (1,)), ((), ())), preferred_element_type=F32)


def _split_bf16(x):
    hi = x.astype(BF16)
    lo = (x - hi.astype(F32)).astype(BF16)
    return hi, lo


def _rms(x, g):
    ms = jnp.mean(x * x, axis=-1, keepdims=True)
    return x * lax.rsqrt(ms + EPS) * g


def _const_spec(shape):
    nd = len(shape)
    return pl.BlockSpec(shape, lambda *_: (0,) * nd, pipeline_mode=pl.Buffered(1))


def _half_swiglu(x, g_ref, wg_ref, wu_ref, wo_ref):
    h = _rms(x, g_ref[...]).astype(BF16)
    acc = jnp.zeros_like(x)
    for s in range(0, D_FF_PAD, FF_CHUNK):
        e = min(s + FF_CHUNK, D_FF_PAD)
        g = _dot(h, wg_ref[:, s:e])
        u = _dot(h, wu_ref[:, s:e])
        a = (g * jax.nn.sigmoid(g) * u).astype(BF16)
        acc = acc + _dot(a, wo_ref[s:e, :])
    return x + 0.5 * acc


def _ffn_kernel(x_ref, g_ref, wg_ref, wu_ref, wo_ref, o_ref):
    o_ref[...] = _half_swiglu(x_ref[...], g_ref, wg_ref, wu_ref, wo_ref)


def _ffn(x, g, wg, wu, wo, tm):
    t = x.shape[0]
    return pl.pallas_call(
        _ffn_kernel,
        grid=(t // tm,),
        in_specs=[
            pl.BlockSpec((tm, D_MODEL), lambda i: (i, 0)),
            _const_spec((1, D_MODEL)),
            _const_spec((D_MODEL, D_FF_PAD)),
            _const_spec((D_MODEL, D_FF_PAD)),
            _const_spec((D_FF_PAD, D_MODEL)),
        ],
        out_specs=pl.BlockSpec((tm, D_MODEL), lambda i: (i, 0)),
        out_shape=jax.ShapeDtypeStruct((t, D_MODEL), F32),
        compiler_params=pltpu.CompilerParams(
            dimension_semantics=("arbitrary",), vmem_limit_bytes=VMEM_LIMIT),
        name="ffn",
    )(x, g, wg, wu, wo)


def _rope(y, cos, sin, first_half):
    r_lo = pltpu.roll(y, 32, 1)
    r_hi = pltpu.roll(y, 96, 1)
    return y * cos + jnp.where(first_half, r_hi, r_lo) * sin


def _proj_kernel(x_ref, gm_ref, wp_ref, gain_ref, bias_ref, cos_ref, sin_ref, bd_ref,
                 qa_ref, qb_ref, qi_ref, kb_ref, ka_ref, ki_ref, vb_ref, va_ref, wi_ref,
                 sga_ref, sgb_ref, *, transposed):
    x = x_ref[...]
    tm = x.shape[0]
    h = _rms(x, gm_ref[...]).astype(BF16)
    cos = cos_ref[...]
    sin = sin_ref[...]
    bd = bd_ref[...]
    lane = lax.broadcasted_iota(I32, (tm, LANES), 1)
    first_half = (lane % HEAD_DIM) < (HEAD_DIM // 2)

    z_a = _dot(h, wp_ref[:, 0:A_W])
    tiles_a = []
    for j in range(0, A_W // LANES, 2):
        z2 = z_a[:, j * LANES:(j + 2) * LANES]
        hi, lo = _split_bf16(z2 * z2)
        ms = (_dot(hi, bd) + _dot(lo, bd)) * (1.0 / HEAD_DIM)
        y2 = z2 * lax.rsqrt(ms + EPS) * gain_ref[:, j * LANES:(j + 2) * LANES]
        for t in range(2):
            tiles_a.append(_rope(y2[:, t * LANES:(t + 1) * LANES], cos, sin, first_half))

    z_b = _dot(h, wp_ref[:, A_W:A_W + B_W])
    tiles_b = [_rope(z_b[:, j * LANES:(j + 1) * LANES], cos, sin, first_half)
               for j in range(B_W // LANES)]

    z_c = _dot(h, wp_ref[:, A_W + B_W:A_W + B_W + C_W])
    z_g = _dot(h, wp_ref[:, A_W + B_W + C_W:P_W]) + bias_ref[...]

    scale = HEAD_DIM ** -0.5
    qa = jnp.concatenate(tiles_a[0:4], axis=1) * (scale * LOG2E)
    qb = jnp.concatenate(tiles_a[4:8], axis=1) * (scale * LOG2E)
    qi = jnp.concatenate(tiles_b[0:4], axis=1) * scale
    wi_scale = N_HEADS ** -0.5
    if transposed:
        qa_ref[...] = qa.T.astype(BF16)
        qb_ref[...] = qb.T.astype(BF16)
        qi_ref[...] = qi.T.astype(BF16)
        kb_ref[...] = tiles_a[8].T
        ka_ref[...] = tiles_a[9].T[0:HEAD_DIM, :]
        ki_ref[...] = tiles_b[4].T[0:HEAD_DIM, :]
        zc_t = z_c.T
        vb_ref[...] = zc_t[0:LANES, :]
        va_ref[...] = zc_t[LANES:LANES + HEAD_DIM, :]
        wi_ref[...] = zc_t[LANES + HEAD_DIM:LANES + HEAD_DIM + N_HEADS, :] * wi_scale
    else:
        qa_ref[...] = qa.astype(BF16)
        qb_ref[...] = qb.astype(BF16)
        qi_ref[...] = qi.astype(BF16)
        kb_ref[...] = tiles_a[8]
        ka_ref[...] = tiles_a[9][:, 0:HEAD_DIM]
        ki_ref[...] = tiles_b[4][:, 0:HEAD_DIM]
        vb_ref[...] = z_c[:, 0:LANES]
        va_ref[...] = z_c[:, LANES:LANES + HEAD_DIM]
        wi_ref[...] = z_c[:, LANES + HEAD_DIM:LANES + HEAD_DIM + N_HEADS] * wi_scale
    sg = jax.nn.sigmoid(z_g)
    sga_ref[...] = sg[:, 0:D_MODEL]
    sgb_ref[...] = sg[:, D_MODEL:G_W]


def _proj(x, gm, wp, gain, bias, cos_t, sin_t, bd, tm, seq):
    t = x.shape[0]
    n_tab = cos_t.shape[0] // tm
    row = lambda w: pl.BlockSpec((tm, w), lambda i: (i, 0))
    widths = (W_Q, W_Q, W_Q, 2 * HEAD_DIM, HEAD_DIM, HEAD_DIM, 2 * HEAD_DIM, HEAD_DIM, N_HEADS)
    dtypes = (BF16, BF16, BF16, F32, F32, F32, F32, F32, F32)
    if seq is None:
        specs = [row(w) for w in widths]
        shapes = [jax.ShapeDtypeStruct((t, w), d) for w, d in zip(widths, dtypes)]
    else:
        nb = seq // tm
        specs = [pl.BlockSpec((None, w, tm), lambda i: (i // nb, 0, i % nb)) for w in widths]
        shapes = [jax.ShapeDtypeStruct((t // seq, w, seq), d) for w, d in zip(widths, dtypes)]
    out_specs = tuple(specs) + (row(D_MODEL), row(D_MODEL))
    out_shape = tuple(shapes) + (jax.ShapeDtypeStruct((t, D_MODEL), F32), jax.ShapeDtypeStruct((t, D_MODEL), F32))
    transposed = seq is not None
    return pl.pallas_call(
        functools.partial(_proj_kernel, transposed=transposed),
        grid=(t // tm,),
        in_specs=[
            row(D_MODEL),
            _const_spec((1, D_MODEL)),
            _const_spec((D_MODEL, P_W)),
            _const_spec((1, A_W)),
            _const_spec((1, G_W)),
            pl.BlockSpec((tm, LANES), lambda i: (i % n_tab, 0)),
            pl.BlockSpec((tm, LANES), lambda i: (i % n_tab, 0)),
            _const_spec((2 * LANES, 2 * LANES)),
        ],
        out_specs=out_specs,
        out_shape=out_shape,
        compiler_params=pltpu.CompilerParams(
            dimension_semantics=("arbitrary",), vmem_limit_bytes=VMEM_LIMIT),
        name="proj",
    )(x, gm, wp, gain, bias, cos_t, sin_t, bd)


def _count(mask, axis):
    return jnp.sum(jnp.where(mask, 1.0, 0.0), axis=axis, keepdims=True)


def _key_to_score(key):
    return lax.bitcast_convert_type(key ^ ((key >> 31) & np.int32(0x7FFFFFFF)), F32)


N_VALUE_PASSES = 16


def _kth_largest(load_scores, axis, vec_shape, k_sel, side_work=None, inline=False):
    def value_bits(it, ans):
        if side_work is not None:
            side_work(it)
        shift = jnp.asarray(30 - 2 * it, I32)
        best = ans
        for digit in (1, 2, 3):
            cand = ans | lax.shift_left(np.int32(digit), shift)
            thr = _key_to_score(cand ^ INT_MIN)
            best = jnp.where(_count(load_scores() >= thr, axis) >= k_sel, cand, best)
        return best

    ans = jnp.zeros(vec_shape, I32)
    if inline:
        for it in range(N_VALUE_PASSES):
            ans = value_bits(it, ans)
    else:
        ans = lax.fori_loop(0, N_VALUE_PASSES, value_bits, ans)
    return _key_to_score(ans ^ INT_MIN)


def _score_to_key(score):
    b = lax.bitcast_convert_type(score, I32)
    return b ^ ((b >> 31) & np.int32(0x7FFFFFFF))


def _count_rows16(mask):
    rows, q = mask.shape
    assert rows % 16 == 0 and rows // 16 <= 256
    ones = jnp.where(mask, jnp.ones((), BF16), jnp.zeros((), BF16))
    part = ones[0:16, :]
    for r in range(16, rows, 16):
        part = part + ones[r:r + 16, :]
    return jnp.sum(part.astype(F32), axis=0, keepdims=True)


def _kth_largest_rows(load_scores, coarse_ref, q_cols, k_sel):
    n = load_scores().shape[0]
    coarse_ref[0:n, :] = load_scores().astype(BF16)
    one = np.int32(1)

    def bf16_of(pattern):
        k = pattern - np.int32(0x8000)
        bits = k ^ ((k >> 15) & np.int32(0x7FFF))
        return lax.bitcast_convert_type(lax.shift_left(bits, 16), F32).astype(BF16)

    def coarse_bit(it, ans):
        cand = ans | lax.shift_left(one, 15 - it)
        return jnp.where(_count_rows16(coarse_ref[0:n, :] >= bf16_of(cand)) >= k_sel, cand, ans)

    coarse = lax.fori_loop(0, 16, coarse_bit, jnp.zeros((1, q_cols), I32))
    base = _score_to_key(bf16_of(coarse).astype(F32)) - np.int32(2 ** 15)

    def fine_bit(it, off):
        cand = off | lax.shift_left(one, 16 - it)
        return jnp.where(_count(load_scores() >= _key_to_score(base + cand), 0) >= k_sel, cand, off)

    return _key_to_score(base + lax.fori_loop(0, 17, fine_bit, jnp.zeros((1, q_cols), I32)))


def _topk_bias(store, load_keys, key_index, axis, vec_shape, k_sel, n_index_bits, coarse_ref=None, thr=None):
    one = np.int32(1)
    if thr is not None:
        pass
    elif coarse_ref is None:
        thr = _kth_largest(load_keys, axis, vec_shape, k_sel)
    else:
        thr = _kth_largest_rows(load_keys, coarse_ref, vec_shape[1], k_sel)
    has_tie = jnp.max(_count(load_keys() >= thr, axis)) > k_sel

    @pl.when(jnp.logical_not(has_tie))
    def _():
        store(jnp.where(load_keys() >= thr, 0.0, NEG_INF))

    @pl.when(has_tie)
    def _():
        need = k_sel - _count(load_keys() > thr, axis)

        def index_bit(it, cut):
            cand = cut | lax.shift_left(one, n_index_bits - 1 - it)
            hit = jnp.where(load_keys() == thr, jnp.where(key_index() < cand, 1.0, 0.0), 0.0)
            return jnp.where(jnp.sum(hit, axis=axis, keepdims=True) < need, cand, cut)

        cut = lax.fori_loop(0, n_index_bits, index_bit, jnp.zeros(vec_shape, I32))
        keys = load_keys()
        keep_eq = jnp.where(key_index() <= cut, 0.0, NEG_INF)
        store(jnp.where(keys > thr, 0.0, jnp.where(keys == thr, keep_eq, NEG_INF)))


HEADS_PER_TRIP = 2
V_ROWS = HEAD_DIM + 16


def _pipelined_heads(n_chunks, chunk, head_logits, values_t, lgb, store_out):
    rows = lambda c: slice(c * chunk, (c + 1) * chunk)

    def logits_stage(h, slot):
        logits = head_logits(h)
        m = None
        for c in range(n_chunks):
            lg = logits(c)
            lgb[slot, rows(c), :] = lg
            cm = jnp.max(lg, axis=0, keepdims=True)
            m = cm if m is None else jnp.maximum(m, cm)
        return m

    def step(h, m_cur, cur, nxt):
        logits = head_logits(jnp.minimum(h + 1, N_HEADS - 1))
        acc = None
        m_next = None
        for c in range(n_chunks):
            lg = logits(c)
            nxt[rows(c), :] = lg
            cm = jnp.max(lg, axis=0, keepdims=True)
            m_next = cm if m_next is None else jnp.maximum(m_next, cm)
            p = jnp.exp2(cur[rows(c), :] - m_cur).astype(BF16)
            pv = _dot(values_t(h, c), p)
            acc = pv if acc is None else acc + pv
        store_out(h, acc[0:HEAD_DIM, :] / acc[HEAD_DIM:HEAD_DIM + 1, :])
        return m_next

    def body(t, m):
        for k in range(HEADS_PER_TRIP):
            m = step(HEADS_PER_TRIP * t + k, m, lgb.at[k % 2], lgb.at[1 - k % 2])
        return m

    lax.fori_loop(0, N_HEADS // HEADS_PER_TRIP, body, logits_stage(0, 0))


def _attn_prompt_kernel(qi_ref, qa_ref, qb_ref, ki_ref, ka_ref, kb_ref, va_ref, vb_ref, wit,
                        oa_ref, ob_ref,
                        kib, kab, kbb, vat, vbt, kmh, kml, sc, sc16, bias, lgb, ot):
    seq = ki_ref.shape[1]
    tq = MOBA_BLOCK
    n_blk = seq // tq

    kk = jnp.concatenate([ka_ref[...], ki_ref[...]], axis=0).T
    kab[...] = kk[:, 0:HEAD_DIM].astype(BF16)
    kib[...] = kk[:, HEAD_DIM:2 * HEAD_DIM].astype(BF16)
    kb = kb_ref[...].T
    ones = jnp.ones((V_ROWS - HEAD_DIM, seq), BF16)
    vat[...] = jnp.concatenate([va_ref[...].astype(BF16), ones], axis=0)
    means = jnp.concatenate(
        [jnp.mean(kb[j * tq:(j + 1) * tq, :], axis=0, keepdims=True) for j in range(n_blk)]
        + [jnp.zeros((kmh.shape[1] - n_blk, 2 * HEAD_DIM), F32)], axis=0)
    for n in range(HKV_B):
        kbb[n] = kb[:, n * HEAD_DIM:(n + 1) * HEAD_DIM].astype(BF16)
        vbt[n] = jnp.concatenate([vb_ref[n * HEAD_DIM:(n + 1) * HEAD_DIM, :].astype(BF16), ones], axis=0)
        hi, lo = _split_bf16(means[:, n * HEAD_DIM:(n + 1) * HEAD_DIM])
        kmh[n] = hi
        kml[n] = lo

    r_loc = lax.broadcasted_iota(I32, (tq, tq), 0)
    c_loc = lax.broadcasted_iota(I32, (tq, tq), 1)
    causal = r_loc <= c_loc

    for i in range(n_blk):
        c0 = i * tq
        lk = c0 + tq
        cols = slice(c0, c0 + tq)

        sc[0:lk, :] = jnp.zeros((lk, tq), F32)

        def idx_head(h, carry):
            off = pl.multiple_of(h * HEAD_DIM, HEAD_DIM)
            s = _dot(kib[0:lk, :], qi_ref[pl.ds(off, HEAD_DIM), cols])
            sc[0:lk, :] += jnp.maximum(s, 0.0) * wit[pl.ds(h, 1), cols]
            return carry

        lax.fori_loop(0, N_HEADS, idx_head, 0)

        if i == 0:
            bias[0:tq, :] = jnp.where(causal, 0.0, NEG_INF)
        else:
            sc[c0:lk, :] = jnp.where(causal, sc[c0:lk, :], NEG_INF)

            def store_bias(v):
                bias[0:lk, :] = v

            _topk_bias(store_bias, lambda: sc[0:lk, :], lambda: lax.broadcasted_iota(I32, (lk, tq), 0),
                       0, (1, tq), TOPK_A, (lk - 1).bit_length(), coarse_ref=sc16)

        blk = lambda c: slice(c * tq, (c + 1) * tq)

        def store_head(h, o):
            ot[pl.ds(pl.multiple_of(h * HEAD_DIM, HEAD_DIM), HEAD_DIM), :] = o

        def dsa_logits(h):
            q_t = qa_ref[pl.ds(pl.multiple_of(h * HEAD_DIM, HEAD_DIM), HEAD_DIM), cols]
            return lambda c: _dot(kab[blk(c), :], q_t) + bias[blk(c), :]

        _pipelined_heads(i + 1, tq, dsa_logits, lambda h, c: vat[:, blk(c)], lgb, store_head)
        oa_ref[cols, :] = ot[...].T.astype(BF16)

        def moba_logits(h):
            n = h // G_B
            q_t = qb_ref[pl.ds(pl.multiple_of(h * HEAD_DIM, HEAD_DIM), HEAD_DIM), cols]
            block_bias = [None] * i
            if i > TOPK_B:
                gate = _dot(kmh[n], q_t) + _dot(kml[n], q_t)
                rows = lax.broadcasted_iota(I32, gate.shape, 0)
                for j in range(i):
                    gj = gate[j:j + 1, :]
                    beats = jnp.where(rows < j, jnp.where(gate >= gj, 1, 0), jnp.where(gate > gj, 1, 0))
                    rank = jnp.sum(jnp.where(rows < i, beats, 0), axis=0, keepdims=True)
                    block_bias[j] = jnp.where(rank < TOPK_B, 0.0, NEG_INF)

            def logits(c):
                lg = _dot(kbb[n, blk(c), :], q_t)
                if c == i:
                    return jnp.where(causal, lg, NEG_INF)
                return lg if block_bias[c] is None else lg + block_bias[c]

            return logits

        _pipelined_heads(i + 1, tq, moba_logits, lambda h, c: vbt[h // G_B, :, blk(c)], lgb, store_head)
        ob_ref[cols, :] = ot[...].T.astype(BF16)


def _attn_prompt(qi_t, qa_t, qb_t, ki_t, ka_t, kb_t, va_t, vb_t, wi_t):
    n_seq, _, seq = qi_t.shape
    tq = MOBA_BLOCK
    fm = lambda w: pl.BlockSpec((None, w, seq), lambda b: (b, 0, 0))
    row = lambda w: pl.BlockSpec((seq, w), lambda b: (b, 0))
    return pl.pallas_call(
        _attn_prompt_kernel,
        grid=(n_seq,),
        in_specs=[fm(W_Q), fm(W_Q), fm(W_Q), fm(HEAD_DIM), fm(HEAD_DIM), fm(2 * HEAD_DIM),
                  fm(HEAD_DIM), fm(2 * HEAD_DIM), fm(N_HEADS)],
        out_specs=(row(W_Q), row(W_Q)),
        out_shape=(jax.ShapeDtypeStruct((n_seq * seq, W_Q), BF16),
                   jax.ShapeDtypeStruct((n_seq * seq, W_Q), BF16)),
        scratch_shapes=[
            pltpu.VMEM((seq, HEAD_DIM), BF16),
            pltpu.VMEM((seq, HEAD_DIM), BF16),
            pltpu.VMEM((HKV_B, seq, HEAD_DIM), BF16),
            pltpu.VMEM((V_ROWS, seq), BF16),
            pltpu.VMEM((HKV_B, V_ROWS, seq), BF16),
            pltpu.VMEM((HKV_B, 16, HEAD_DIM), BF16),
            pltpu.VMEM((HKV_B, 16, HEAD_DIM), BF16),
            pltpu.VMEM((seq, tq), F32),
            pltpu.VMEM((seq, tq), BF16),
            pltpu.VMEM((seq, tq), F32),
            pltpu.VMEM((2, seq, tq), F32),
            pltpu.VMEM((W_Q, tq), F32),
        ],
        compiler_params=pltpu.CompilerParams(
            dimension_semantics=("arbitrary",), vmem_limit_bytes=VMEM_LIMIT),
        name="attn_prompt",
    )(qi_t, qa_t, qb_t, ki_t, ka_t, kb_t, va_t, vb_t, wi_t)


SAMPLE_GROUP = 4


def _attn_sample_kernel(pt_ref, cki_hbm, cka_hbm, cva_hbm, ckb_hbm, cvb_hbm,
                        new_ref,
                        qi_ref, qa_ref, qb_ref, wc_ref,
                        oa_ref, ob_ref,
                        ski, ska, sva, skb, svb, scores, bias, sem,
                        *, group, n_new, past):
    g = pl.program_id(0)
    n_groups = pl.num_programs(0)
    n_pages = past // PAGE
    lpad = ski.shape[3]
    slot = g % 2
    pools = ((cki_hbm, ski), (cka_hbm, ska), (cva_hbm, sva), (ckb_hbm, skb), (cvb_hbm, svb))

    def page_copies(grp, to_slot, t):
        s = t // n_pages
        p = t % n_pages
        page = pt_ref[grp * group + s, p]
        lanes = pl.ds(pl.multiple_of(p * PAGE, PAGE), PAGE)
        return [pltpu.make_async_copy(hbm.at[page], slab.at[to_slot, s, :, lanes], sem.at[to_slot])
                for hbm, slab in pools]

    def start_group(grp, to_slot):
        def body(t, carry):
            for cp in page_copies(grp, to_slot, t):
                cp.start()
            return carry
        lax.fori_loop(0, group * n_pages, body, 0)

    def wait_group(to_slot):
        for _, slab in pools:
            filled = slab.at[to_slot, :, :, 0:past]
            pltpu.make_async_copy(filled, filled, sem.at[to_slot]).wait()

    @pl.when(g == 0)
    def _():
        start_group(0, 0)

    wait_group(slot)

    n_starts = group * n_pages
    assert n_starts % N_VALUE_PASSES == 0

    next_group = jnp.minimum(g + 1, n_groups - 1)

    def prefetch_next(it):
        for k in range(n_starts // N_VALUE_PASSES):
            for cp in page_copies(next_group, 1 - slot, it * (n_starts // N_VALUE_PASSES) + k):
                cp.start()

    n_blk = past // MOBA_BLOCK
    col = lax.broadcasted_iota(I32, (8, lpad), 1)
    qrow = lax.broadcasted_iota(I32, (8, lpad), 0)
    visible8 = col <= past + jnp.minimum(qrow, n_new - 1)
    for s in range(group):
        r0 = 0
        for _, slab in pools:
            w = slab.shape[2]
            slab[slot, s, :, past:lpad] = jnp.zeros((w, lpad - past), F32)
            slab[slot, s, :, past:past + n_new] = new_ref[s, r0:r0 + w, :]
            r0 += w
    st = jnp.concatenate([_dot(qi_ref[s], ski[slot, s].astype(BF16)) for s in range(group)], axis=0)
    st = jnp.maximum(st, 0.0) * jnp.concatenate([wc_ref[s] for s in range(group)], axis=0)
    for s in range(group):
        per_q = [jnp.sum(st[(s * n_new + q) * N_HEADS:(s * n_new + q + 1) * N_HEADS, :], axis=0, keepdims=True)
                 for q in range(n_new)]
        score = jnp.concatenate(per_q + [per_q[-1]] * (8 - n_new), axis=0)
        scores[s * 8:(s + 1) * 8, :] = jnp.where(visible8, score, NEG_INF)

    rows = group * 8
    n_rows = n_new * N_HEADS
    all_rows = group * n_rows
    seq_rows = lambda s: slice(s * n_rows, (s + 1) * n_rows)

    def softmax_rows(lg):
        m = jnp.max(lg, axis=1, keepdims=True)
        p = jnp.exp2(lg - m)
        return p.astype(BF16), jnp.sum(p, axis=1, keepdims=True)

    thr = _kth_largest(lambda: scores[...], 1, (rows, 1), TOPK_A, side_work=prefetch_next, inline=True)

    lgb = jnp.concatenate([_dot(qb_ref[s], skb[slot, s].astype(BF16)) for s in range(group)], axis=0)
    gate = [jnp.sum(lgb[:, j * MOBA_BLOCK:(j + 1) * MOBA_BLOCK], axis=1, keepdims=True) for j in range(n_blk)]
    pieces = []
    for j in range(n_blk):
        rank = jnp.zeros((all_rows, 1), I32)
        for j2 in range(n_blk):
            if j2 != j:
                rank = rank + jnp.where((gate[j2] >= gate[j]) if j2 < j else (gate[j2] > gate[j]), 1, 0)
        pieces.append(lgb[:, j * MOBA_BLOCK:(j + 1) * MOBA_BLOCK] + jnp.where(rank < TOPK_B, 0.0, NEG_INF))
    colr = lax.broadcasted_iota(I32, (all_rows, LANES), 1)
    query = (lax.broadcasted_iota(I32, (all_rows, LANES), 0) % n_rows) // N_HEADS
    pieces.append(jnp.where(colr <= query, lgb[:, past:lpad], NEG_INF))
    p, l = softmax_rows(jnp.concatenate(pieces, axis=1))
    for s in range(group):
        ob_ref[s] = _dot_nt(p[seq_rows(s), :], svb[slot, s].astype(BF16)) / l[seq_rows(s), :]

    def store_bias(v):
        bias[...] = v

    _topk_bias(store_bias, lambda: scores[...], lambda: lax.broadcasted_iota(I32, (rows, lpad), 1),
               1, (rows, 1), TOPK_A, (lpad - 1).bit_length(), thr=thr)
    lg = jnp.concatenate([_dot(qa_ref[s], ska[slot, s].astype(BF16)) for s in range(group)], axis=0)
    mask_rows = jnp.concatenate(
        [jnp.broadcast_to(bias[s * 8 + q:s * 8 + q + 1, :], (N_HEADS, lpad))
         for s in range(group) for q in range(n_new)], axis=0)
    p, l = softmax_rows(lg + mask_rows)
    for s in range(group):
        oa_ref[s] = _dot_nt(p[seq_rows(s), :], sva[slot, s].astype(BF16)) / l[seq_rows(s), :]

    @pl.when(g == n_groups - 1)
    def _():
        wait_group(1 - slot)


def _attn_sample(page_table, cki, cka, cva, ckb, cvb, new_rows, qi, qa, qb2, wcol, group):
    n_seq, n_pages = page_table.shape
    n_new = new_rows.shape[2]
    new_w = new_rows.shape[1]
    assert new_w == 3 * HEAD_DIM + 2 * 2 * HEAD_DIM
    n_rows = n_new * N_HEADS
    assert n_rows % 16 == 0
    past = n_pages * PAGE
    lpad = past + LANES
    hbm = pl.BlockSpec(memory_space=pl.ANY)
    grp = lambda a, b: pl.BlockSpec((group, a, b), lambda g, pt: (g, 0, 0))
    slab = lambda w: pltpu.VMEM((2, group, w, lpad), F32)
    grid_spec = pltpu.PrefetchScalarGridSpec(
        num_scalar_prefetch=1,
        grid=(n_seq // group,),
        in_specs=[hbm, hbm, hbm, hbm, hbm,
                  grp(new_w, n_new),
                  grp(n_rows, HEAD_DIM), grp(n_rows, HEAD_DIM), grp(n_rows, 2 * HEAD_DIM), grp(n_rows, 1)],
        out_specs=(grp(n_rows, HEAD_DIM), grp(n_rows, 2 * HEAD_DIM)),
        scratch_shapes=[
            slab(HEAD_DIM),
            slab(HEAD_DIM),
            slab(HEAD_DIM),
            slab(2 * HEAD_DIM),
            slab(2 * HEAD_DIM),
            pltpu.VMEM((group * 8, lpad), F32),
            pltpu.VMEM((group * 8, lpad), F32),
            pltpu.SemaphoreType.DMA((2,)),
        ],
    )
    return pl.pallas_call(
        functools.partial(_attn_sample_kernel, group=group, n_new=n_new, past=past),
        grid_spec=grid_spec,
        out_shape=(jax.ShapeDtypeStruct((n_seq, n_rows, HEAD_DIM), F32),
                   jax.ShapeDtypeStruct((n_seq, n_rows, 2 * HEAD_DIM), F32)),
        compiler_params=pltpu.CompilerParams(
            dimension_semantics=("arbitrary",), vmem_limit_bytes=VMEM_LIMIT),
        name="attn_sample",
    )(page_table, cki, cka, cva, ckb, cvb, new_rows, qi, qa, qb2, wcol)


def _merge_ffn_kernel(x_ref, oa_ref, ob_ref, sga_ref, sgb_ref, wa_ref, wb_ref, wo_ref,
                      g_ref, wg_ref, wu_ref, wf_ref, o_ref):
    m = sga_ref[...] * _dot(oa_ref[...], wa_ref[...]) + sgb_ref[...] * _dot(ob_ref[...], wb_ref[...])
    x = x_ref[...] + _dot(m.astype(BF16), wo_ref[...])
    o_ref[...] = _half_swiglu(x, g_ref, wg_ref, wu_ref, wf_ref)


def _merge_ffn(x, oa, ob, sga, sgb, wa, wb, wo, g, wg, wu, wf, tm):
    t = x.shape[0]
    row = lambda w: pl.BlockSpec((tm, w), lambda i: (i, 0))
    return pl.pallas_call(
        _merge_ffn_kernel,
        grid=(t // tm,),
        in_specs=[row(D_MODEL), row(W_Q), row(W_Q), row(D_MODEL), row(D_MODEL),
                  _const_spec((W_Q, D_MODEL)), _const_spec((W_Q, D_MODEL)), _const_spec((D_MODEL, D_MODEL)),
                  _const_spec((1, D_MODEL)), _const_spec((D_MODEL, D_FF_PAD)), _const_spec((D_MODEL, D_FF_PAD)),
                  _const_spec((D_FF_PAD, D_MODEL))],
        out_specs=row(D_MODEL),
        out_shape=jax.ShapeDtypeStruct((t, D_MODEL), F32),
        compiler_params=pltpu.CompilerParams(
            dimension_semantics=("arbitrary",), vmem_limit_bytes=VMEM_LIMIT),
        name="merge_ffn",
    )(x, oa, ob, sga, sgb, wa, wb, wo, g, wg, wu, wf)


def _ffn_weights(w_in, w_out):
    pad = D_FF_PAD - D_FF
    wg = jnp.pad(w_in[:, :D_FF], ((0, 0), (0, pad))).astype(BF16)
    wu = jnp.pad(w_in[:, D_FF:], ((0, 0), (0, pad))).astype(BF16)
    wo = jnp.pad(w_out, ((0, pad), (0, 0))).astype(BF16)
    return wg, wu, wo


def _rope_tables(pos):
    half = HEAD_DIM // 2
    inv = ROPE_THETA ** (-jnp.arange(half, dtype=F32) / half)
    ang = pos.astype(F32)[:, None] * inv[None, :]
    cos, sin = jnp.cos(ang), jnp.sin(ang)
    return (jnp.concatenate([cos, cos, cos, cos], axis=1),
            jnp.concatenate([-sin, sin, -sin, sin], axis=1))


def _head_rows(a, n_seq, n_new):
    return a.reshape(n_seq, n_new * N_HEADS, a.shape[1] // N_HEADS)


def _from_head_rows(a, n_seq, n_new):
    return a.reshape(n_seq * n_new, N_HEADS * a.shape[2])


def kernel(x_prompt, x_sample, cache_k_a, cache_v_a, cache_kidx_a, cache_k_b, cache_v_b, page_table,
           ffn1_norm, ffn1_w_in, ffn1_w_out, mix_norm, w_in, q_norm_a, k_norm_a, q_norm_b, k_norm_b,
           gate_bias, w_up_a, w_up_b, w_out, ffn2_norm, ffn2_w_in, ffn2_w_out):
    n_seq, seq, _ = x_prompt.shape
    n_dec, n_new, _ = x_sample.shape
    depth = ffn1_norm.shape[0]
    n_pool = cache_k_a.shape[1]
    past = page_table.shape[1] * PAGE
    assert depth == 1 and seq % MOBA_BLOCK == 0 and past % MOBA_BLOCK == 0 and n_new <= 8
    assert seq >= 4 * TOPK_A and past + n_new >= 4 * TOPK_A
    l = 0
    tp, ts = n_seq * seq, n_dec * n_new

    xp = x_prompt.reshape(tp, D_MODEL)
    xs = x_sample.reshape(ts, D_MODEL)

    f1 = _ffn_weights(ffn1_w_in[l], ffn1_w_out[l])
    f2 = _ffn_weights(ffn2_w_in[l], ffn2_w_out[l])
    w = w_in[l]
    o = np.cumsum([0, W_Q, HEAD_DIM, HEAD_DIM, W_Q, HEAD_DIM, N_HEADS, W_Q, 2 * HEAD_DIM, 2 * HEAD_DIM,
                   D_MODEL, D_MODEL])
    c = lambda k: w[:, o[k]:o[k + 1]]
    z = lambda n: jnp.zeros((D_MODEL, n), F32)
    wp = jnp.concatenate([c(0), c(6), c(7), c(1), z(64), c(3), c(4), z(64), c(8), c(2), c(5), z(56),
                          c(9), c(10)], axis=1).astype(BF16)
    gain = jnp.concatenate([jnp.tile(q_norm_a[l], N_HEADS), jnp.tile(q_norm_b[l], N_HEADS),
                            jnp.tile(k_norm_b[l], HKV_B), k_norm_a[l], jnp.zeros((64,), F32)])[None, :]
    bias = gate_bias[l][None, :]
    lane = np.arange(2 * LANES)
    bd =jnp.asarray((lane[:, None] // HEAD_DIM) == (lane[None, :] // HEAD_DIM), BF16)
    cos_p, sin_p = _rope_tables(jnp.arange(seq, dtype=I32))
    cos_s, sin_s = _rope_tables(past + (jnp.arange(ts, dtype=I32) % n_new))
    wa, wb, wo = w_up_a[l].astype(BF16), w_up_b[l].astype(BF16), w_out[l].astype(BF16)
    g1, gm, g2 = ffn1_norm[l][None, :], mix_norm[l][None, :], ffn2_norm[l][None, :]

    tm = 512
    xp1 = _ffn(xp, g1, *f1, tm)
    (qa_t, qb_t, qi_t, kb_t, ka_t, ki_t, vb_t, va_t, wi_t, sga_p, sgb_p) = _proj(
        xp1, gm, wp, gain, bias, cos_p, sin_p, bd, tm, seq)
    oa_p, ob_p = _attn_prompt(qi_t, qa_t, qb_t, ki_t, ka_t, kb_t, va_t, vb_t, wi_t)
    yp = _merge_ffn(xp1, oa_p, ob_p, sga_p, sgb_p, wa, wb, wo, g2, *f2, tm)

    xs1 = _ffn(xs, g1, *f1, ts)
    (qa_s, qb_s, qi_s, kb_s, ka_s, ki_s, vb_s, va_s, wi_s, sga_s, sgb_s) = _proj(
        xs1, gm, wp, gain, bias, cos_s, sin_s, bd, ts, None)
    qi_r = _head_rows(qi_s, n_dec, n_new)
    qa_r = _head_rows(qa_s, n_dec, n_new)
    qb_r = _head_rows(qb_s, n_dec, n_new)
    zq = jnp.zeros_like(qb_r)
    first = ((jnp.arange(n_new * N_HEADS) % N_HEADS) < G_B)[None, :, None]
    qb2 = jnp.concatenate([jnp.where(first, qb_r, zq), jnp.where(first, zq, qb_r)], axis=2)
    wcol = _head_rows(wi_s, n_dec, n_new)
    pages_t = lambda a: jnp.swapaxes(a[l].reshape(n_pool, PAGE, -1), 1, 2)
    new_rows = jnp.concatenate([ki_s, ka_s, va_s, kb_s, vb_s], axis=1)
    new_rows = jnp.swapaxes(new_rows.reshape(n_dec, n_new, new_rows.shape[1]), 1, 2)
    oa_r, ob_r = _attn_sample(
        page_table,
        pages_t(cache_kidx_a), pages_t(cache_k_a), pages_t(cache_v_a), pages_t(cache_k_b), pages_t(cache_v_b),
        new_rows, qi_r, qa_r, qb2, wcol, SAMPLE_GROUP)
    oa_s = _from_head_rows(oa_r, n_dec, n_new).astype(BF16)
    ob_sel = jnp.where(first, ob_r[:, :, :HEAD_DIM], ob_r[:, :, HEAD_DIM:])
    ob_s = _from_head_rows(ob_sel, n_dec, n_new).astype(BF16)
    ys = _merge_ffn(xs1, oa_s, ob_s, sga_s, sgb_s, wa, wb, wo, g2, *f2, ts)

    d = depth
    tok = lambda a: jnp.swapaxes(a, 1, 2)
    return (yp.reshape(n_seq, seq, D_MODEL), ys.reshape(n_dec, n_new, D_MODEL),
            tok(ka_t).reshape(d, n_seq, seq, 1, HEAD_DIM), tok(va_t).reshape(d, n_seq, seq, 1, HEAD_DIM),
            tok(ki_t).reshape(d, n_seq, seq, HEAD_DIM),
            tok(kb_t).reshape(d, n_seq, seq, HKV_B, HEAD_DIM), tok(vb_t).reshape(d, n_seq, seq, HKV_B, HEAD_DIM),
            ka_s.reshape(d, n_dec, n_new, 1, HEAD_DIM), va_s.reshape(d, n_dec, n_new, 1, HEAD_DIM),
            ki_s.reshape(d, n_dec, n_new, HEAD_DIM),
            kb_s.reshape(d, n_dec, n_new, HKV_B, HEAD_DIM), vb_s.reshape(d, n_dec, n_new, HKV_B, HEAD_DIM))
```

```python
import functools

import numpy as np
import jax
import jax.numpy as jnp
from jax import lax
from jax.experimental import pallas as pl
from jax.experimental.pallas import tpu as pltpu

F32 = jnp.float32
BF16 = jnp.bfloat16
I32 = jnp.int32

D_MODEL = 1024
HEAD_DIM = 64
N_HEADS = 8
HKV_B = 2
G_B = N_HEADS // HKV_B
D_FF = 2752
TOPK_A = 256
MOBA_BLOCK = 256
TOPK_B = 3
PAGE = 128
ROPE_THETA = 10000.0
EPS = 1e-6

LANES = 128
D_FF_PAD = 2816
FF_CHUNK = 512
W_Q = N_HEADS * HEAD_DIM

A_W = 1280
B_W = 640
C_W = 256
G_W = 2 * D_MODEL
P_W = A_W + B_W + C_W + G_W

LOG2E = 1.4426950408889634
INT_MIN = np.int32(-2 ** 31)
NEG_INF = float("-inf")
VMEM_LIMIT = 56 * 1024 * 1024


def _dot(a, b):
    return jnp.dot(a, b, preferred_element_type=F32)


def _dot_nt(a, b):
    return lax.dot_general(a, b, (((1,), (1,)), ((), ())), preferred_element_type=F32)


def _split_bf16(x):
    hi = x.astype(BF16)
    lo = (x - hi.astype(F32)).astype(BF16)
    return hi, lo


def _rms(x, g):
    ms = jnp.mean(x * x, axis=-1, keepdims=True)
    return x * lax.rsqrt(ms + EPS) * g


def _const_spec(shape):
    nd = len(shape)
    return pl.BlockSpec(shape, lambda *_: (0,) * nd, pipeline_mode=pl.Buffered(1))


def _half_swiglu(x, g_ref, wg_ref, wu_ref, wo_ref):
    h = _rms(x, g_ref[...]).astype(BF16)
    acc = jnp.zeros_like(x)
    for s in range(0, D_FF_PAD, FF_CHUNK):
        e = min(s + FF_CHUNK, D_FF_PAD)
        g = _dot(h, wg_ref[:, s:e])
        u = _dot(h, wu_ref[:, s:e])
        a = (g * jax.nn.sigmoid(g) * u).astype(BF16)
        acc = acc + _dot(a, wo_ref[s:e, :])
    return x + 0.5 * acc


def _ffn_kernel(x_ref, g_ref, wg_ref, wu_ref, wo_ref, o_ref):
    o_ref[...] = _half_swiglu(x_ref[...], g_ref, wg_ref, wu_ref, wo_ref)


def _ffn(x, g, wg, wu, wo, tm):
    t = x.shape[0]
    return pl.pallas_call(
        _ffn_kernel,
        grid=(t // tm,),
        in_specs=[
            pl.BlockSpec((tm, D_MODEL), lambda i: (i, 0)),
            _const_spec((1, D_MODEL)),
            _const_spec((D_MODEL, D_FF_PAD)),
            _const_spec((D_MODEL, D_FF_PAD)),
            _const_spec((D_FF_PAD, D_MODEL)),
        ],
        out_specs=pl.BlockSpec((tm, D_MODEL), lambda i: (i, 0)),
        out_shape=jax.ShapeDtypeStruct((t, D_MODEL), F32),
        compiler_params=pltpu.CompilerParams(
            dimension_semantics=("arbitrary",), vmem_limit_bytes=VMEM_LIMIT),
        name="ffn",
    )(x, g, wg, wu, wo)


def _rope(y, cos, sin, first_half):
    r_lo = pltpu.roll(y, 32, 1)
    r_hi = pltpu.roll(y, 96, 1)
    return y * cos + jnp.where(first_half, r_hi, r_lo) * sin


def _proj_kernel(x_ref, gm_ref, wp_ref, gain_ref, bias_ref, cos_ref, sin_ref, bd_ref,
                 qa_ref, qb_ref, qi_ref, kb_ref, ka_ref, ki_ref, vb_ref, va_ref, wi_ref,
                 sga_ref, sgb_ref, *, transposed):
    x = x_ref[...]
    tm = x.shape[0]
    h = _rms(x, gm_ref[...]).astype(BF16)
    cos = cos_ref[...]
    sin = sin_ref[...]
    bd = bd_ref[...]
    lane = lax.broadcasted_iota(I32, (tm, LANES), 1)
    first_half = (lane % HEAD_DIM) < (HEAD_DIM // 2)

    z_a = _dot(h, wp_ref[:, 0:A_W])
    tiles_a = []
    for j in range(0, A_W // LANES, 2):
        z2 = z_a[:, j * LANES:(j + 2) * LANES]
        hi, lo = _split_bf16(z2 * z2)
        ms = (_dot(hi, bd) + _dot(lo, bd)) * (1.0 / HEAD_DIM)
        y2 = z2 * lax.rsqrt(ms + EPS) * gain_ref[:, j * LANES:(j + 2) * LANES]
        for t in range(2):
            tiles_a.append(_rope(y2[:, t * LANES:(t + 1) * LANES], cos, sin, first_half))

    z_b = _dot(h, wp_ref[:, A_W:A_W + B_W])
    tiles_b = [_rope(z_b[:, j * LANES:(j + 1) * LANES], cos, sin, first_half)
               for j in range(B_W // LANES)]

    z_c = _dot(h, wp_ref[:, A_W + B_W:A_W + B_W + C_W])
    z_g = _dot(h, wp_ref[:, A_W + B_W + C_W:P_W]) + bias_ref[...]

    scale = HEAD_DIM ** -0.5
    qa = jnp.concatenate(tiles_a[0:4], axis=1) * (scale * LOG2E)
    qb = jnp.concatenate(tiles_a[4:8], axis=1) * (scale * LOG2E)
    qi = jnp.concatenate(tiles_b[0:4], axis=1) * scale
    wi_scale = N_HEADS ** -0.5
    if transposed:
        qa_ref[...] = qa.T.astype(BF16)
        qb_ref[...] = qb.T.astype(BF16)
        qi_ref[...] = qi.T.astype(BF16)
        kb_ref[...] = tiles_a[8].T
        ka_ref[...] = tiles_a[9].T[0:HEAD_DIM, :]
        ki_ref[...] = tiles_b[4].T[0:HEAD_DIM, :]
        zc_t = z_c.T
        vb_ref[...] = zc_t[0:LANES, :]
        va_ref[...] = zc_t[LANES:LANES + HEAD_DIM, :]
        wi_ref[...] = zc_t[LANES + HEAD_DIM:LANES + HEAD_DIM + N_HEADS, :] * wi_scale
    else:
        qa_ref[...] = qa.astype(BF16)
        qb_ref[...] = qb.astype(BF16)
        qi_ref[...] = qi.astype(BF16)
        kb_ref[...] = tiles_a[8]
        ka_ref[...] = tiles_a[9][:, 0:HEAD_DIM]
        ki_ref[...] = tiles_b[4][:, 0:HEAD_DIM]
        vb_ref[...] = z_c[:, 0:LANES]
        va_ref[...] = z_c[:, LANES:LANES + HEAD_DIM]
        wi_ref[...] = z_c[:, LANES + HEAD_DIM:LANES + HEAD_DIM + N_HEADS] * wi_scale
    sg = jax.nn.sigmoid(z_g)
    sga_ref[...] = sg[:, 0:D_MODEL]
    sgb_ref[...] = sg[:, D_MODEL:G_W]


def _proj(x, gm, wp, gain, bias, cos_t, sin_t, bd, tm, seq):
    t = x.shape[0]
    n_tab = cos_t.shape[0] // tm
    row = lambda w: pl.BlockSpec((tm, w), lambda i: (i, 0))
    widths = (W_Q, W_Q, W_Q, 2 * HEAD_DIM, HEAD_DIM, HEAD_DIM, 2 * HEAD_DIM, HEAD_DIM, N_HEADS)
    dtypes = (BF16, BF16, BF16, F32, F32, F32, F32, F32, F32)
    if seq is None:
        specs = [row(w) for w in widths]
        shapes = [jax.ShapeDtypeStruct((t, w), d) for w, d in zip(widths, dtypes)]
    else:
        nb = seq // tm
        specs = [pl.BlockSpec((None, w, tm), lambda i: (i // nb, 0, i % nb)) for w in widths]
        shapes = [jax.ShapeDtypeStruct((t // seq, w, seq), d) for w, d in zip(widths, dtypes)]
    out_specs = tuple(specs) + (row(D_MODEL), row(D_MODEL))
    out_shape = tuple(shapes) + (jax.ShapeDtypeStruct((t, D_MODEL), F32), jax.ShapeDtypeStruct((t, D_MODEL), F32))
    transposed = seq is not None
    return pl.pallas_call(
        functools.partial(_proj_kernel, transposed=transposed),
        grid=(t // tm,),
        in_specs=[
            row(D_MODEL),
            _const_spec((1, D_MODEL)),
            _const_spec((D_MODEL, P_W)),
            _const_spec((1, A_W)),
            _const_spec((1, G_W)),
            pl.BlockSpec((tm, LANES), lambda i: (i % n_tab, 0)),
            pl.BlockSpec((tm, LANES), lambda i: (i % n_tab, 0)),
            _const_spec((2 * LANES, 2 * LANES)),
        ],
        out_specs=out_specs,
        out_shape=out_shape,
        compiler_params=pltpu.CompilerParams(
            dimension_semantics=("arbitrary",), vmem_limit_bytes=VMEM_LIMIT),
        name="proj",
    )(x, gm, wp, gain, bias, cos_t, sin_t, bd)


def _count(mask, axis):
    return jnp.sum(jnp.where(mask, 1.0, 0.0), axis=axis, keepdims=True)


def _key_to_score(key):
    return lax.bitcast_convert_type(key ^ ((key >> 31) & np.int32(0x7FFFFFFF)), F32)


N_VALUE_PASSES = 16


def _kth_largest(load_scores, axis, vec_shape, k_sel, side_work=None):
    def value_bits(it, ans):
        if side_work is not None:
            side_work(it)
        shift = 30 - 2 * it
        best = ans
        for digit in (1, 2, 3):
            cand = ans | lax.shift_left(np.int32(digit), shift)
            thr = _key_to_score(cand ^ INT_MIN)
            best = jnp.where(_count(load_scores() >= thr, axis) >= k_sel, cand, best)
        return best

    return _key_to_score(lax.fori_loop(0, N_VALUE_PASSES, value_bits, jnp.zeros(vec_shape, I32)) ^ INT_MIN)


def _score_to_key(score):
    b = lax.bitcast_convert_type(score, I32)
    return b ^ ((b >> 31) & np.int32(0x7FFFFFFF))


def _count_rows16(mask):
    rows, q = mask.shape
    assert rows % 16 == 0 and rows // 16 <= 256
    ones = jnp.where(mask, jnp.ones((), BF16), jnp.zeros((), BF16))
    part = ones[0:16, :]
    for r in range(16, rows, 16):
        part = part + ones[r:r + 16, :]
    return jnp.sum(part.astype(F32), axis=0, keepdims=True)


def _kth_largest_rows(load_scores, coarse_ref, q_cols, k_sel):
    n = load_scores().shape[0]
    coarse_ref[0:n, :] = load_scores().astype(BF16)
    one = np.int32(1)

    def bf16_of(pattern):
        k = pattern - np.int32(0x8000)
        bits = k ^ ((k >> 15) & np.int32(0x7FFF))
        return lax.bitcast_convert_type(lax.shift_left(bits, 16), F32).astype(BF16)

    def coarse_bit(it, ans):
        cand = ans | lax.shift_left(one, 15 - it)
        return jnp.where(_count_rows16(coarse_ref[0:n, :] >= bf16_of(cand)) >= k_sel, cand, ans)

    coarse = lax.fori_loop(0, 16, coarse_bit, jnp.zeros((1, q_cols), I32))
    base = _score_to_key(bf16_of(coarse).astype(F32)) - np.int32(2 ** 15)

    def fine_bit(it, off):
        cand = off | lax.shift_left(one, 16 - it)
        return jnp.where(_count(load_scores() >= _key_to_score(base + cand), 0) >= k_sel, cand, off)

    return _key_to_score(base + lax.fori_loop(0, 17, fine_bit, jnp.zeros((1, q_cols), I32)))


def _topk_bias(store, load_keys, key_index, axis, vec_shape, k_sel, n_index_bits, coarse_ref=None,
               side_work=None):
    one = np.int32(1)
    if coarse_ref is None:
        thr = _kth_largest(load_keys, axis, vec_shape, k_sel, side_work)
    else:
        thr = _kth_largest_rows(load_keys, coarse_ref, vec_shape[1], k_sel)
    has_tie = jnp.max(_count(load_keys() >= thr, axis)) > k_sel

    @pl.when(jnp.logical_not(has_tie))
    def _():
        store(jnp.where(load_keys() >= thr, 0.0, NEG_INF))

    @pl.when(has_tie)
    def _():
        need = k_sel - _count(load_keys() > thr, axis)

        def index_bit(it, cut):
            cand = cut | lax.shift_left(one, n_index_bits - 1 - it)
            hit = jnp.where(load_keys() == thr, jnp.where(key_index() < cand, 1.0, 0.0), 0.0)
            return jnp.where(jnp.sum(hit, axis=axis, keepdims=True) < need, cand, cut)

        cut = lax.fori_loop(0, n_index_bits, index_bit, jnp.zeros(vec_shape, I32))
        keys = load_keys()
        keep_eq = jnp.where(key_index() <= cut, 0.0, NEG_INF)
        store(jnp.where(keys > thr, 0.0, jnp.where(keys == thr, keep_eq, NEG_INF)))


HEADS_PER_TRIP = 2
V_ROWS = HEAD_DIM + 16


def _pipelined_heads(n_chunks, chunk, head_logits, values_t, lgb, store_out):
    rows = lambda c: slice(c * chunk, (c + 1) * chunk)

    def logits_stage(h, slot):
        logits = head_logits(h)
        m = None
        for c in range(n_chunks):
            lg = logits(c)
            lgb[slot, rows(c), :] = lg
            cm = jnp.max(lg, axis=0, keepdims=True)
            m = cm if m is None else jnp.maximum(m, cm)
        return m

    def step(h, m_cur, cur, nxt):
        logits = head_logits(jnp.minimum(h + 1, N_HEADS - 1))
        acc = None
        m_next = None
        for c in range(n_chunks):
            lg = logits(c)
            nxt[rows(c), :] = lg
            cm = jnp.max(lg, axis=0, keepdims=True)
            m_next = cm if m_next is None else jnp.maximum(m_next, cm)
            p = jnp.exp2(cur[rows(c), :] - m_cur).astype(BF16)
            pv = _dot(values_t(h, c), p)
            acc = pv if acc is None else acc + pv
        store_out(h, acc[0:HEAD_DIM, :] / acc[HEAD_DIM:HEAD_DIM + 1, :])
        return m_next

    def body(t, m):
        for k in range(HEADS_PER_TRIP):
            m = step(HEADS_PER_TRIP * t + k, m, lgb.at[k % 2], lgb.at[1 - k % 2])
        return m

    lax.fori_loop(0, N_HEADS // HEADS_PER_TRIP, body, logits_stage(0, 0))


def _attn_prompt_kernel(qi_ref, qa_ref, qb_ref, ki_ref, ka_ref, kb_ref, va_ref, vb_ref, wit,
                        oa_ref, ob_ref,
                        kib, kab, kbb, vat, vbt, kmh, kml, sc, sc16, bias, lgb, ot):
    seq = ki_ref.shape[1]
    tq = MOBA_BLOCK
    n_blk = seq // tq

    kk = jnp.concatenate([ka_ref[...], ki_ref[...]], axis=0).T
    kab[...] = kk[:, 0:HEAD_DIM].astype(BF16)
    kib[...] = kk[:, HEAD_DIM:2 * HEAD_DIM].astype(BF16)
    kb = kb_ref[...].T
    ones = jnp.ones((V_ROWS - HEAD_DIM, seq), BF16)
    vat[...] = jnp.concatenate([va_ref[...].astype(BF16), ones], axis=0)
    means = jnp.concatenate(
        [jnp.mean(kb[j * tq:(j + 1) * tq, :], axis=0, keepdims=True) for j in range(n_blk)]
        + [jnp.zeros((kmh.shape[1] - n_blk, 2 * HEAD_DIM), F32)], axis=0)
    for n in range(HKV_B):
        kbb[n] = kb[:, n * HEAD_DIM:(n + 1) * HEAD_DIM].astype(BF16)
        vbt[n] = jnp.concatenate([vb_ref[n * HEAD_DIM:(n + 1) * HEAD_DIM, :].astype(BF16), ones], axis=0)
        hi, lo = _split_bf16(means[:, n * HEAD_DIM:(n + 1) * HEAD_DIM])
        kmh[n] = hi
        kml[n] = lo

    r_loc = lax.broadcasted_iota(I32, (tq, tq), 0)
    c_loc = lax.broadcasted_iota(I32, (tq, tq), 1)
    causal = r_loc <= c_loc

    for i in range(n_blk):
        c0 = i * tq
        lk = c0 + tq
        cols = slice(c0, c0 + tq)

        sc[0:lk, :] = jnp.zeros((lk, tq), F32)

        def idx_head(h, carry):
            off = pl.multiple_of(h * HEAD_DIM, HEAD_DIM)
            s = _dot(kib[0:lk, :], qi_ref[pl.ds(off, HEAD_DIM), cols])
            sc[0:lk, :] += jnp.maximum(s, 0.0) * wit[pl.ds(h, 1), cols]
            return carry

        lax.fori_loop(0, N_HEADS, idx_head, 0, unroll=4)

        if i == 0:
            bias[0:tq, :] = jnp.where(causal, 0.0, NEG_INF)
        else:
            sc[c0:lk, :] = jnp.where(causal, sc[c0:lk, :], NEG_INF)

            def store_bias(v):
                bias[0:lk, :] = v

            _topk_bias(store_bias, lambda: sc[0:lk, :], lambda: lax.broadcasted_iota(I32, (lk, tq), 0),
                       0, (1, tq), TOPK_A, (lk - 1).bit_length(), coarse_ref=sc16)

        blk = lambda c: slice(c * tq, (c + 1) * tq)

        def store_head(h, o):
            ot[pl.ds(pl.multiple_of(h * HEAD_DIM, HEAD_DIM), HEAD_DIM), :] = o

        def dsa_logits(h):
            q_t = qa_ref[pl.ds(pl.multiple_of(h * HEAD_DIM, HEAD_DIM), HEAD_DIM), cols]
            return lambda c: _dot(kab[blk(c), :], q_t) + bias[blk(c), :]

        _pipelined_heads(i + 1, tq, dsa_logits, lambda h, c: vat[:, blk(c)], lgb, store_head)
        oa_ref[cols, :] = ot[...].T.astype(BF16)

        def moba_logits(h):
            n = h // G_B
            q_t = qb_ref[pl.ds(pl.multiple_of(h * HEAD_DIM, HEAD_DIM), HEAD_DIM), cols]
            block_bias = [None] * i
            if i > TOPK_B:
                gate = _dot(kmh[n], q_t) + _dot(kml[n], q_t)
                rows = lax.broadcasted_iota(I32, gate.shape, 0)
                for j in range(i):
                    gj = gate[j:j + 1, :]
                    beats = jnp.where(rows < j, jnp.where(gate >= gj, 1, 0), jnp.where(gate > gj, 1, 0))
                    rank = jnp.sum(jnp.where(rows < i, beats, 0), axis=0, keepdims=True)
                    block_bias[j] = jnp.where(rank < TOPK_B, 0.0, NEG_INF)

            def logits(c):
                lg = _dot(kbb[n, blk(c), :], q_t)
                if c == i:
                    return jnp.where(causal, lg, NEG_INF)
                return lg if block_bias[c] is None else lg + block_bias[c]

            return logits

        _pipelined_heads(i + 1, tq, moba_logits, lambda h, c: vbt[h // G_B, :, blk(c)], lgb, store_head)
        ob_ref[cols, :] = ot[...].T.astype(BF16)


def _attn_prompt(qi_t, qa_t, qb_t, ki_t, ka_t, kb_t, va_t, vb_t, wi_t):
    n_seq, _, seq = qi_t.shape
    tq = MOBA_BLOCK
    fm = lambda w: pl.BlockSpec((None, w, seq), lambda b: (b, 0, 0))
    row = lambda w: pl.BlockSpec((seq, w), lambda b: (b, 0))
    return pl.pallas_call(
        _attn_prompt_kernel,
        grid=(n_seq,),
        in_specs=[fm(W_Q), fm(W_Q), fm(W_Q), fm(HEAD_DIM), fm(HEAD_DIM), fm(2 * HEAD_DIM),
                  fm(HEAD_DIM), fm(2 * HEAD_DIM), fm(N_HEADS)],
        out_specs=(row(W_Q), row(W_Q)),
        out_shape=(jax.ShapeDtypeStruct((n_seq * seq, W_Q), BF16),
                   jax.ShapeDtypeStruct((n_seq * seq, W_Q), BF16)),
        scratch_shapes=[
            pltpu.VMEM((seq, HEAD_DIM), BF16),
            pltpu.VMEM((seq, HEAD_DIM), BF16),
            pltpu.VMEM((HKV_B, seq, HEAD_DIM), BF16),
            pltpu.VMEM((V_ROWS, seq), BF16),
            pltpu.VMEM((HKV_B, V_ROWS, seq), BF16),
            pltpu.VMEM((HKV_B, 16, HEAD_DIM), BF16),
            pltpu.VMEM((HKV_B, 16, HEAD_DIM), BF16),
            pltpu.VMEM((seq, tq), F32),
            pltpu.VMEM((seq, tq), BF16),
            pltpu.VMEM((seq, tq), F32),
            pltpu.VMEM((2, seq, tq), F32),
            pltpu.VMEM((W_Q, tq), F32),
        ],
        compiler_params=pltpu.CompilerParams(
            dimension_semantics=("arbitrary",), vmem_limit_bytes=VMEM_LIMIT),
        name="attn_prompt",
    )(qi_t, qa_t, qb_t, ki_t, ka_t, kb_t, va_t, vb_t, wi_t)


SAMPLE_GROUP = 4


def _attn_sample_kernel(pt_ref, cki_hbm, cka_hbm, cva_hbm, ckb_hbm, cvb_hbm,
                        new_ref,
                        qi_ref, qa_ref, qb_ref, wc_ref,
                        oa_ref, ob_ref,
                        ski, ska, sva, skb, svb, scores, bias, sem,
                        *, group, n_new, past):
    g = pl.program_id(0)
    n_groups = pl.num_programs(0)
    n_pages = past // PAGE
    lpad = ski.shape[3]
    slot = g % 2
    pools = ((cki_hbm, ski), (cka_hbm, ska), (cva_hbm, sva), (ckb_hbm, skb), (cvb_hbm, svb))

    def page_copies(grp, to_slot, t):
        s = t // n_pages
        p = t % n_pages
        page = pt_ref[grp * group + s, p]
        lanes = pl.ds(pl.multiple_of(p * PAGE, PAGE), PAGE)
        return [pltpu.make_async_copy(hbm.at[page], slab.at[to_slot, s, :, lanes], sem.at[to_slot])
                for hbm, slab in pools]

    def start_group(grp, to_slot):
        def body(t, carry):
            for cp in page_copies(grp, to_slot, t):
                cp.start()
            return carry
        lax.fori_loop(0, group * n_pages, body, 0)

    def wait_group(to_slot):
        for _, slab in pools:
            filled = slab.at[to_slot, :, :, 0:past]
            pltpu.make_async_copy(filled, filled, sem.at[to_slot]).wait()

    @pl.when(g == 0)
    def _():
        start_group(0, 0)

    wait_group(slot)

    n_starts = group * n_pages
    assert n_starts % N_VALUE_PASSES == 0

    def prefetch_next(it):
        for k in range(n_starts // N_VALUE_PASSES):
            for cp in page_copies(g + 1, 1 - slot, it * (n_starts // N_VALUE_PASSES) + k):
                cp.start()

    n_blk = past // MOBA_BLOCK
    col = lax.broadcasted_iota(I32, (8, lpad), 1)
    qrow = lax.broadcasted_iota(I32, (8, lpad), 0)
    visible8 = col <= past + jnp.minimum(qrow, n_new - 1)
    for s in range(group):
        r0 = 0
        for _, slab in pools:
            w = slab.shape[2]
            slab[slot, s, :, past:lpad] = jnp.zeros((w, lpad - past), F32)
            slab[slot, s, :, past:past + n_new] = new_ref[s, r0:r0 + w, :]
            r0 += w
    st = jnp.concatenate([_dot(qi_ref[s], ski[slot, s].astype(BF16)) for s in range(group)], axis=0)
    st = jnp.maximum(st, 0.0) * jnp.concatenate([wc_ref[s] for s in range(group)], axis=0)
    for s in range(group):
        per_q = [jnp.sum(st[(s * n_new + q) * N_HEADS:(s * n_new + q + 1) * N_HEADS, :], axis=0, keepdims=True)
                 for q in range(n_new)]
        score = jnp.concatenate(per_q + [per_q[-1]] * (8 - n_new), axis=0)
        scores[s * 8:(s + 1) * 8, :] = jnp.where(visible8, score, NEG_INF)

    rows = group * 8

    def store_bias(v):
        bias[...] = v

    for has_next, side_work in ((g + 1 < n_groups, prefetch_next), (g + 1 == n_groups, None)):
        @pl.when(has_next)
        def _():
            _topk_bias(store_bias, lambda: scores[...], lambda: lax.broadcasted_iota(I32, (rows, lpad), 1),
                       1, (rows, 1), TOPK_A, (lpad - 1).bit_length(), side_work=side_work)

    n_rows = n_new * N_HEADS
    all_rows = group * n_rows
    seq_rows = lambda s: slice(s * n_rows, (s + 1) * n_rows)

    def softmax_rows(lg):
        m = jnp.max(lg, axis=1, keepdims=True)
        p = jnp.exp2(lg - m)
        return p.astype(BF16), jnp.sum(p, axis=1, keepdims=True)

    lg = jnp.concatenate([_dot(qa_ref[s], ska[slot, s].astype(BF16)) for s in range(group)], axis=0)
    mask_rows = jnp.concatenate(
        [jnp.broadcast_to(bias[s * 8 + q:s * 8 + q + 1, :], (N_HEADS, lpad))
         for s in range(group) for q in range(n_new)], axis=0)
    p, l = softmax_rows(lg + mask_rows)
    for s in range(group):
        oa_ref[s] = _dot_nt(p[seq_rows(s), :], sva[slot, s].astype(BF16)) / l[seq_rows(s), :]

    lgb = jnp.concatenate([_dot(qb_ref[s], skb[slot, s].astype(BF16)) for s in range(group)], axis=0)
    gate = [jnp.sum(lgb[:, j * MOBA_BLOCK:(j + 1) * MOBA_BLOCK], axis=1, keepdims=True) for j in range(n_blk)]
    pieces = []
    for j in range(n_blk):
        rank = jnp.zeros((all_rows, 1), I32)
        for j2 in range(n_blk):
            if j2 != j:
                rank = rank + jnp.where((gate[j2] >= gate[j]) if j2 < j else (gate[j2] > gate[j]), 1, 0)
        pieces.append(lgb[:, j * MOBA_BLOCK:(j + 1) * MOBA_BLOCK] + jnp.where(rank < TOPK_B, 0.0, NEG_INF))
    colr = lax.broadcasted_iota(I32, (all_rows, LANES), 1)
    query = (lax.broadcasted_iota(I32, (all_rows, LANES), 0) % n_rows) // N_HEADS
    pieces.append(jnp.where(colr <= query, lgb[:, past:lpad], NEG_INF))
    p, l = softmax_rows(jnp.concatenate(pieces, axis=1))
    for s in range(group):
        ob_ref[s] = _dot_nt(p[seq_rows(s), :], svb[slot, s].astype(BF16)) / l[seq_rows(s), :]


def _attn_sample(page_table, cki, cka, cva, ckb, cvb, new_rows, qi, qa, qb2, wcol, group):
    n_seq, n_pages = page_table.shape
    n_new = new_rows.shape[2]
    new_w = new_rows.shape[1]
    assert new_w == 3 * HEAD_DIM + 2 * 2 * HEAD_DIM
    n_rows = n_new * N_HEADS
    assert n_rows % 16 == 0
    past = n_pages * PAGE
    lpad = past + LANES
    hbm = pl.BlockSpec(memory_space=pl.ANY)
    grp = lambda a, b: pl.BlockSpec((group, a, b), lambda g, pt: (g, 0, 0))
    slab = lambda w: pltpu.VMEM((2, group, w, lpad), F32)
    grid_spec = pltpu.PrefetchScalarGridSpec(
        num_scalar_prefetch=1,
        grid=(n_seq // group,),
        in_specs=[hbm, hbm, hbm, hbm, hbm,
                  grp(new_w, n_new),
                  grp(n_rows, HEAD_DIM), grp(n_rows, HEAD_DIM), grp(n_rows, 2 * HEAD_DIM), grp(n_rows, 1)],
        out_specs=(grp(n_rows, HEAD_DIM), grp(n_rows, 2 * HEAD_DIM)),
        scratch_shapes=[
            slab(HEAD_DIM),
            slab(HEAD_DIM),
            slab(HEAD_DIM),
            slab(2 * HEAD_DIM),
            slab(2 * HEAD_DIM),
            pltpu.VMEM((group * 8, lpad), F32),
            pltpu.VMEM((group * 8, lpad), F32),
            pltpu.SemaphoreType.DMA((2,)),
        ],
    )
    return pl.pallas_call(
        functools.partial(_attn_sample_kernel, group=group, n_new=n_new, past=past),
        grid_spec=grid_spec,
        out_shape=(jax.ShapeDtypeStruct((n_seq, n_rows, HEAD_DIM), F32),
                   jax.ShapeDtypeStruct((n_seq, n_rows, 2 * HEAD_DIM), F32)),
        compiler_params=pltpu.CompilerParams(
            dimension_semantics=("arbitrary",), vmem_limit_bytes=VMEM_LIMIT),
        name="attn_sample",
    )(page_table, cki, cka, cva, ckb, cvb, new_rows, qi, qa, qb2, wcol)


def _merge_ffn_kernel(x_ref, oa_ref, ob_ref, sga_ref, sgb_ref, wa_ref, wb_ref, wo_ref,
                      g_ref, wg_ref, wu_ref, wf_ref, o_ref):
    m = sga_ref[...] * _dot(oa_ref[...], wa_ref[...]) + sgb_ref[...] * _dot(ob_ref[...], wb_ref[...])
    x = x_ref[...] + _dot(m.astype(BF16), wo_ref[...])
    o_ref[...] = _half_swiglu(x, g_ref, wg_ref, wu_ref, wf_ref)


def _merge_ffn(x, oa, ob, sga, sgb, wa, wb, wo, g, wg, wu, wf, tm):
    t = x.shape[0]
    row = lambda w: pl.BlockSpec((tm, w), lambda i: (i, 0))
    return pl.pallas_call(
        _merge_ffn_kernel,
        grid=(t // tm,),
        in_specs=[row(D_MODEL), row(W_Q), row(W_Q), row(D_MODEL), row(D_MODEL),
                  _const_spec((W_Q, D_MODEL)), _const_spec((W_Q, D_MODEL)), _const_spec((D_MODEL, D_MODEL)),
                  _const_spec((1, D_MODEL)), _const_spec((D_MODEL, D_FF_PAD)), _const_spec((D_MODEL, D_FF_PAD)),
                  _const_spec((D_FF_PAD, D_MODEL))],
        out_specs=row(D_MODEL),
        out_shape=jax.ShapeDtypeStruct((t, D_MODEL), F32),
        compiler_params=pltpu.CompilerParams(
            dimension_semantics=("arbitrary",), vmem_limit_bytes=VMEM_LIMIT),
        name="merge_ffn",
    )(x, oa, ob, sga, sgb, wa, wb, wo, g, wg, wu, wf)


def _ffn_weights(w_in, w_out):
    pad = D_FF_PAD - D_FF
    wg = jnp.pad(w_in[:, :D_FF], ((0, 0), (0, pad))).astype(BF16)
    wu = jnp.pad(w_in[:, D_FF:], ((0, 0), (0, pad))).astype(BF16)
    wo = jnp.pad(w_out, ((0, pad), (0, 0))).astype(BF16)
    return wg, wu, wo


def _rope_tables(pos):
    half = HEAD_DIM // 2
    inv = ROPE_THETA ** (-jnp.arange(half, dtype=F32) / half)
    ang = pos.astype(F32)[:, None] * inv[None, :]
    cos, sin = jnp.cos(ang), jnp.sin(ang)
    return (jnp.concatenate([cos, cos, cos, cos], axis=1),
            jnp.concatenate([-sin, sin, -sin, sin], axis=1))


def _head_rows(a, n_seq, n_new):
    return a.reshape(n_seq, n_new * N_HEADS, a.shape[1] // N_HEADS)


def _from_head_rows(a, n_seq, n_new):
    return a.reshape(n_seq * n_new, N_HEADS * a.shape[2])


def kernel(x_prompt, x_sample, cache_k_a, cache_v_a, cache_kidx_a, cache_k_b, cache_v_b, page_table,
           ffn1_norm, ffn1_w_in, ffn1_w_out, mix_norm, w_in, q_norm_a, k_norm_a, q_norm_b, k_norm_b,
           gate_bias, w_up_a, w_up_b, w_out, ffn2_norm, ffn2_w_in, ffn2_w_out):
    n_seq, seq, _ = x_prompt.shape
    n_dec, n_new, _ = x_sample.shape
    depth = ffn1_norm.shape[0]
    n_pool = cache_k_a.shape[1]
    past = page_table.shape[1] * PAGE
    assert depth == 1 and seq % MOBA_BLOCK == 0 and past % MOBA_BLOCK == 0 and n_new <= 8
    assert seq >= 4 * TOPK_A and past + n_new >= 4 * TOPK_A
    l = 0
    tp, ts = n_seq * seq, n_dec * n_new

    xp = x_prompt.reshape(tp, D_MODEL)
    xs = x_sample.reshape(ts, D_MODEL)

    f1 = _ffn_weights(ffn1_w_in[l], ffn1_w_out[l])
    f2 = _ffn_weights(ffn2_w_in[l], ffn2_w_out[l])
    w = w_in[l]
    o = np.cumsum([0, W_Q, HEAD_DIM, HEAD_DIM, W_Q, HEAD_DIM, N_HEADS, W_Q, 2 * HEAD_DIM, 2 * HEAD_DIM,
                   D_MODEL, D_MODEL])
    c = lambda k: w[:, o[k]:o[k + 1]]
    z = lambda n: jnp.zeros((D_MODEL, n), F32)
    wp = jnp.concatenate([c(0), c(6), c(7), c(1), z(64), c(3), c(4), z(64), c(8), c(2), c(5), z(56),
                          c(9), c(10)], axis=1).astype(BF16)
    gain = jnp.concatenate([jnp.tile(q_norm_a[l], N_HEADS), jnp.tile(q_norm_b[l], N_HEADS),
                            jnp.tile(k_norm_b[l], HKV_B), k_norm_a[l], jnp.zeros((64,), F32)])[None, :]
    bias = gate_bias[l][None, :]
    lane = np.arange(2 * LANES)
    bd =jnp.asarray((lane[:, None] // HEAD_DIM) == (lane[None, :] // HEAD_DIM), BF16)
    cos_p, sin_p = _rope_tables(jnp.arange(seq, dtype=I32))
    cos_s, sin_s = _rope_tables(past + (jnp.arange(ts, dtype=I32) % n_new))
    wa, wb, wo = w_up_a[l].astype(BF16), w_up_b[l].astype(BF16), w_out[l].astype(BF16)
    g1, gm, g2 = ffn1_norm[l][None, :], mix_norm[l][None, :], ffn2_norm[l][None, :]

    tm = 512
    xp1 = _ffn(xp, g1, *f1, tm)
    (qa_t, qb_t, qi_t, kb_t, ka_t, ki_t, vb_t, va_t, wi_t, sga_p, sgb_p) = _proj(
        xp1, gm, wp, gain, bias, cos_p, sin_p, bd, tm, seq)
    oa_p, ob_p = _attn_prompt(qi_t, qa_t, qb_t, ki_t, ka_t, kb_t, va_t, vb_t, wi_t)
    yp = _merge_ffn(xp1, oa_p, ob_p, sga_p, sgb_p, wa, wb, wo, g2, *f2, tm)

    xs1 = _ffn(xs, g1, *f1, ts)
    (qa_s, qb_s, qi_s, kb_s, ka_s, ki_s, vb_s, va_s, wi_s, sga_s, sgb_s) = _proj(
        xs1, gm, wp, gain, bias, cos_s, sin_s, bd, ts, None)
    qi_r = _head_rows(qi_s, n_dec, n_new)
    qa_r = _head_rows(qa_s, n_dec, n_new)
    qb_r = _head_rows(qb_s, n_dec, n_new)
    zq = jnp.zeros_like(qb_r)
    first = ((jnp.arange(n_new * N_HEADS) % N_HEADS) < G_B)[None, :, None]
    qb2 = jnp.concatenate([jnp.where(first, qb_r, zq), jnp.where(first, zq, qb_r)], axis=2)
    wcol = _head_rows(wi_s, n_dec, n_new)
    pages_t = lambda a: jnp.swapaxes(a[l].reshape(n_pool, PAGE, -1), 1, 2)
    new_rows = jnp.concatenate([ki_s, ka_s, va_s, kb_s, vb_s], axis=1)
    new_rows = jnp.swapaxes(new_rows.reshape(n_dec, n_new, new_rows.shape[1]), 1, 2)
    oa_r, ob_r = _attn_sample(
        page_table,
        pages_t(cache_kidx_a), pages_t(cache_k_a), pages_t(cache_v_a), pages_t(cache_k_b), pages_t(cache_v_b),
        new_rows, qi_r, qa_r, qb2, wcol, SAMPLE_GROUP)
    oa_s = _from_head_rows(oa_r, n_dec, n_new).astype(BF16)
    ob_sel = jnp.where(first, ob_r[:, :, :HEAD_DIM], ob_r[:, :, HEAD_DIM:])
    ob_s = _from_head_rows(ob_sel, n_dec, n_new).astype(BF16)
    ys = _merge_ffn(xs1, oa_s, ob_s, sga_s, sgb_s, wa, wb, wo, g2, *f2, ts)

    d = depth
    tok = lambda a: jnp.swapaxes(a, 1, 2)
    return (yp.reshape(n_seq, seq, D_MODEL), ys.reshape(n_dec, n_new, D_MODEL),
            tok(ka_t).reshape(d, n_seq, seq, 1, HEAD_DIM), tok(va_t).reshape(d, n_seq, seq, 1, HEAD_DIM),
            tok(ki_t).reshape(d, n_seq, seq, HEAD_DIM),
            tok(kb_t).reshape(d, n_seq, seq, HKV_B, HEAD_DIM), tok(vb_t).reshape(d, n_seq, seq, HKV_B, HEAD_DIM),
            ka_s.reshape(d, n_dec, n_new, 1, HEAD_DIM), va_s.reshape(d, n_dec, n_new, 1, HEAD_DIM),
            ki_s.reshape(d, n_dec, n_new, HEAD_DIM),
            kb_s.reshape(d, n_dec, n_new, HKV_B, HEAD_DIM), vb_s.reshape(d, n_dec, n_new, HKV_B, HEAD_DIM))
```
